```python
import math
import jax, jax.numpy as jnp
from jax import lax
import numpy as np

D_MODEL = 2048
BATCH = 4
SEQ = 2048
DEPTH = 4
DEC_BATCH = 128
DEC_SEQ = 1
PAST_LEN = 16384
PAGE_SIZE = 128

MIX_W = D_MODEL // 2
N_BRANCH = 3
HEAD_A = 64
W_A = MIX_W
H_A = W_A // HEAD_A
LORA_DECAY = max(32, int(round(1.8 * math.sqrt(D_MODEL) / 32)) * 32)
LORA_ICLR = max(32, int(round(1.8 * math.sqrt(D_MODEL) / 32)) * 32)
GN_EPS_A = 64e-5
A_SHIFT_W = 3 * W_A + LORA_DECAY + LORA_ICLR
W_B = MIX_W
LRU_BLOCKS = 16
LRU_BS = W_B // LRU_BLOCKS
LRU_C = 8.0
CONV_W = 4
DK = 128
DV = 128
H_C = MIX_W // DV
W_C = H_C * DV
QKV_W = 2 * H_C * DK + H_C * DV
CHUNK = 64
IN_SIZES = (A_SHIFT_W, W_A, W_B, W_B, QKV_W, H_C, H_C, W_C, N_BRANCH * D_MODEL)
IN_COLS = int(sum(IN_SIZES))
IN_SPLITS = tuple(int(s) for s in np.cumsum(IN_SIZES)[:-1])
NORM_EPS = 1e-6

kernel_name = 'hybrid_rwkv7_rglru_gdn_step'


def _rmsnorm(x, g, eps=NORM_EPS):
    xf = x.astype(jnp.float32)
    y = xf * lax.rsqrt(jnp.mean(xf * xf, axis=-1, keepdims=True) + eps)
    return (y * g.astype(jnp.float32)).astype(x.dtype)


def _l2norm(x, eps=1e-6):
    xf = x.astype(jnp.float32)
    return xf * lax.rsqrt(jnp.sum(xf * xf, axis=-1, keepdims=True) + eps)


def _causal_dwconv(x, buf, w):
    xx = jnp.concatenate([buf.astype(x.dtype), x], axis=1)
    y = lax.conv_general_dilated(xx, w[:, None, :].astype(x.dtype), window_strides=(1,), padding='VALID',
                                 dimension_numbers=('NWC', 'WIO', 'NWC'), feature_group_count=x.shape[-1])
    return y, xx[:, -(CONV_W - 1):]


def _rwkv7_mixer(u, gate, shift_prev, S0, mix, w0, w2, a0, a2, k_k, k_a, r_k, lnx_g, lnx_b):
    B, T, _ = u.shape
    f32 = jnp.float32
    prev = jnp.concatenate([shift_prev[:, None].astype(u.dtype), u[:, :-1]], axis=1)
    z = u + (prev - u) * mix
    r, k, v, wd, ad = jnp.split(z, (W_A, 2 * W_A, 3 * W_A, 3 * W_A + LORA_DECAY), axis=-1)
    logw = -jax.nn.softplus(-(w0 + jnp.tanh(wd) @ w2)) - 0.5
    decay = jnp.exp(-jnp.exp(logw.astype(f32)))
    a = jax.nn.sigmoid((a0 + ad @ a2).astype(f32))
    def heads(t):
        return t.astype(f32).reshape(B, T, H_A, HEAD_A)
    r, k, v, decay, a = heads(r), heads(k), heads(v), heads(decay), heads(a)
    kk = _l2norm(k * k_k.reshape(H_A, HEAD_A))
    k = k * (1.0 + (a - 1.0) * k_a.reshape(H_A, HEAD_A))

    def step(S, inp):
        r_t, k_t, v_t, w_t, kk_t, b_t = inp
        sa = jnp.einsum('bhvk,bhk->bhv', S, kk_t)
        S = S * w_t[:, :, None, :] - sa[..., None] * b_t[:, :, None, :] + v_t[..., None] * k_t[:, :, None, :]
        return S, jnp.einsum('bhvk,bhk->bhv', S, r_t)

    xs = tuple(jnp.moveaxis(t, 1, 0) for t in (r, k, v, decay, kk, kk * a))
    S, y = lax.scan(step, S0.astype(f32), xs)
    y = jnp.moveaxis(y, 0, 1)
    mu = jnp.mean(y, axis=-1, keepdims=True)
    var = jnp.mean(jnp.square(y - mu), axis=-1, keepdims=True)
    y = ((y - mu) * lax.rsqrt(var + GN_EPS_A)).reshape(B, T, W_A) * lnx_g + lnx_b
    bonus = jnp.sum(r * k * r_k, axis=-1, keepdims=True) * v
    y = (y + bonus.reshape(B, T, W_A)) * jax.nn.silu(gate.astype(f32))
    return y.astype(u.dtype), u[:, -1], S


def _rglru_mixer(xb, gate, buf, h0, conv_w, conv_b, wa, ba, wx, bx, lam):
    B, T, _ = xb.shape
    f32 = jnp.float32
    xc, buf_new = _causal_dwconv(xb, buf, conv_w)
    xc = (xc + conv_b).astype(f32)
    xblk = xc.reshape(B, T, LRU_BLOCKS, LRU_BS)
    r = jax.nn.sigmoid(jnp.einsum('btni,nij->btnj', xblk, wa).reshape(B, T, W_B) + ba)
    i = jax.nn.sigmoid(jnp.einsum('btni,nij->btnj', xblk, wx).reshape(B, T, W_B) + bx)
    log_a = -LRU_C * r * jax.nn.softplus(-lam.astype(f32))
    a = jnp.exp(log_a)
    b = jnp.sqrt(-jnp.expm1(2.0 * log_a)) * (i * xc)

    def combine(lhs, rhs):
        a1, b1 = lhs
        a2, b2 = rhs
        return a1 * a2, a2 * b1 + b2

    a_cum, h = lax.associative_scan(combine, (a, b), axis=1)
    h = h + a_cum * h0.astype(f32)[:, None]
    y = h * jax.nn.silu(gate.astype(f32))
    return y.astype(xb.dtype), buf_new, h[:, -1]


def _chunk_gated_delta(q, k, v, g, beta, S0):
    B, T = q.shape[0], q.shape[1]
    n_chunks = -(-T // CHUNK)
    pad = n_chunks * CHUNK - T

    def prep(t):
        t = jnp.pad(t, [(0, 0), (0, pad)] + [(0, 0)] * (t.ndim - 2))
        t = t.reshape((B, n_chunks, CHUNK) + t.shape[2:])
        return jnp.moveaxis(t, 3, 1)

    q, k, v, g, beta = prep(q), prep(k), prep(v), prep(g), prep(beta)
    gc = jnp.cumsum(g, axis=-1)
    idx = jnp.arange(CHUNK)
    incl = idx[:, None] >= idx[None, :]
    strict = idx[:, None] > idx[None, :]
    decay = jnp.exp(jnp.where(incl, gc[..., :, None] - gc[..., None, :], -jnp.inf))
    kb = k * beta[..., None]
    A = jnp.where(strict, jnp.einsum('bhnid,bhnjd->bhnij', kb, k) * decay, 0.0)
    eye = jnp.eye(CHUNK, dtype=jnp.float32)
    rhs = jnp.concatenate([v * beta[..., None], kb * jnp.exp(gc)[..., None]], axis=-1)
    sol = lax.linalg.triangular_solve(A + eye, rhs, left_side=True, lower=True, unit_diagonal=True)
    u_val, w_kd = sol[..., :DV], sol[..., DV:]
    qk = jnp.where(incl, jnp.einsum('bhnid,bhnjd->bhnij', q, k) * decay, 0.0)
    q_dec = q * jnp.exp(gc)[..., None]
    k_dec = k * jnp.exp(gc[..., -1:] - gc)[..., None]
    g_last = jnp.exp(gc[..., -1])

    def step(S, inp):
        qd, kd, u, w, qkc, gl = inp
        v_new = u - jnp.einsum('bhcd,bhde->bhce', w, S)
        o = jnp.einsum('bhcd,bhde->bhce', qd, S) + jnp.einsum('bhij,bhje->bhie', qkc, v_new)
        S = S * gl[..., None, None] + jnp.einsum('bhcd,bhce->bhde', kd, v_new)
        return S, o

    xs = tuple(jnp.moveaxis(t, 2, 0) for t in (q_dec, k_dec, u_val, w_kd, qk, g_last))
    S, o = lax.scan(step, S0, xs)
    o = jnp.transpose(o, (1, 0, 3, 2, 4)).reshape(B, n_chunks * CHUNK, H_C, DV)[:, :T]
    return o, S


def _gated_deltanet_mixer(qkv, a_in, b_in, z, buf, S0, conv_w, a_log, dt_bias, norm_g):
    B, T, _ = qkv.shape
    f32 = jnp.float32
    qkv_c, buf_new = _causal_dwconv(qkv, buf, conv_w)
    qkv_c = jax.nn.silu(qkv_c.astype(f32))
    q, k, v = jnp.split(qkv_c, (H_C * DK, 2 * H_C * DK), axis=-1)
    q = _l2norm(q.reshape(B, T, H_C, DK)) * (DK ** -0.5)
    k = _l2norm(k.reshape(B, T, H_C, DK))
    v = v.reshape(B, T, H_C, DV)
    beta = jax.nn.sigmoid(b_in.astype(f32))
    g = -jnp.exp(a_log.astype(f32)) * jax.nn.softplus(a_in.astype(f32) + dt_bias)
    o, S = _chunk_gated_delta(q, k, v, g, beta, S0.astype(f32))
    o = _rmsnorm(o, norm_g) * jax.nn.silu(z.astype(f32).reshape(B, T, H_C, DV))
    return o.reshape(B, T, W_C).astype(qkv.dtype), buf_new, S


def _layer(x, c, st, lp):
    shift_prev, wkv0, lru_buf, lru_h0, gdn_buf, gdn_S0 = st
    B, T, _ = x.shape
    mod = jax.nn.silu(c) @ lp['ada_w'] + lp['ada_b']
    m_shift, m_scale, m_gate = jnp.split(mod[:, None], 3, axis=-1)
    h = _rmsnorm(x, lp['norm_g']) * (1.0 + m_scale) + m_shift
    u = h @ lp['w_in']
    a_sh, a_gate, b_x, b_gate, c_qkv, c_a, c_b, c_z, merge = jnp.split(u, IN_SPLITS, axis=-1)
    y_a, shift_new, wkv_new = _rwkv7_mixer(a_sh, a_gate, shift_prev, wkv0, lp['rwkv_mix'], lp['rwkv_w0'],
                                           lp['rwkv_w2'], lp['rwkv_a0'], lp['rwkv_a2'], lp['rwkv_kk'],
                                           lp['rwkv_ka'], lp['rwkv_rk'], lp['rwkv_lnx_g'], lp['rwkv_lnx_b'])
    y_b, lru_buf_new, lru_h_new = _rglru_mixer(b_x, b_gate, lru_buf, lru_h0, lp['lru_conv_w'], lp['lru_conv_b'],
                                               lp['lru_wa'], lp['lru_ba'], lp['lru_wx'], lp['lru_bx'],
                                               lp['lru_lambda'])
    y_c, gdn_buf_new, gdn_S_new = _gated_deltanet_mixer(c_qkv, c_a, c_b, c_z, gdn_buf, gdn_S0, lp['gdn_conv_w'],
                                                        lp['gdn_a_log'], lp['gdn_dt_bias'], lp['gdn_norm_g'])
    ys = jnp.stack([y_a, y_b, y_c], axis=2)
    proj = jnp.einsum('btmw,mwd->btmd', ys, lp['w_branch'])
    gates = jax.nn.sigmoid(merge.reshape(B, T, N_BRANCH, D_MODEL).astype(jnp.float32))
    merged = jnp.sum(gates * proj, axis=2).astype(x.dtype)
    x = x + m_gate * (merged @ lp['w_out'])
    return x, (shift_new, wkv_new, lru_buf_new, lru_h_new, gdn_buf_new, gdn_S_new)


def _trunk(x, c, states, params, final_g):
    collected = tuple([] for _ in states)
    for l in range(DEPTH):
        lp = {name: arr[l] for name, arr in params.items()}
        x, new = _layer(x, c, tuple(s[l] for s in states), lp)
        for lst, s in zip(collected, new):
            lst.append(s)
    return _rmsnorm(x, final_g), tuple(jnp.stack(lst) for lst in collected)


def _zero_states(batch, dtype):
    f32 = jnp.float32
    return (jnp.zeros((DEPTH, batch, A_SHIFT_W), dtype),
            jnp.zeros((DEPTH, batch, H_A, HEAD_A, HEAD_A), f32),
            jnp.zeros((DEPTH, batch, CONV_W - 1, W_B), dtype),
            jnp.zeros((DEPTH, batch, W_B), f32),
            jnp.zeros((DEPTH, batch, CONV_W - 1, QKV_W), dtype),
            jnp.zeros((DEPTH, batch, H_C, DK, DV), f32))


def setup_inputs(seed: int = 0) -> dict:
    key = jax.random.key(seed)
    keys = jax.random.split(key, 64)
    count = [0]

    def nk():
        count[0] += 1
        return keys[count[0] - 1]

    f32 = jnp.float32

    def nrm(shape, scale):
        return jax.random.normal(nk(), shape, f32) * scale

    def unif(shape, lo, hi):
        return jax.random.uniform(nk(), shape, f32, lo, hi)

    L = DEPTH
    p_lru = unif((L, W_B), 0.9, 0.999)
    dt = jnp.exp(unif((L, H_C), math.log(1e-3), math.log(1e-1)))
    return {
        'x_prompt': nrm((BATCH, SEQ, D_MODEL), 1.0),
        'x_sample': nrm((DEC_BATCH, DEC_SEQ, D_MODEL), 1.0),
        'state_rwkv_shift': nrm((L, DEC_BATCH, A_SHIFT_W), 1.0),
        'state_rwkv_wkv': nrm((L, DEC_BATCH, H_A, HEAD_A, HEAD_A), 0.1),
        'state_lru_conv': nrm((L, DEC_BATCH, CONV_W - 1, W_B), 1.0),
        'state_lru_h': nrm((L, DEC_BATCH, W_B), 0.5),
        'state_gdn_conv': nrm((L, DEC_BATCH, CONV_W - 1, QKV_W), 1.0),
        'state_gdn_ssm': nrm((L, DEC_BATCH, H_C, DK, DV), 0.1),
        'c_prompt': nrm((BATCH, D_MODEL), 1.0),
        'c_sample': nrm((DEC_BATCH, D_MODEL), 1.0),
        'ada_w': nrm((L, D_MODEL, 3 * D_MODEL), D_MODEL ** -0.5),
        'ada_b': nrm((L, 3 * D_MODEL), 0.02),
        'norm_g': 1.0 + nrm((L, D_MODEL), 0.02),
        'w_in': nrm((L, D_MODEL, IN_COLS), D_MODEL ** -0.5),
        'rwkv_mix': unif((L, A_SHIFT_W), 0.0, 1.0),
        'rwkv_w0': unif((L, W_A), -6.5, -1.5),
        'rwkv_w2': nrm((L, LORA_DECAY, W_A), LORA_DECAY ** -0.5),
        'rwkv_a0': nrm((L, W_A), 0.1),
        'rwkv_a2': nrm((L, LORA_ICLR, W_A), LORA_ICLR ** -0.5),
        'rwkv_kk': 0.85 + nrm((L, W_A), 0.02),
        'rwkv_ka': 1.0 + nrm((L, W_A), 0.02),
        'rwkv_rk': nrm((L, H_A, HEAD_A), 0.1),
        'rwkv_lnx_g': 1.0 + nrm((L, W_A), 0.02),
        'rwkv_lnx_b': nrm((L, W_A), 0.02),
        'lru_conv_w': nrm((L, CONV_W, W_B), CONV_W ** -0.5),
        'lru_conv_b': nrm((L, W_B), 0.02),
        'lru_wa': nrm((L, LRU_BLOCKS, LRU_BS, LRU_BS), LRU_BS ** -0.5),
        'lru_ba': nrm((L, W_B), 0.02),
        'lru_wx': nrm((L, LRU_BLOCKS, LRU_BS, LRU_BS), LRU_BS ** -0.5),
        'lru_bx': nrm((L, W_B), 0.02),
        'lru_lambda': jnp.log(p_lru) - jnp.log1p(-p_lru),
        'gdn_conv_w': nrm((L, CONV_W, QKV_W), CONV_W ** -0.5),
        'gdn_a_log': jnp.log(unif((L, H_C), 1.0, 16.0)),
        'gdn_dt_bias': dt + jnp.log(-jnp.expm1(-dt)),
        'gdn_norm_g': 1.0 + nrm((L, DV), 0.02),
        'w_branch': nrm((L, N_BRANCH, MIX_W, D_MODEL), MIX_W ** -0.5),
        'w_out': nrm((L, D_MODEL, D_MODEL), D_MODEL ** -0.5),
        'final_g': 1.0 + nrm((D_MODEL,), 0.02),
    }


def reference(x_prompt, x_sample, state_rwkv_shift, state_rwkv_wkv, state_lru_conv, state_lru_h,
              state_gdn_conv, state_gdn_ssm, c_prompt, c_sample, ada_w, ada_b, norm_g, w_in,
              rwkv_mix, rwkv_w0, rwkv_w2, rwkv_a0, rwkv_a2, rwkv_kk, rwkv_ka, rwkv_rk, rwkv_lnx_g, rwkv_lnx_b,
              lru_conv_w, lru_conv_b, lru_wa, lru_ba, lru_wx, lru_bx, lru_lambda,
              gdn_conv_w, gdn_a_log, gdn_dt_bias, gdn_norm_g, w_branch, w_out, final_g):
    params = dict(ada_w=ada_w, ada_b=ada_b, norm_g=norm_g, w_in=w_in,
                  rwkv_mix=rwkv_mix, rwkv_w0=rwkv_w0, rwkv_w2=rwkv_w2, rwkv_a0=rwkv_a0, rwkv_a2=rwkv_a2,
                  rwkv_kk=rwkv_kk, rwkv_ka=rwkv_ka, rwkv_rk=rwkv_rk, rwkv_lnx_g=rwkv_lnx_g, rwkv_lnx_b=rwkv_lnx_b,
                  lru_conv_w=lru_conv_w, lru_conv_b=lru_conv_b, lru_wa=lru_wa, lru_ba=lru_ba,
                  lru_wx=lru_wx, lru_bx=lru_bx, lru_lambda=lru_lambda,
                  gdn_conv_w=gdn_conv_w, gdn_a_log=gdn_a_log, gdn_dt_bias=gdn_dt_bias, gdn_norm_g=gdn_norm_g,
                  w_branch=w_branch, w_out=w_out)
    y_prompt, (p_rwkv_shift, p_rwkv_wkv, p_lru_conv, p_lru_h, p_gdn_conv, p_gdn_ssm) = _trunk(
        x_prompt, c_prompt, _zero_states(x_prompt.shape[0], x_prompt.dtype), params, final_g)
    y_sample, (s_rwkv_shift, s_rwkv_wkv, s_lru_conv, s_lru_h, s_gdn_conv, s_gdn_ssm) = _trunk(
        x_sample, c_sample,
        (state_rwkv_shift, state_rwkv_wkv, state_lru_conv, state_lru_h, state_gdn_conv, state_gdn_ssm),
        params, final_g)
    return (y_prompt, y_sample,
            p_rwkv_shift, p_rwkv_wkv, p_lru_conv, p_lru_h, p_gdn_conv, p_gdn_ssm,
            s_rwkv_shift, s_rwkv_wkv, s_lru_conv, s_lru_h, s_gdn_conv, s_gdn_ssm)
```

```python
import functools
import math

import jax
import jax.numpy as jnp
import numpy as np
from jax import lax
from jax.experimental import pallas as pl
from jax.experimental.pallas import tpu as pltpu

F32 = jnp.float32
BF16 = jnp.bfloat16
HIGHEST = lax.Precision.HIGHEST

D_MODEL = 2048
DEPTH = 4
MIX_W = 1024
HEAD_A = 64
H_A = 16
LORA = 96
GN_EPS_A = 64e-5
A_SHIFT_W = 3 * MIX_W + 2 * LORA
LRU_BLOCKS = 16
LRU_BS = 64
LRU_C = 8.0
CONV_W = 4
DK = 128
DV = 128
H_C = 8
QKV_W = 3 * MIX_W
NORM_EPS = 1e-6
IN_COLS = 16592

LANE = 128
SUBLANE = 8

OFF_R, OFF_K, OFF_V = 0, 1024, 2048
OFF_AG = 3072
OFF_BX, OFF_BG = 4096, 5120
OFF_Q = 6144
OFF_Z = 9216
OFF_MERGE = 10240
OFF_WD, OFF_AD, OFF_AB = 16384, 16512, 16640
N_PAD = 16896
SHIFT_PAD = 3 * MIX_W + 2 * LANE

RWKV_CHUNK = 64
GDN_CHUNK = 128
VMEM_LIMIT = 56 * 1024 * 1024


def _cparams(sem):
    return pltpu.CompilerParams(dimension_semantics=sem, vmem_limit_bytes=VMEM_LIMIT)


def _sigmoid(x):
    return 1.0 / (1.0 + jnp.exp(-x))


def _silu(x):
    return x * _sigmoid(x)


def _softplus(x):
    return jnp.maximum(x, 0.0) + jnp.log1p(jnp.exp(-jnp.abs(x)))


def _mm(a, b):
    return jnp.dot(a, b, preferred_element_type=F32, precision=HIGHEST)


def _mm_nt(a, b):
    return lax.dot_general(a, b, (((1,), (1,)), ((), ())), preferred_element_type=F32, precision=HIGHEST)


def _mm_tn(a, b):
    return lax.dot_general(a, b, (((0,), (0,)), ((), ())), preferred_element_type=F32, precision=HIGHEST)


def _mm_bf16(a, b):
    return jnp.dot(a.astype(BF16), b.astype(BF16), preferred_element_type=F32)


def _iota2(n, m):
    return (lax.broadcasted_iota(jnp.int32, (n, m), 0), lax.broadcasted_iota(jnp.int32, (n, m), 1))


def _inv_unit_lower(a, n, levels):
    row, col = _iota2(n, n)
    eye = (row == col).astype(F32)
    same = (row >> 4) == (col >> 4)
    d = jnp.where(same, a, 0.0)
    o = a - d
    nd = -d
    x2 = _mm(nd, nd)
    x4 = _mm(x2, x2)
    x8 = _mm(x4, x4)
    td = _mm(_mm(_mm(eye + nd, eye + x2), eye + x4), eye + x8)
    p = -_mm(td, o)
    t = eye + p
    for _ in range(levels - 1):
        p = _mm(p, p)
        t = _mm(t, eye + p)
    return _mm(t, td)


def _ada_kernel(c_ref, w_ref, b_ref, o_ref):
    h = _silu(c_ref[...]).astype(BF16)
    o_ref[...] = jnp.dot(h, w_ref[...].astype(BF16), preferred_element_type=F32) + b_ref[...]


def _ada_all(c_all, ada_w, ada_b, tn=512):
    rows = c_all.shape[0]
    depth = ada_w.shape[0]
    n = ada_w.shape[2]
    return pl.pallas_call(
        _ada_kernel,
        out_shape=jax.ShapeDtypeStruct((depth, rows, n), F32),
        grid=(depth, n // tn),
        in_specs=[pl.BlockSpec((rows, D_MODEL), lambda l, j: (0, 0)),
                  pl.BlockSpec((None, D_MODEL, tn), lambda l, j: (l, 0, j)),
                  pl.BlockSpec((None, 1, tn), lambda l, j: (l, 0, j))],
        out_specs=pl.BlockSpec((None, rows, tn), lambda l, j: (l, 0, j)),
        compiler_params=_cparams(("parallel", "parallel")),
        name="ada_mod",
    )(c_all, ada_w, ada_b.reshape(depth, 1, n))


def _inproj_kernel(x_ref, shift_ref, scale_ref, g_ref, w_ref, o_ref, h_ref):
    @pl.when(pl.program_id(1) == 0)
    def _():
        x = x_ref[...]
        y = x * lax.rsqrt(jnp.mean(x * x, axis=-1, keepdims=True) + NORM_EPS) * g_ref[...]
        h_ref[...] = (y * (1.0 + scale_ref[...]) + shift_ref[...]).astype(BF16)

    o_ref[...] = jnp.dot(h_ref[...], w_ref[...], preferred_element_type=F32)


def _inproj(x2d, mod, norm_g3, w_in_p, l, rows_per_mod, tm, tn):
    m = x2d.shape[0]
    n = w_in_p.shape[2]
    if rows_per_mod > 1:
        def mspec(part):
            return pl.BlockSpec((None, None, 1, D_MODEL), lambda i, j: (l, (i * tm) // rows_per_mod, 0, part))
    else:
        def mspec(part):
            return pl.BlockSpec((None, tm, D_MODEL), lambda i, j: (l, i, part))
    return pl.pallas_call(
        _inproj_kernel,
        out_shape=jax.ShapeDtypeStruct((m, n), F32),
        grid=(m // tm, n // tn),
        in_specs=[pl.BlockSpec((tm, D_MODEL), lambda i, j: (i, 0)),
                  mspec(0), mspec(1),
                  pl.BlockSpec((None, 1, D_MODEL), lambda i, j: (l, 0, 0)),
                  pl.BlockSpec((None, D_MODEL, tn), lambda i, j: (l, 0, j))],
        out_specs=pl.BlockSpec((tm, tn), lambda i, j: (i, j)),
        scratch_shapes=[pltpu.VMEM((tm, D_MODEL), BF16)],
        compiler_params=_cparams(("parallel", "arbitrary")),
        name="inproj",
    )(x2d, mod, mod, norm_g3, w_in_p)


def _merge_kernel(ya_ref, yb_ref, yc_ref, ga_ref, gb_ref, gc_ref, wa_ref, wb_ref, wc_ref, o_ref):
    acc = _sigmoid(ga_ref[...]) * jnp.dot(ya_ref[...].astype(BF16), wa_ref[...], preferred_element_type=F32)
    acc += _sigmoid(gb_ref[...]) * jnp.dot(yb_ref[...].astype(BF16), wb_ref[...], preferred_element_type=F32)
    acc += _sigmoid(gc_ref[...]) * jnp.dot(yc_ref[...].astype(BF16), wc_ref[...], preferred_element_type=F32)
    o_ref[...] = acc.astype(BF16)


def _merge(ya, yb, yc, u, w_branch_b, l, tm, tn):
    m = ya.shape[0]
    yspec = pl.BlockSpec((tm, MIX_W), lambda i, j: (i, 0))

    def gspec(br):
        return pl.BlockSpec((tm, tn), lambda i, j: (i, (OFF_MERGE + br * D_MODEL) // tn + j))

    def wspec(br):
        return pl.BlockSpec((None, None, MIX_W, tn), lambda i, j: (l, br, 0, j))

    return pl.pallas_call(
        _merge_kernel,
        out_shape=jax.ShapeDtypeStruct((m, D_MODEL), BF16),
        grid=(m // tm, D_MODEL // tn),
        in_specs=[yspec, yspec, yspec, gspec(0), gspec(1), gspec(2), wspec(0), wspec(1), wspec(2)],
        out_specs=pl.BlockSpec((tm, tn), lambda i, j: (i, j)),
        compiler_params=_cparams(("parallel", "parallel")),
        name="branch_merge",
    )(ya, yb, yc, u, u, u, w_branch_b, w_branch_b, w_branch_b)


def _outproj_kernel(m_ref, w_ref, x_ref, gate_ref, o_ref):
    o_ref[...] = x_ref[...] + gate_ref[...] * jnp.dot(m_ref[...], w_ref[...], preferred_element_type=F32)


def _outproj(merged, w_out_b, x2d, mod, l, rows_per_mod, tm, tn):
    m = x2d.shape[0]
    gate_blk = (2 * D_MODEL) // tn
    if rows_per_mod > 1:
        gspec = pl.BlockSpec((None, None, 1, tn), lambda i, j: (l, (i * tm) // rows_per_mod, 0, gate_blk + j))
    else:
        gspec = pl.BlockSpec((None, tm, tn), lambda i, j: (l, i, gate_blk + j))
    return pl.pallas_call(
        _outproj_kernel,
        out_shape=jax.ShapeDtypeStruct((m, D_MODEL), F32),
        grid=(m // tm, D_MODEL // tn),
        in_specs=[pl.BlockSpec((tm, D_MODEL), lambda i, j: (i, 0)),
                  pl.BlockSpec((None, D_MODEL, tn), lambda i, j: (l, 0, j)),
                  pl.BlockSpec((tm, tn), lambda i, j: (i, j)),
                  gspec],
        out_specs=pl.BlockSpec((tm, tn), lambda i, j: (i, j)),
        compiler_params=_cparams(("parallel", "parallel")),
        name="outproj",
    )(merged, w_out_b, x2d, mod)


def _final_norm_kernel(x_ref, g_ref, o_ref):
    x = x_ref[...]
    o_ref[...] = x * lax.rsqrt(jnp.mean(x * x, axis=-1, keepdims=True) + NORM_EPS) * g_ref[...]


def _final_norm(x2d, g, tm):
    m = x2d.shape[0]
    return pl.pallas_call(
        _final_norm_kernel,
        out_shape=jax.ShapeDtypeStruct((m, D_MODEL), F32),
        grid=(m // tm,),
        in_specs=[pl.BlockSpec((tm, D_MODEL), lambda i: (i, 0)), pl.BlockSpec((1, D_MODEL), lambda i: (0, 0))],
        out_specs=pl.BlockSpec((tm, D_MODEL), lambda i: (i, 0)),
        compiler_params=_cparams(("parallel",)),
        name="final_norm",
    )(x2d, g.reshape(1, D_MODEL))


PV_MIX_R, PV_MIX_K, PV_MIX_V, PV_W0, PV_A0, PV_KK, PV_KA, PV_RK, PV_LNG, PV_LNB = range(10)


def _half_sum(x, lo_mask):
    s_lo = jnp.sum(jnp.where(lo_mask, x, 0.0), axis=-1, keepdims=True)
    s_hi = jnp.sum(jnp.where(lo_mask, 0.0, x), axis=-1, keepdims=True)
    return jnp.where(lo_mask, s_lo, s_hi)


def _rwkv_prompt_kernel(r_ref, k_ref, v_ref, g_ref, wd_ref, ad_ref, pv_ref, mixl_ref, w2_ref, a2_ref,
                        sh_ref, s0_ref, y_ref, sout_ref, xbuf, sb, *, n_chunks):
    c = RWKV_CHUNK
    ti = pl.program_id(2)
    lo = lax.broadcasted_iota(jnp.int32, (1, LANE), 1) < HEAD_A

    @pl.when(ti == 0)
    def _():
        xbuf[0:SUBLANE, :] = sh_ref[...]
        zero = jnp.zeros((HEAD_A, HEAD_A), F32)
        top = jnp.concatenate([s0_ref[0], zero], axis=1)
        bot = jnp.concatenate([zero, s0_ref[1]], axis=1)
        sb[...] = jnp.concatenate([top, bot], axis=0)

    cur = jnp.concatenate([r_ref[...], k_ref[...], v_ref[...], wd_ref[...], ad_ref[...]], axis=1)
    xbuf[SUBLANE:SUBLANE + c, :] = cur
    prev = xbuf[pl.ds(SUBLANE - 1, c), :]
    pv = pv_ref[...]
    mixl = mixl_ref[...]
    mix = jnp.concatenate([pv[PV_MIX_R:PV_MIX_R + 1], pv[PV_MIX_K:PV_MIX_K + 1], pv[PV_MIX_V:PV_MIX_V + 1],
                           mixl[0:1], mixl[1:2]], axis=1)
    z = cur + (prev - cur) * mix
    xbuf[0:SUBLANE, :] = cur[c - SUBLANE:c, :]

    zr, zk, zv = z[:, 0:LANE], z[:, LANE:2 * LANE], z[:, 2 * LANE:3 * LANE]
    zwd, zad = z[:, 3 * LANE:4 * LANE], z[:, 4 * LANE:5 * LANE]
    w0, a0 = pv[PV_W0:PV_W0 + 1], pv[PV_A0:PV_A0 + 1]
    k_k, k_a, r_k = pv[PV_KK:PV_KK + 1], pv[PV_KA:PV_KA + 1], pv[PV_RK:PV_RK + 1]
    ln_g, ln_b = pv[PV_LNG:PV_LNG + 1], pv[PV_LNB:PV_LNB + 1]

    logw = -_softplus(-(w0 + _mm(jnp.tanh(zwd), w2_ref[...]))) - 0.5
    lw = -jnp.exp(logw)
    a = _sigmoid(a0 + _mm(zad, a2_ref[...]))
    kkp = zk * k_k
    kk = kkp * lax.rsqrt(_half_sum(kkp * kkp, lo) + 1e-6)
    kmod = zk * (1.0 + (a - 1.0) * k_a)
    bt = kk * a

    rowc, colc = _iota2(c, c)
    gcum = _mm((rowc >= colc).astype(F32), lw)
    g_end = gcum[c - 1:c, :]
    e_pos, e_neg = jnp.exp(gcum), jnp.exp(-gcum)
    e_prev, e_end = jnp.exp(gcum - lw), jnp.exp(g_end - gcum)

    def stack(x):
        return jnp.concatenate([jnp.where(lo, x, 0.0), jnp.where(lo, 0.0, x)], axis=0)

    rt_s, kb_s, bb_s = stack(zr * e_pos), stack(kmod * e_neg), stack(bt * e_neg)
    kt_s, kend_s, bend_s = stack(kk * e_prev), stack(kmod * e_end), stack(bt * e_end)
    v_s = stack(zv)

    n = 2 * c
    p_all = _mm_nt(jnp.concatenate([kt_s, rt_s], axis=0), jnp.concatenate([bb_s, kb_s], axis=0))
    row, col = _iota2(n, n)
    same_head = (row >> 6) == (col >> 6)
    strict = same_head & (row > col)
    incl = same_head & (row >= col)
    a_ub = jnp.where(strict, p_all[0:n, 0:n], 0.0)
    a_vk = jnp.where(strict, p_all[0:n, n:2 * n], 0.0)
    p_rb = jnp.where(incl, p_all[n:2 * n, 0:n], 0.0)
    p_rk = jnp.where(incl, p_all[n:2 * n, n:2 * n], 0.0)

    t_inv = _inv_unit_lower(a_ub, n, levels=2)
    av = _mm(a_vk, v_s)
    m12 = _mm(t_inv, jnp.concatenate([kt_s, av], axis=1))
    m1, m2 = m12[:, 0:LANE], m12[:, LANE:2 * LANE]
    w2m = _mm(p_rb, m12)
    q1 = rt_s - w2m[:, 0:LANE]
    q2 = _mm(p_rk, v_s) - w2m[:, LANE:2 * LANE]
    phi = jnp.where(row == col, jnp.exp(g_end), 0.0) - _mm_tn(bend_s, m1)
    psi_t = _mm_tn(jnp.concatenate([v_s, m2], axis=0), jnp.concatenate([kend_s, -bend_s], axis=0))

    s_old = sb[...]
    y_s = _mm_nt(q1, s_old) + q2
    s_new = _mm_nt(s_old, phi) + psi_t
    sb[...] = s_new

    y2 = y_s[0:c, :] + y_s[c:n, :]
    mu = _half_sum(y2, lo) * (1.0 / HEAD_A)
    dlt = y2 - mu
    var = _half_sum(dlt * dlt, lo) * (1.0 / HEAD_A)
    yn = dlt * lax.rsqrt(var + GN_EPS_A) * ln_g + ln_b
    bonus = _half_sum(zr * kmod * r_k, lo) * zv
    y_ref[...] = ((yn + bonus) * _silu(g_ref[...])).astype(y_ref.dtype)

    @pl.when(ti == n_chunks - 1)
    def _():
        sout_ref[0] = s_new[0:HEAD_A, 0:HEAD_A]
        sout_ref[1] = s_new[HEAD_A:2 * HEAD_A, HEAD_A:2 * HEAD_A]


def _rwkv_prompt(u, nb, t, pvec, mixl, w2p, a2p, sh_init, s0):
    c = RWKV_CHUNK
    nt = t // c
    hp = H_A // 2

    def uspec(off, per_pair=True):
        if per_pair:
            return pl.BlockSpec((c, LANE), lambda b, p, i: (b * nt + i, off // LANE + p))
        return pl.BlockSpec((c, LANE), lambda b, p, i: (b * nt + i, off // LANE))

    pair_cols = lambda rows: pl.BlockSpec((rows, LANE), lambda b, p, i: (0, p))
    return pl.pallas_call(
        functools.partial(_rwkv_prompt_kernel, n_chunks=nt),
        out_shape=(jax.ShapeDtypeStruct((nb * t, MIX_W), BF16),
                   jax.ShapeDtypeStruct((nb, H_A, HEAD_A, HEAD_A), F32)),
        grid=(nb, hp, nt),
        in_specs=[uspec(OFF_R), uspec(OFF_K), uspec(OFF_V), uspec(OFF_AG),
                  uspec(OFF_WD, False), uspec(OFF_AD, False),
                  pair_cols(16),
                  pl.BlockSpec((SUBLANE, LANE), lambda b, p, i: (0, 0)),
                  pair_cols(LANE), pair_cols(LANE),
                  pl.BlockSpec((None, None, SUBLANE, 5 * LANE), lambda b, p, i: (b, p, 0, 0)),
                  pl.BlockSpec((None, 2, HEAD_A, HEAD_A), lambda b, p, i: (b, p, 0, 0))],
        out_specs=(pl.BlockSpec((c, LANE), lambda b, p, i: (b * nt + i, p)),
                   pl.BlockSpec((None, 2, HEAD_A, HEAD_A), lambda b, p, i: (b, p, 0, 0))),
        scratch_shapes=[pltpu.VMEM((c + SUBLANE, 5 * LANE), F32), pltpu.VMEM((2 * HEAD_A, 2 * HEAD_A), F32)],
        compiler_params=_cparams(("parallel", "parallel", "arbitrary")),
        name="rwkv_prompt",
    )(u, u, u, u, u, u, pvec, mixl, w2p, a2p, sh_init, s0)


def _rwkv_step_kernel(rkv_ref, g_ref, wd_ref, ad_ref, prev_ref, mix_ref, pv_ref, w2_ref, a2_ref, s_ref,
                      y_ref, sout_ref, hs, *, bb):
    cur = jnp.concatenate([rkv_ref[...], wd_ref[...], ad_ref[...]], axis=1)
    z = cur + (prev_ref[...] - cur) * mix_ref[...]
    zr, zk, zv = z[:, 0:MIX_W], z[:, MIX_W:2 * MIX_W], z[:, 2 * MIX_W:3 * MIX_W]
    zwd, zad = z[:, 3 * MIX_W:3 * MIX_W + LANE], z[:, 3 * MIX_W + LANE:]
    pv = pv_ref[...]
    w0, a0 = pv[PV_W0:PV_W0 + 1], pv[PV_A0:PV_A0 + 1]
    k_k, k_a = pv[PV_KK:PV_KK + 1], pv[PV_KA:PV_KA + 1]
    logw = -_softplus(-(w0 + _mm(jnp.tanh(zwd), w2_ref[...]))) - 0.5
    decay = jnp.exp(-jnp.exp(logw))
    a = _sigmoid(a0 + _mm(zad, a2_ref[...]))
    kkp = zk * k_k
    kmod = zk * (1.0 + (a - 1.0) * k_a)
    gate = _silu(g_ref[...])
    for q, arr in enumerate((zr, decay, kkp, a, kmod, zv, gate)):
        for h in range(H_A):
            hs[q, h] = arr[:, h * HEAD_A:(h + 1) * HEAD_A]

    row64, col64 = _iota2(HEAD_A, HEAD_A)
    diag = row64 == col64
    rowb = lax.broadcasted_iota(jnp.int32, (bb, HEAD_A), 0)

    def body(b, outs):
        new_outs = []
        for h in range(H_A):
            r = hs[0, h, pl.ds(b, 1), :]
            w = hs[1, h, pl.ds(b, 1), :]
            kkp_h = hs[2, h, pl.ds(b, 1), :]
            a_h = hs[3, h, pl.ds(b, 1), :]
            km = hs[4, h, pl.ds(b, 1), :]
            v = hs[5, h, pl.ds(b, 1), :]
            gt = hs[6, h, pl.ds(b, 1), :]
            kk = kkp_h * lax.rsqrt(jnp.sum(kkp_h * kkp_h, axis=-1, keepdims=True) + 1e-6)
            bt = kk * a_h
            s = s_ref[b, h]
            sa = jnp.sum(s * kk, axis=-1, keepdims=True)
            vcol = jnp.sum(jnp.where(diag, jnp.broadcast_to(v, (HEAD_A, HEAD_A)), 0.0), axis=-1, keepdims=True)
            s_new = s * w - sa * bt + vcol * km
            sout_ref[b, h] = s_new
            ycol = jnp.sum(s_new * r, axis=-1, keepdims=True)
            y = jnp.sum(jnp.where(diag, jnp.broadcast_to(ycol, (HEAD_A, HEAD_A)), 0.0), axis=0, keepdims=True)
            mu = jnp.sum(y, axis=-1, keepdims=True) * (1.0 / HEAD_A)
            dlt = y - mu
            var = jnp.sum(dlt * dlt, axis=-1, keepdims=True) * (1.0 / HEAD_A)
            sl = slice(h * HEAD_A, (h + 1) * HEAD_A)
            yn = dlt * lax.rsqrt(var + GN_EPS_A) * pv[PV_LNG:PV_LNG + 1, sl] + pv[PV_LNB:PV_LNB + 1, sl]
            bonus = jnp.sum(r * km * pv[PV_RK:PV_RK + 1, sl], axis=-1, keepdims=True) * v
            o = (yn + bonus) * gt
            new_outs.append(jnp.where(rowb == b, jnp.broadcast_to(o, (bb, HEAD_A)), outs[h]))
        return tuple(new_outs)

    outs = lax.fori_loop(0, bb, body, tuple(jnp.zeros((bb, HEAD_A), F32) for _ in range(H_A)))
    y_ref[...] = jnp.concatenate(outs, axis=1)


def _rwkv_step(u, prev_pad, mix_pad, pvec, w2p, a2p, s0, bb=SUBLANE):
    nb = u.shape[0]
    full = lambda shape: pl.BlockSpec(shape, lambda i: tuple(0 for _ in shape))
    return pl.pallas_call(
        functools.partial(_rwkv_step_kernel, bb=bb),
        out_shape=(jax.ShapeDtypeStruct((nb, MIX_W), F32),
                   jax.ShapeDtypeStruct((nb, H_A, HEAD_A, HEAD_A), F32)),
        grid=(nb // bb,),
        in_specs=[pl.BlockSpec((bb, 3 * MIX_W), lambda i: (i, 0)),
                  pl.BlockSpec((bb, MIX_W), lambda i: (i, OFF_AG // MIX_W)),
                  pl.BlockSpec((bb, LANE), lambda i: (i, OFF_WD // LANE)),
                  pl.BlockSpec((bb, LANE), lambda i: (i, OFF_AD // LANE)),
                  pl.BlockSpec((bb, SHIFT_PAD), lambda i: (i, 0)),
                  full((1, SHIFT_PAD)), full((16, MIX_W)), full((LANE, MIX_W)), full((LANE, MIX_W)),
                  pl.BlockSpec((bb, H_A, HEAD_A, HEAD_A), lambda i: (i, 0, 0, 0))],
        out_specs=(pl.BlockSpec((bb, MIX_W), lambda i: (i, 0)),
                   pl.BlockSpec((bb, H_A, HEAD_A, HEAD_A), lambda i: (i, 0, 0, 0))),
        scratch_shapes=[pltpu.VMEM((7, H_A, bb, HEAD_A), F32)],
        compiler_params=_cparams(("parallel",)),
        name="rwkv_step",
    )(u, u, u, u, prev_pad, mix_pad, pvec, w2p, a2p, s0)


LP_CONVB, LP_BA, LP_BX, LP_LAM = range(4)


def _lru_gates(xc, gate, wa_ref, wx_ref, lp):
    r = _sigmoid(_mm_bf16(xc, wa_ref[...]) + lp[LP_BA:LP_BA + 1])
    i = _sigmoid(_mm_bf16(xc, wx_ref[...]) + lp[LP_BX:LP_BX + 1])
    log_a = -LRU_C * r * _softplus(-lp[LP_LAM:LP_LAM + 1])
    a = jnp.exp(log_a)
    b = jnp.sqrt(jnp.tanh(-log_a) * (a * a + 1.0)) * (i * xc)
    return a, b


def _lru_prompt_kernel(x_ref, g_ref, cw_ref, lp_ref, wa_ref, wx_ref, cinit_ref, h0_ref,
                       y_ref, hout_ref, xbuf, hcar, *, n_blocks, tc):
    ti = pl.program_id(1)

    @pl.when(ti == 0)
    def _():
        xbuf[0:SUBLANE, :] = cinit_ref[...]
        hcar[...] = h0_ref[...]

    x = x_ref[...]
    xbuf[SUBLANE:SUBLANE + tc, :] = x
    cw = cw_ref[...]
    lp = lp_ref[...]
    xc = (xbuf[pl.ds(SUBLANE - 3, tc), :] * cw[0:1] + xbuf[pl.ds(SUBLANE - 2, tc), :] * cw[1:2]
          + xbuf[pl.ds(SUBLANE - 1, tc), :] * cw[2:3] + x * cw[3:4] + lp[LP_CONVB:LP_CONVB + 1])
    xbuf[0:SUBLANE, :] = x[tc - SUBLANE:tc, :]
    a, b = _lru_gates(xc, None, wa_ref, wx_ref, lp)
    rows = lax.broadcasted_iota(jnp.int32, (tc, MIX_W), 0)
    s = 1
    while s < tc:
        a_sh = pltpu.roll(a, s, 0)
        b_sh = pltpu.roll(b, s, 0)
        m = rows >= s
        b = jnp.where(m, a * b_sh + b, b)
        a = jnp.where(m, a * a_sh, a)
        s *= 2
    h = b + a * hcar[...]
    hcar[...] = h[tc - 1:tc, :]
    y_ref[...] = (h * _silu(g_ref[...])).astype(y_ref.dtype)

    @pl.when(ti == n_blocks - 1)
    def _():
        hout_ref[...] = h[tc - 1:tc, :]


def _lru_prompt(u, nb, t, conv_w, lp, wa_d, wx_d, cinit, h0, tc=256):
    nt = t // tc
    const = lambda shape: pl.BlockSpec(shape, lambda b, i: tuple(0 for _ in shape))
    return pl.pallas_call(
        functools.partial(_lru_prompt_kernel, n_blocks=nt, tc=tc),
        out_shape=(jax.ShapeDtypeStruct((nb * t, MIX_W), BF16),
                   jax.ShapeDtypeStruct((nb, 1, MIX_W), F32)),
        grid=(nb, nt),
        in_specs=[pl.BlockSpec((tc, MIX_W), lambda b, i: (b * nt + i, OFF_BX // MIX_W)),
                  pl.BlockSpec((tc, MIX_W), lambda b, i: (b * nt + i, OFF_BG // MIX_W)),
                  const((CONV_W, MIX_W)), const((SUBLANE, MIX_W)),
                  const((MIX_W, MIX_W)), const((MIX_W, MIX_W)),
                  pl.BlockSpec((None, SUBLANE, MIX_W), lambda b, i: (b, 0, 0)),
                  pl.BlockSpec((None, 1, MIX_W), lambda b, i: (b, 0, 0))],
        out_specs=(pl.BlockSpec((tc, MIX_W), lambda b, i: (b * nt + i, 0)),
                   pl.BlockSpec((None, 1, MIX_W), lambda b, i: (b, 0, 0))),
        scratch_shapes=[pltpu.VMEM((tc + SUBLANE, MIX_W), F32), pltpu.VMEM((1, MIX_W), F32)],
        compiler_params=_cparams(("parallel", "arbitrary")),
        name="lru_prompt",
    )(u, u, conv_w, lp, wa_d, wx_d, cinit, h0)


def _lru_step_kernel(x_ref, g_ref, buf_ref, h0_ref, cw_ref, lp_ref, wa_ref, wx_ref, y_ref, hout_ref):
    x = x_ref[...]
    cw = cw_ref[...]
    lp = lp_ref[...]
    xc = buf_ref[0] * cw[0:1] + buf_ref[1] * cw[1:2] + buf_ref[2] * cw[2:3] + x * cw[3:4] + lp[LP_CONVB:LP_CONVB + 1]
    a, b = _lru_gates(xc, None, wa_ref, wx_ref, lp)
    h = a * h0_ref[...] + b
    hout_ref[...] = h
    y_ref[...] = h * _silu(g_ref[...])


def _lru_step(u, buf_t, h0, conv_w, lp, wa_d, wx_d):
    nb = u.shape[0]
    const = lambda shape: pl.BlockSpec(shape, lambda i: tuple(0 for _ in shape))
    return pl.pallas_call(
        _lru_step_kernel,
        out_shape=(jax.ShapeDtypeStruct((nb, MIX_W), F32), jax.ShapeDtypeStruct((nb, MIX_W), F32)),
        grid=(1,),
        in_specs=[pl.BlockSpec((nb, MIX_W), lambda i: (0, OFF_BX // MIX_W)),
                  pl.BlockSpec((nb, MIX_W), lambda i: (0, OFF_BG // MIX_W)),
                  const((CONV_W - 1, nb, MIX_W)), const((nb, MIX_W)),
                  const((CONV_W, MIX_W)), const((SUBLANE, MIX_W)),
                  const((MIX_W, MIX_W)), const((MIX_W, MIX_W))],
        out_specs=(const((nb, MIX_W)), const((nb, MIX_W))),
        compiler_params=_cparams(("arbitrary",)),
        name="lru_step",
    )(u, u, buf_t, h0, conv_w, lp, wa_d, wx_d)


GP_ALOG, GP_DTB, GP_NORM = range(3)


def _gdn_prompt_kernel(q_ref, k_ref, v_ref, z_ref, ab_ref, cwq_ref, cwk_ref, cwv_ref, gp_ref,
                       ciq_ref, cik_ref, civ_ref, s0_ref, y_ref, sout_ref, xbuf, st, *, n_chunks):
    c = GDN_CHUNK
    h = pl.program_id(1)
    ti = pl.program_id(2)

    @pl.when(ti == 0)
    def _():
        xbuf[0:SUBLANE, :] = jnp.concatenate([ciq_ref[...], cik_ref[...], civ_ref[...]], axis=1)
        st[...] = s0_ref[...]

    cur = jnp.concatenate([q_ref[...], k_ref[...], v_ref[...]], axis=1)
    xbuf[SUBLANE:SUBLANE + c, :] = cur
    cw = jnp.concatenate([cwq_ref[...], cwk_ref[...], cwv_ref[...]], axis=1)
    conv = (xbuf[pl.ds(SUBLANE - 3, c), :] * cw[0:1] + xbuf[pl.ds(SUBLANE - 2, c), :] * cw[1:2]
            + xbuf[pl.ds(SUBLANE - 1, c), :] * cw[2:3] + cur * cw[3:4])
    xbuf[0:SUBLANE, :] = cur[c - SUBLANE:c, :]
    act = _silu(conv)
    qh, kh, vh = act[:, 0:DK], act[:, DK:2 * DK], act[:, 2 * DK:]
    qn = qh * lax.rsqrt(jnp.sum(qh * qh, axis=-1, keepdims=True) + 1e-6) * (DK ** -0.5)
    kn = kh * lax.rsqrt(jnp.sum(kh * kh, axis=-1, keepdims=True) + 1e-6)

    gp = gp_ref[...]
    lane_c = lax.broadcasted_iota(jnp.int32, (c, LANE), 1)
    lane_1 = lax.broadcasted_iota(jnp.int32, (1, LANE), 1)
    ab = ab_ref[...]
    a_in = jnp.sum(jnp.where(lane_c == h, ab, 0.0), axis=-1, keepdims=True)
    b_in = jnp.sum(jnp.where(lane_c == h + H_C, ab, 0.0), axis=-1, keepdims=True)
    a_log = jnp.sum(jnp.where(lane_1 == h, gp[GP_ALOG:GP_ALOG + 1], 0.0), axis=-1, keepdims=True)
    dt_b = jnp.sum(jnp.where(lane_1 == h, gp[GP_DTB:GP_DTB + 1], 0.0), axis=-1, keepdims=True)
    beta = _sigmoid(b_in)
    g = -jnp.exp(a_log) * _softplus(a_in + dt_b)

    row, col = _iota2(c, c)
    incl = row >= col
    strict = row > col
    gc = _mm(incl.astype(F32), jnp.broadcast_to(g, (c, LANE)))
    gc_row = jnp.sum(jnp.where(row == col, gc, 0.0), axis=0, keepdims=True)
    decay = jnp.where(incl, jnp.exp(jnp.where(incl, gc - gc_row, 0.0)), 0.0)
    g_last = gc[c - 1:c, :]
    eg = jnp.exp(gc)

    kb = kn * beta
    p_all = _mm_nt(jnp.concatenate([kb, qn], axis=0), kn)
    a_mat = jnp.where(strict, p_all[0:c] * decay, 0.0)
    qk = p_all[c:2 * c] * decay
    t_inv = _inv_unit_lower(a_mat, c, levels=3)
    wu = _mm(t_inv, jnp.concatenate([kb * eg, vh * beta], axis=1))
    k_dec = kn * jnp.exp(g_last - gc)
    w3 = _mm_tn(k_dec, wu)
    phi = jnp.where(row == col, jnp.exp(g_last), 0.0) - w3[:, 0:DK]
    psi = w3[:, DK:]
    w4 = _mm(qk, wu)
    q1 = qn * eg - w4[:, 0:DK]
    q2 = w4[:, DK:]

    s_old = st[...]
    o = _mm(q1, s_old) + q2
    s_new = _mm(phi, s_old) + psi
    st[...] = s_new

    on = o * lax.rsqrt(jnp.mean(o * o, axis=-1, keepdims=True) + NORM_EPS) * gp[GP_NORM:GP_NORM + 1]
    y_ref[...] = (on * _silu(z_ref[...])).astype(y_ref.dtype)

    @pl.when(ti == n_chunks - 1)
    def _():
        sout_ref[...] = s_new


def _gdn_prompt(u, nb, t, conv_w, gpar, cinit, s0):
    c = GDN_CHUNK
    nt = t // c

    def uspec(off):
        return pl.BlockSpec((c, LANE), lambda b, h, i: (b * nt + i, off // LANE + h))

    def cwspec(off):
        return pl.BlockSpec((CONV_W, LANE), lambda b, h, i: (0, off // LANE + h))

    def cispec(off):
        return pl.BlockSpec((None, SUBLANE, LANE), lambda b, h, i: (b, 0, off // LANE + h))

    return pl.pallas_call(
        functools.partial(_gdn_prompt_kernel, n_chunks=nt),
        out_shape=(jax.ShapeDtypeStruct((nb * t, MIX_W), BF16),
                   jax.ShapeDtypeStruct((nb, H_C, DK, DV), F32)),
        grid=(nb, H_C, nt),
        in_specs=[uspec(OFF_Q), uspec(OFF_Q + MIX_W), uspec(OFF_Q + 2 * MIX_W), uspec(OFF_Z),
                  pl.BlockSpec((c, LANE), lambda b, h, i: (b * nt + i, OFF_AB // LANE)),
                  cwspec(0), cwspec(MIX_W), cwspec(2 * MIX_W),
                  pl.BlockSpec((SUBLANE, LANE), lambda b, h, i: (0, 0)),
                  cispec(0), cispec(MIX_W), cispec(2 * MIX_W),
                  pl.BlockSpec((None, None, DK, DV), lambda b, h, i: (b, h, 0, 0))],
        out_specs=(pl.BlockSpec((c, LANE), lambda b, h, i: (b * nt + i, h)),
                   pl.BlockSpec((None, None, DK, DV), lambda b, h, i: (b, h, 0, 0))),
        scratch_shapes=[pltpu.VMEM((c + SUBLANE, 3 * LANE), F32), pltpu.VMEM((DK, DV), F32)],
        compiler_params=_cparams(("parallel", "parallel", "arbitrary")),
        name="gdn_prompt",
    )(u, u, u, u, u, conv_w, conv_w, conv_w, gpar, cinit, cinit, cinit, s0)


def _gdn_step_kernel(qkv_ref, z_ref, ab_ref, buf_ref, cw_ref, gp_ref, s_ref, y_ref, sout_ref, hs, *, bb):
    cw = cw_ref[...]
    cur = qkv_ref[...]
    act = _silu(buf_ref[0] * cw[0:1] + buf_ref[1] * cw[1:2] + buf_ref[2] * cw[2:3] + cur * cw[3:4])
    gp = gp_ref[...]
    ab = ab_ref[...]
    g_all = -jnp.exp(gp[GP_ALOG:GP_ALOG + 1]) * _softplus(ab + gp[GP_DTB:GP_DTB + 1])
    beta_all = _sigmoid(ab)
    zg = _silu(z_ref[...])
    for h in range(H_C):
        qh = act[:, h * DK:(h + 1) * DK]
        kh = act[:, MIX_W + h * DK:MIX_W + (h + 1) * DK]
        hs[0, h] = qh * lax.rsqrt(jnp.sum(qh * qh, axis=-1, keepdims=True) + 1e-6) * (DK ** -0.5)
        hs[1, h] = kh * lax.rsqrt(jnp.sum(kh * kh, axis=-1, keepdims=True) + 1e-6)
        hs[2, h] = act[:, 2 * MIX_W + h * DV:2 * MIX_W + (h + 1) * DV]
        hs[3, h] = jnp.broadcast_to(jnp.exp(g_all[:, h:h + 1]), (bb, LANE))
        hs[4, h] = jnp.broadcast_to(beta_all[:, H_C + h:H_C + h + 1], (bb, LANE))
        hs[5, h] = zg[:, h * DV:(h + 1) * DV]

    row, col = _iota2(DK, DV)
    diag = row == col
    rowb = lax.broadcasted_iota(jnp.int32, (bb, DV), 0)
    norm_g = gp[GP_NORM:GP_NORM + 1]

    def body(b, outs):
        new_outs = []
        for h in range(H_C):
            q = hs[0, h, pl.ds(b, 1), :]
            k = hs[1, h, pl.ds(b, 1), :]
            v = hs[2, h, pl.ds(b, 1), :]
            eg = hs[3, h, pl.ds(b, 1), :]
            beta = hs[4, h, pl.ds(b, 1), :]
            zg_h = hs[5, h, pl.ds(b, 1), :]
            s = s_ref[b, h]
            kcol = jnp.sum(jnp.where(diag, jnp.broadcast_to(k, (DK, DV)), 0.0), axis=-1, keepdims=True)
            qcol = jnp.sum(jnp.where(diag, jnp.broadcast_to(q, (DK, DV)), 0.0), axis=-1, keepdims=True)
            ks = jnp.sum(kcol * s, axis=0, keepdims=True)
            qs = jnp.sum(qcol * s, axis=0, keepdims=True)
            v_new = beta * (v - eg * ks)
            o = eg * qs + jnp.sum(q * k, axis=-1, keepdims=True) * v_new
            sout_ref[b, h] = s * eg + kcol * v_new
            on = o * lax.rsqrt(jnp.mean(o * o, axis=-1, keepdims=True) + NORM_EPS) * norm_g
            new_outs.append(jnp.where(rowb == b, jnp.broadcast_to(on * zg_h, (bb, DV)), outs[h]))
        return tuple(new_outs)

    outs = lax.fori_loop(0, bb, body, tuple(jnp.zeros((bb, DV), F32) for _ in range(H_C)))
    y_ref[...] = jnp.concatenate(outs, axis=1)


def _gdn_step(u, buf_t, conv_w, gpar, s0, bb=SUBLANE):
    nb = u.shape[0]
    const = lambda shape: pl.BlockSpec(shape, lambda i: tuple(0 for _ in shape))
    return pl.pallas_call(
        functools.partial(_gdn_step_kernel, bb=bb),
        out_shape=(jax.ShapeDtypeStruct((nb, MIX_W), F32),
                   jax.ShapeDtypeStruct((nb, H_C, DK, DV), F32)),
        grid=(nb // bb,),
        in_specs=[pl.BlockSpec((bb, QKV_W), lambda i: (i, OFF_Q // QKV_W)),
                  pl.BlockSpec((bb, MIX_W), lambda i: (i, OFF_Z // MIX_W)),
                  pl.BlockSpec((bb, LANE), lambda i: (i, OFF_AB // LANE)),
                  pl.BlockSpec((CONV_W - 1, bb, QKV_W), lambda i: (0, i, 0)),
                  const((CONV_W, QKV_W)), const((SUBLANE, LANE)),
                  pl.BlockSpec((bb, H_C, DK, DV), lambda i: (i, 0, 0, 0))],
        out_specs=(pl.BlockSpec((bb, MIX_W), lambda i: (i, 0)),
                   pl.BlockSpec((bb, H_C, DK, DV), lambda i: (i, 0, 0, 0))),
        scratch_shapes=[pltpu.VMEM((6, H_C, bb, LANE), F32)],
        compiler_params=_cparams(("parallel",)),
        name="gdn_step",
    )(u, u, u, buf_t, conv_w, gpar, s0)


def _pad_cols(w, width):
    pad = width - w.shape[-1]
    return jnp.pad(w, [(0, 0)] * (w.ndim - 1) + [(0, pad)])


def _layout_in_cols(w):
    a_sh_end = A_SHIFT_W
    o_ag = a_sh_end
    o_bx = o_ag + MIX_W
    o_bg = o_bx + MIX_W
    o_q = o_bg + MIX_W
    o_a = o_q + QKV_W
    o_z = o_a + 2 * H_C
    o_m = o_z + MIX_W
    parts = [w[..., 0:3 * MIX_W], w[..., o_ag:o_bx], w[..., o_bx:o_bg], w[..., o_bg:o_q], w[..., o_q:o_a],
             w[..., o_z:o_m], w[..., o_m:IN_COLS],
             _pad_cols(w[..., 3 * MIX_W:3 * MIX_W + LORA], LANE),
             _pad_cols(w[..., 3 * MIX_W + LORA:a_sh_end], LANE),
             _pad_cols(w[..., o_a:o_z], LANE)]
    out = jnp.concatenate(parts, axis=-1)
    return _pad_cols(out, N_PAD)


def _shift_to_padded(s):
    return jnp.concatenate([s[..., 0:3 * MIX_W], _pad_cols(s[..., 3 * MIX_W:3 * MIX_W + LORA], LANE),
                            _pad_cols(s[..., 3 * MIX_W + LORA:], LANE)], axis=-1)


def _shift_from_u(u_rows):
    return jnp.concatenate([u_rows[..., 0:3 * MIX_W], u_rows[..., OFF_WD:OFF_WD + LORA],
                            u_rows[..., OFF_AD:OFF_AD + LORA]], axis=-1)


def _block_diag(w):
    eye = jnp.eye(LRU_BLOCKS, dtype=w.dtype)
    return jnp.einsum('nij,nm->nimj', w, eye).reshape(MIX_W, MIX_W)


def _prep_layer_params(l, p):
    pvec = jnp.zeros((16, MIX_W), F32)
    rows = [p['rwkv_mix'][l, 0:MIX_W], p['rwkv_mix'][l, MIX_W:2 * MIX_W], p['rwkv_mix'][l, 2 * MIX_W:3 * MIX_W],
            p['rwkv_w0'][l], p['rwkv_a0'][l], p['rwkv_kk'][l], p['rwkv_ka'][l], p['rwkv_rk'][l].reshape(MIX_W),
            p['rwkv_lnx_g'][l], p['rwkv_lnx_b'][l]]
    pvec = pvec.at[0:len(rows)].set(jnp.stack(rows))
    mix_pad = _shift_to_padded(p['rwkv_mix'][l])[None]
    mixl = jnp.zeros((SUBLANE, LANE), F32).at[0].set(mix_pad[0, 3 * MIX_W:3 * MIX_W + LANE])
    mixl = mixl.at[1].set(mix_pad[0, 3 * MIX_W + LANE:])
    w2p = jnp.pad(p['rwkv_w2'][l], ((0, LANE - LORA), (0, 0)))
    a2p = jnp.pad(p['rwkv_a2'][l], ((0, LANE - LORA), (0, 0)))
    lp = jnp.zeros((SUBLANE, MIX_W), F32).at[0:4].set(
        jnp.stack([p['lru_conv_b'][l], p['lru_ba'][l], p['lru_bx'][l], p['lru_lambda'][l]]))
    wa_d = _block_diag(p['lru_wa'][l]).astype(BF16)
    wx_d = _block_diag(p['lru_wx'][l]).astype(BF16)
    gpar = jnp.zeros((SUBLANE, LANE), F32)
    gpar = gpar.at[GP_ALOG, 0:H_C].set(p['gdn_a_log'][l]).at[GP_DTB, 0:H_C].set(p['gdn_dt_bias'][l])
    gpar = gpar.at[GP_NORM].set(p['gdn_norm_g'][l])
    return dict(pvec=pvec, mix_pad=mix_pad, mixl=mixl, w2p=w2p, a2p=a2p, lru_cw=p['lru_conv_w'][l], lp=lp,
                wa_d=wa_d, wx_d=wx_d, gdn_cw=p['gdn_conv_w'][l], gpar=gpar)


def _prompt_layer(x2d, nb, t, l, lp_, mod_p, norm_g3, w_in_p, w_branch_b, w_out_b):
    u = _inproj(x2d, mod_p, norm_g3, w_in_p, l, rows_per_mod=t, tm=1024, tn=768)
    hp = H_A // 2
    ya, wkv = _rwkv_prompt(u, nb, t, lp_['pvec'], lp_['mixl'], lp_['w2p'], lp_['a2p'],
                           jnp.zeros((nb, hp, SUBLANE, 5 * LANE), F32),
                           jnp.zeros((nb, H_A, HEAD_A, HEAD_A), F32))
    yb, h_last = _lru_prompt(u, nb, t, lp_['lru_cw'], lp_['lp'], lp_['wa_d'], lp_['wx_d'],
                             jnp.zeros((nb, SUBLANE, MIX_W), F32), jnp.zeros((nb, 1, MIX_W), F32))
    yc, ssm = _gdn_prompt(u, nb, t, lp_['gdn_cw'], lp_['gpar'],
                          jnp.zeros((nb, SUBLANE, QKV_W), F32), jnp.zeros((nb, H_C, DK, DV), F32))
    merged = _merge(ya, yb, yc, u, w_branch_b, l, tm=512, tn=1024)
    x_new = _outproj(merged, w_out_b, x2d, mod_p, l, rows_per_mod=t, tm=512, tn=1024)
    u3 = u.reshape(nb, t, N_PAD)
    states = (_shift_from_u(u3[:, t - 1]), wkv,
              u3[:, t - (CONV_W - 1):, OFF_BX:OFF_BX + MIX_W], h_last.reshape(nb, MIX_W),
              u3[:, t - (CONV_W - 1):, OFF_Q:OFF_Q + QKV_W], ssm)
    return x_new, states


def _sample_layer(x2d, l, lp_, st, mod_s, norm_g3, w_in_p, w_branch_b, w_out_b):
    shift_prev, wkv0, lru_buf, lru_h0, gdn_buf, gdn_s0 = st
    nb = x2d.shape[0]
    u = _inproj(x2d, mod_s, norm_g3, w_in_p, l, rows_per_mod=1, tm=nb, tn=768)
    ya, wkv = _rwkv_step(u, _shift_to_padded(shift_prev), lp_['mix_pad'], lp_['pvec'], lp_['w2p'], lp_['a2p'], wkv0)
    yb, h_new = _lru_step(u, jnp.swapaxes(lru_buf, 0, 1), lru_h0, lp_['lru_cw'], lp_['lp'], lp_['wa_d'], lp_['wx_d'])
    yc, ssm = _gdn_step(u, jnp.swapaxes(gdn_buf, 0, 1), lp_['gdn_cw'], lp_['gpar'], gdn_s0)
    merged = _merge(ya, yb, yc, u, w_branch_b, l, tm=nb, tn=1024)
    x_new = _outproj(merged, w_out_b, x2d, mod_s, l, rows_per_mod=1, tm=nb, tn=1024)
    states = (_shift_from_u(u), wkv,
              jnp.concatenate([lru_buf[:, 1:], u[:, None, OFF_BX:OFF_BX + MIX_W]], axis=1), h_new,
              jnp.concatenate([gdn_buf[:, 1:], u[:, None, OFF_Q:OFF_Q + QKV_W]], axis=1), ssm)
    return x_new, states


def kernel(x_prompt, x_sample, state_rwkv_shift, state_rwkv_wkv, state_lru_conv, state_lru_h, state_gdn_conv, state_gdn_ssm, c_prompt, c_sample, ada_w, ada_b, norm_g, w_in, rwkv_mix, rwkv_w0, rwkv_w2, rwkv_a0, rwkv_a2, rwkv_kk, rwkv_ka, rwkv_rk, rwkv_lnx_g, rwkv_lnx_b, lru_conv_w, lru_conv_b, lru_wa, lru_ba, lru_wx, lru_bx, lru_lambda, gdn_conv_w, gdn_a_log, gdn_dt_bias, gdn_norm_g, w_branch, w_out, final_g):
    p = dict(rwkv_mix=rwkv_mix, rwkv_w0=rwkv_w0, rwkv_w2=rwkv_w2, rwkv_a0=rwkv_a0, rwkv_a2=rwkv_a2,
             rwkv_kk=rwkv_kk, rwkv_ka=rwkv_ka, rwkv_rk=rwkv_rk, rwkv_lnx_g=rwkv_lnx_g, rwkv_lnx_b=rwkv_lnx_b,
             lru_conv_w=lru_conv_w, lru_conv_b=lru_conv_b, lru_wa=lru_wa, lru_ba=lru_ba, lru_wx=lru_wx,
             lru_bx=lru_bx, lru_lambda=lru_lambda, gdn_conv_w=gdn_conv_w, gdn_a_log=gdn_a_log,
             gdn_dt_bias=gdn_dt_bias, gdn_norm_g=gdn_norm_g)
    depth = w_in.shape[0]
    bp, t, _ = x_prompt.shape
    bs = x_sample.shape[0]

    w_in_p = _layout_in_cols(w_in).astype(BF16)
    w_branch_b = w_branch.astype(BF16)
    w_out_b = w_out.astype(BF16)
    norm_g3 = norm_g.reshape(depth, 1, D_MODEL)

    rows = bp + bs
    rows_pad = -(-rows // SUBLANE) * SUBLANE
    c_all = jnp.pad(jnp.concatenate([c_prompt, c_sample], axis=0), ((0, rows_pad - rows), (0, 0)))
    mod = _ada_all(c_all, ada_w, ada_b)
    mod_p = mod[:, 0:bp].reshape(depth, bp, 1, 3 * D_MODEL)
    mod_s = mod[:, bp:bp + bs]

    layer_params = [_prep_layer_params(l, p) for l in range(depth)]

    xp = x_prompt.reshape(bp * t, D_MODEL)
    xs = x_sample.reshape(bs, D_MODEL)
    p_states, s_states = [], []
    for l in range(depth):
        xp, st = _prompt_layer(xp, bp, t, l, layer_params[l], mod_p, norm_g3, w_in_p, w_branch_b, w_out_b)
        p_states.append(st)
        st_in = (state_rwkv_shift[l], state_rwkv_wkv[l], state_lru_conv[l], state_lru_h[l],
                 state_gdn_conv[l], state_gdn_ssm[l])
        xs, st = _sample_layer(xs, l, layer_params[l], st_in, mod_s, norm_g3, w_in_p, w_branch_b, w_out_b)
        s_states.append(st)

    y_prompt = _final_norm(xp, final_g, tm=512).reshape(bp, t, D_MODEL)
    y_sample = _final_norm(xs, final_g, tm=bs).reshape(bs, 1, D_MODEL)
    p_out = tuple(jnp.stack([st[i] for st in p_states]) for i in range(6))
    s_out = tuple(jnp.stack([st[i] for st in s_states]) for i in range(6))
    return (y_prompt, y_sample) + p_out + s_out
```

```python
import functools
import math

import jax
import jax.numpy as jnp
import numpy as np
from jax import lax
from jax.experimental import pallas as pl
from jax.experimental.pallas import tpu as pltpu

F32 = jnp.float32
BF16 = jnp.bfloat16
HIGHEST = lax.Precision.HIGHEST

D_MODEL = 2048
DEPTH = 4
MIX_W = 1024
HEAD_A = 64
H_A = 16
LORA = 96
GN_EPS_A = 64e-5
A_SHIFT_W = 3 * MIX_W + 2 * LORA
LRU_BLOCKS = 16
LRU_BS = 64
LRU_C = 8.0
CONV_W = 4
DK = 128
DV = 128
H_C = 8
QKV_W = 3 * MIX_W
NORM_EPS = 1e-6
IN_COLS = 16592

LANE = 128
SUBLANE = 8

OFF_R, OFF_K, OFF_V = 0, 1024, 2048
OFF_AG = 3072
OFF_BX, OFF_BG = 4096, 5120
OFF_Q = 6144
OFF_Z = 9216
OFF_MERGE = 10240
OFF_WD, OFF_AD, OFF_AB = 16384, 16512, 16640
N_PAD = 16896
SHIFT_PAD = 3 * MIX_W + 2 * LANE

RWKV_CHUNK = 64
GDN_CHUNK = 128
VMEM_LIMIT = 56 * 1024 * 1024


def _cparams(sem):
    return pltpu.CompilerParams(dimension_semantics=sem, vmem_limit_bytes=VMEM_LIMIT)


def _sigmoid(x):
    return 1.0 / (1.0 + jnp.exp(-x))


def _silu(x):
    return x * _sigmoid(x)


def _softplus(x):
    return jnp.maximum(x, 0.0) + jnp.log1p(jnp.exp(-jnp.abs(x)))


_NN = (((1,), (0,)), ((), ()))
_NT = (((1,), (1,)), ((), ()))
_TN = (((0,), (0,)), ((), ()))

P_LORA = "bf16"
P_CHUNK = "bf16"
P_INV = "bf16"
P_STATE = "bf16"
P_STEP = "f32"


def _dg(a, b, dims):
    return lax.dot_general(a, b, dims, preferred_element_type=F32)


def _dot(a, b, dims=_NN, mode="bf16"):
    if mode == "f32":
        return lax.dot_general(a, b, dims, preferred_element_type=F32, precision=HIGHEST)
    ah, bh = a.astype(BF16), b.astype(BF16)
    if mode == "bf16":
        return _dg(ah, bh, dims)
    al = (a - ah.astype(F32)).astype(BF16)
    bl = (b - bh.astype(F32)).astype(BF16)
    return _dg(ah, bh, dims) + (_dg(al, bh, dims) + _dg(ah, bl, dims))


def _dot_exact_lhs(a01, x):
    a = a01.astype(BF16)
    x1 = x.astype(BF16)
    r1 = x - x1.astype(F32)
    x2 = r1.astype(BF16)
    x3 = (r1 - x2.astype(F32)).astype(BF16)
    return _dg(a, x1, _NN) + (_dg(a, x2, _NN) + _dg(a, x3, _NN))


def _mm_bf16(a, b):
    return jnp.dot(a.astype(BF16), b.astype(BF16), preferred_element_type=F32)


def _iota2(n, m):
    return (lax.broadcasted_iota(jnp.int32, (n, m), 0), lax.broadcasted_iota(jnp.int32, (n, m), 1))


def _inv_unit_lower(mats, n, levels):
    mm = functools.partial(_dot, mode=P_INV)
    row, col = _iota2(n, n)
    eye = (row == col).astype(F32)
    same = (row >> 4) == (col >> 4)
    d = [jnp.where(same, a, 0.0) for a in mats]
    o = [a - di for a, di in zip(mats, d)]
    x1 = [-di for di in d]
    x2 = [mm(x, x) for x in x1]
    td = [mm(eye + a, eye + b) for a, b in zip(x1, x2)]
    x4 = [mm(x, x) for x in x2]
    td = [mm(t, eye + x) for t, x in zip(td, x4)]
    x8 = [mm(x, x) for x in x4]
    td = [mm(t, eye + x) for t, x in zip(td, x8)]
    p = [-mm(t, oi) for t, oi in zip(td, o)]
    t = [eye + pi for pi in p]
    for _ in range(levels - 1):
        p = [mm(pi, pi) for pi in p]
        t = [mm(ti, eye + pi) for ti, pi in zip(t, p)]
    return [mm(ti, tdi) for ti, tdi in zip(t, td)]


def _ada_kernel(c_ref, w_ref, b_ref, o_ref):
    h = _silu(c_ref[...]).astype(BF16)
    o_ref[...] = jnp.dot(h, w_ref[...].astype(BF16), preferred_element_type=F32) + b_ref[...]


def _ada_all(c_all, ada_w, ada_b, tn=512):
    rows = c_all.shape[0]
    depth = ada_w.shape[0]
    n = ada_w.shape[2]
    return pl.pallas_call(
        _ada_kernel,
        out_shape=jax.ShapeDtypeStruct((depth, rows, n), F32),
        grid=(depth, n // tn),
        in_specs=[pl.BlockSpec((rows, D_MODEL), lambda l, j: (0, 0)),
                  pl.BlockSpec((None, D_MODEL, tn), lambda l, j: (l, 0, j)),
                  pl.BlockSpec((None, 1, tn), lambda l, j: (l, 0, j))],
        out_specs=pl.BlockSpec((None, rows, tn), lambda l, j: (l, 0, j)),
        compiler_params=_cparams(("parallel", "parallel")),
        name="ada_mod",
    )(c_all, ada_w, ada_b.reshape(depth, 1, n))


def _inproj_kernel(x_ref, shift_ref, scale_ref, g_ref, w_ref, o_ref, h_ref):
    @pl.when(pl.program_id(1) == 0)
    def _():
        x = x_ref[...]
        y = x * lax.rsqrt(jnp.mean(x * x, axis=-1, keepdims=True) + NORM_EPS) * g_ref[...]
        h_ref[...] = (y * (1.0 + scale_ref[...]) + shift_ref[...]).astype(BF16)

    o_ref[...] = jnp.dot(h_ref[...], w_ref[...], preferred_element_type=F32)


def _inproj(x2d, mod, norm_g3, w_in_p, l, rows_per_mod, tm, tn):
    m = x2d.shape[0]
    n = w_in_p.shape[2]
    if rows_per_mod > 1:
        def mspec(part):
            return pl.BlockSpec((None, None, 1, D_MODEL), lambda i, j: (l, (i * tm) // rows_per_mod, 0, part))
    else:
        def mspec(part):
            return pl.BlockSpec((None, tm, D_MODEL), lambda i, j: (l, i, part))
    return pl.pallas_call(
        _inproj_kernel,
        out_shape=jax.ShapeDtypeStruct((m, n), F32),
        grid=(m // tm, n // tn),
        in_specs=[pl.BlockSpec((tm, D_MODEL), lambda i, j: (i, 0)),
                  mspec(0), mspec(1),
                  pl.BlockSpec((None, 1, D_MODEL), lambda i, j: (l, 0, 0)),
                  pl.BlockSpec((None, D_MODEL, tn), lambda i, j: (l, 0, j))],
        out_specs=pl.BlockSpec((tm, tn), lambda i, j: (i, j)),
        scratch_shapes=[pltpu.VMEM((tm, D_MODEL), BF16)],
        compiler_params=_cparams(("parallel", "arbitrary")),
        name="inproj",
    )(x2d, mod, mod, norm_g3, w_in_p)


def _merge_kernel(ya_ref, yb_ref, yc_ref, ga_ref, gb_ref, gc_ref, wa_ref, wb_ref, wc_ref, o_ref):
    acc = _sigmoid(ga_ref[...]) * jnp.dot(ya_ref[...].astype(BF16), wa_ref[...], preferred_element_type=F32)
    acc += _sigmoid(gb_ref[...]) * jnp.dot(yb_ref[...].astype(BF16), wb_ref[...], preferred_element_type=F32)
    acc += _sigmoid(gc_ref[...]) * jnp.dot(yc_ref[...].astype(BF16), wc_ref[...], preferred_element_type=F32)
    o_ref[...] = acc.astype(BF16)


def _merge(ya, yb, yc, u, w_branch_b, l, tm, tn):
    m = ya.shape[0]
    yspec = pl.BlockSpec((tm, MIX_W), lambda i, j: (i, 0))

    def gspec(br):
        return pl.BlockSpec((tm, tn), lambda i, j: (i, (OFF_MERGE + br * D_MODEL) // tn + j))

    def wspec(br):
        return pl.BlockSpec((None, None, MIX_W, tn), lambda i, j: (l, br, 0, j))

    return pl.pallas_call(
        _merge_kernel,
        out_shape=jax.ShapeDtypeStruct((m, D_MODEL), BF16),
        grid=(m // tm, D_MODEL // tn),
        in_specs=[yspec, yspec, yspec, gspec(0), gspec(1), gspec(2), wspec(0), wspec(1), wspec(2)],
        out_specs=pl.BlockSpec((tm, tn), lambda i, j: (i, j)),
        compiler_params=_cparams(("parallel", "parallel")),
        name="branch_merge",
    )(ya, yb, yc, u, u, u, w_branch_b, w_branch_b, w_branch_b)


def _outproj_kernel(m_ref, w_ref, x_ref, gate_ref, o_ref):
    o_ref[...] = x_ref[...] + gate_ref[...] * jnp.dot(m_ref[...], w_ref[...], preferred_element_type=F32)


def _outproj(merged, w_out_b, x2d, mod, l, rows_per_mod, tm, tn):
    m = x2d.shape[0]
    gate_blk = (2 * D_MODEL) // tn
    if rows_per_mod > 1:
        gspec = pl.BlockSpec((None, None, 1, tn), lambda i, j: (l, (i * tm) // rows_per_mod, 0, gate_blk + j))
    else:
        gspec = pl.BlockSpec((None, tm, tn), lambda i, j: (l, i, gate_blk + j))
    return pl.pallas_call(
        _outproj_kernel,
        out_shape=jax.ShapeDtypeStruct((m, D_MODEL), F32),
        grid=(m // tm, D_MODEL // tn),
        in_specs=[pl.BlockSpec((tm, D_MODEL), lambda i, j: (i, 0)),
                  pl.BlockSpec((None, D_MODEL, tn), lambda i, j: (l, 0, j)),
                  pl.BlockSpec((tm, tn), lambda i, j: (i, j)),
                  gspec],
        out_specs=pl.BlockSpec((tm, tn), lambda i, j: (i, j)),
        compiler_params=_cparams(("parallel", "parallel")),
        name="outproj",
    )(merged, w_out_b, x2d, mod)


def _final_norm_kernel(x_ref, g_ref, o_ref):
    x = x_ref[...]
    o_ref[...] = x * lax.rsqrt(jnp.mean(x * x, axis=-1, keepdims=True) + NORM_EPS) * g_ref[...]


def _final_norm(x2d, g, tm):
    m = x2d.shape[0]
    return pl.pallas_call(
        _final_norm_kernel,
        out_shape=jax.ShapeDtypeStruct((m, D_MODEL), F32),
        grid=(m // tm,),
        in_specs=[pl.BlockSpec((tm, D_MODEL), lambda i: (i, 0)), pl.BlockSpec((1, D_MODEL), lambda i: (0, 0))],
        out_specs=pl.BlockSpec((tm, D_MODEL), lambda i: (i, 0)),
        compiler_params=_cparams(("parallel",)),
        name="final_norm",
    )(x2d, g.reshape(1, D_MODEL))


PV_MIX_R, PV_MIX_K, PV_MIX_V, PV_W0, PV_A0, PV_KK, PV_KA, PV_RK, PV_LNG, PV_LNB = range(10)


def _half_sum(x, lo_mask):
    s_lo = jnp.sum(jnp.where(lo_mask, x, 0.0), axis=-1, keepdims=True)
    s_hi = jnp.sum(jnp.where(lo_mask, 0.0, x), axis=-1, keepdims=True)
    return jnp.where(lo_mask, s_lo, s_hi)


def _rwkv_prompt_kernel(r_ref, k_ref, v_ref, g_ref, wd_ref, ad_ref, pv_ref, mixl_ref, w2_ref, a2_ref,
                        shr_ref, shk_ref, shv_ref, shwd_ref, shad_ref, s0_ref,
                        y_ref, sout_ref, xbuf, sb, *, n_chunks, pairs):
    c = RWKV_CHUNK
    gw = pairs * LANE
    ti = pl.program_id(2)
    lo = lax.broadcasted_iota(jnp.int32, (1, LANE), 1) < HEAD_A

    @pl.when(ti == 0)
    def _():
        xbuf[0:SUBLANE, :] = jnp.concatenate(
            [shr_ref[...], shk_ref[...], shv_ref[...], shwd_ref[...], shad_ref[...]], axis=1)
        zero = jnp.zeros((HEAD_A, HEAD_A), F32)
        for g in range(pairs):
            top = jnp.concatenate([s0_ref[2 * g], zero], axis=1)
            bot = jnp.concatenate([zero, s0_ref[2 * g + 1]], axis=1)
            sb[g] = jnp.concatenate([top, bot], axis=0)

    cur = jnp.concatenate([r_ref[...], k_ref[...], v_ref[...], wd_ref[...], ad_ref[...]], axis=1)
    xbuf[SUBLANE:SUBLANE + c, :] = cur
    prev = xbuf[pl.ds(SUBLANE - 1, c), :]
    pv = pv_ref[...]
    mixl = mixl_ref[...]
    mix = jnp.concatenate([pv[PV_MIX_R:PV_MIX_R + 1], pv[PV_MIX_K:PV_MIX_K + 1], pv[PV_MIX_V:PV_MIX_V + 1],
                           mixl[0:1], mixl[1:2]], axis=1)
    z = cur + (prev - cur) * mix
    xbuf[0:SUBLANE, :] = cur[c - SUBLANE:c, :]

    zr, zk, zv = z[:, 0:gw], z[:, gw:2 * gw], z[:, 2 * gw:3 * gw]
    zwd, zad = z[:, 3 * gw:3 * gw + LANE], z[:, 3 * gw + LANE:]
    w0, a0 = pv[PV_W0:PV_W0 + 1], pv[PV_A0:PV_A0 + 1]
    k_k, k_a, r_k = pv[PV_KK:PV_KK + 1], pv[PV_KA:PV_KA + 1], pv[PV_RK:PV_RK + 1]
    ln_g, ln_b = pv[PV_LNG:PV_LNG + 1], pv[PV_LNB:PV_LNB + 1]

    logw = -_softplus(-(w0 + _dot(jnp.tanh(zwd), w2_ref[...], mode=P_LORA))) - 0.5
    lw = -jnp.exp(logw)
    a = _sigmoid(a0 + _dot(zad, a2_ref[...], mode=P_LORA))
    kkp = zk * k_k
    kmod = zk * (1.0 + (a - 1.0) * k_a)

    rowc, colc = _iota2(c, c)
    gcum = _dot_exact_lhs(rowc >= colc, lw)
    g_end = gcum[c - 1:c, :]
    e_pos, e_neg = jnp.exp(gcum), jnp.exp(-gcum)
    e_prev, e_end = jnp.exp(gcum - lw), jnp.exp(g_end - gcum)
    eg_end = jnp.exp(g_end)
    rt, kb, kend = zr * e_pos, kmod * e_neg, kmod * e_end
    gate = _silu(g_ref[...])
    rk_prod = zr * kmod * r_k

    n = 2 * c
    row, col = _iota2(n, n)
    same_head = (row >> 6) == (col >> 6)
    strict = same_head & (row > col)
    incl = same_head & (row >= col)
    eye = row == col
    mm = functools.partial(_dot, mode=P_CHUNK)

    def stack(x):
        return jnp.concatenate([jnp.where(lo, x, 0.0), jnp.where(lo, 0.0, x)], axis=0)

    gs = range(pairs)
    sls = [slice(g * LANE, (g + 1) * LANE) for g in gs]
    kk = [kkp[:, sl] * lax.rsqrt(_half_sum(kkp[:, sl] * kkp[:, sl], lo) + 1e-6) for sl in sls]
    bt = [kk[g] * a[:, sls[g]] for g in gs]
    rt_s = [stack(rt[:, sl]) for sl in sls]
    kb_s = [stack(kb[:, sl]) for sl in sls]
    bb_s = [stack(bt[g] * e_neg[:, sls[g]]) for g in gs]
    kt_s = [stack(kk[g] * e_prev[:, sls[g]]) for g in gs]
    kend_s = [stack(kend[:, sl]) for sl in sls]
    bend_s = [stack(bt[g] * e_end[:, sls[g]]) for g in gs]
    v_s = [stack(zv[:, sl]) for sl in sls]

    p_all = [mm(jnp.concatenate([kt_s[g], rt_s[g]], axis=0), jnp.concatenate([bb_s[g], kb_s[g]], axis=0), _NT)
             for g in gs]
    a_ub = [jnp.where(strict, p[0:n, 0:n], 0.0) for p in p_all]
    a_vk = [jnp.where(strict, p[0:n, n:2 * n], 0.0) for p in p_all]
    p_rb = [jnp.where(incl, p[n:2 * n, 0:n], 0.0) for p in p_all]
    p_rk = [jnp.where(incl, p[n:2 * n, n:2 * n], 0.0) for p in p_all]

    av = [mm(a_vk[g], v_s[g]) for g in gs]
    pv_ = [mm(p_rk[g], v_s[g]) for g in gs]
    t_inv = _inv_unit_lower(a_ub, n, levels=2)
    m12 = [mm(t_inv[g], jnp.concatenate([kt_s[g], av[g]], axis=1)) for g in gs]
    w2m = [mm(p_rb[g], m12[g]) for g in gs]
    q1 = [rt_s[g] - w2m[g][:, 0:LANE] for g in gs]
    q2 = [pv_[g] - w2m[g][:, LANE:2 * LANE] for g in gs]
    phi = [jnp.where(eye, eg_end[:, sls[g]], 0.0) - mm(bend_s[g], m12[g][:, 0:LANE], _TN) for g in gs]
    psi_t = [mm(jnp.concatenate([v_s[g], m12[g][:, LANE:2 * LANE]], axis=0),
                jnp.concatenate([kend_s[g], -bend_s[g]], axis=0), _TN) for g in gs]

    s_old = [sb[g] for g in gs]
    s_new = [_dot(s_old[g], phi[g], _NT, P_STATE) + psi_t[g] for g in gs]
    y_s = [_dot(q1[g], s_old[g], _NT, P_STATE) + q2[g] for g in gs]
    for g in gs:
        sb[g] = s_new[g]

    outs = []
    for g in gs:
        sl = sls[g]
        y2 = y_s[g][0:c, :] + y_s[g][c:n, :]
        mu = _half_sum(y2, lo) * (1.0 / HEAD_A)
        dlt = y2 - mu
        var = _half_sum(dlt * dlt, lo) * (1.0 / HEAD_A)
        yn = dlt * lax.rsqrt(var + GN_EPS_A) * ln_g[:, sl] + ln_b[:, sl]
        bonus = _half_sum(rk_prod[:, sl], lo) * zv[:, sl]
        outs.append((yn + bonus) * gate[:, sl])

    y_ref[...] = jnp.concatenate(outs, axis=1).astype(y_ref.dtype)

    @pl.when(ti == n_chunks - 1)
    def _():
        for g in range(pairs):
            s_fin = sb[g]
            sout_ref[2 * g] = s_fin[0:HEAD_A, 0:HEAD_A]
            sout_ref[2 * g + 1] = s_fin[HEAD_A:2 * HEAD_A, HEAD_A:2 * HEAD_A]


def _rwkv_prompt(u, nb, t, pvec, mixl, w2p, a2p, sh_init, s0, pairs=8):
    c = RWKV_CHUNK
    nt = t // c
    gw = pairs * LANE
    groups = (H_A // 2) // pairs

    def uspec(off):
        return pl.BlockSpec((c, gw), lambda b, p, i: (b * nt + i, off // gw + p))

    def lspec(off):
        return pl.BlockSpec((c, LANE), lambda b, p, i: (b * nt + i, off // LANE))

    def shspec(off):
        return pl.BlockSpec((None, SUBLANE, gw), lambda b, p, i: (b, 0, off // gw + p))

    def shlspec(off):
        return pl.BlockSpec((None, SUBLANE, LANE), lambda b, p, i: (b, 0, off // LANE))

    group_cols = lambda rows: pl.BlockSpec((rows, gw), lambda b, p, i: (0, p))
    return pl.pallas_call(
        functools.partial(_rwkv_prompt_kernel, n_chunks=nt, pairs=pairs),
        out_shape=(jax.ShapeDtypeStruct((nb * t, MIX_W), BF16),
                   jax.ShapeDtypeStruct((nb, H_A, HEAD_A, HEAD_A), F32)),
        grid=(nb, groups, nt),
        in_specs=[uspec(OFF_R), uspec(OFF_K), uspec(OFF_V), uspec(OFF_AG), lspec(OFF_WD), lspec(OFF_AD),
                  group_cols(16),
                  pl.BlockSpec((SUBLANE, LANE), lambda b, p, i: (0, 0)),
                  group_cols(LANE), group_cols(LANE),
                  shspec(0), shspec(MIX_W), shspec(2 * MIX_W), shlspec(3 * MIX_W), shlspec(3 * MIX_W + LANE),
                  pl.BlockSpec((None, 2 * pairs, HEAD_A, HEAD_A), lambda b, p, i: (b, p, 0, 0))],
        out_specs=(pl.BlockSpec((c, gw), lambda b, p, i: (b * nt + i, p)),
                   pl.BlockSpec((None, 2 * pairs, HEAD_A, HEAD_A), lambda b, p, i: (b, p, 0, 0))),
        scratch_shapes=[pltpu.VMEM((c + SUBLANE, 3 * gw + 2 * LANE), F32),
                        pltpu.VMEM((pairs, 2 * HEAD_A, 2 * HEAD_A), F32)],
        compiler_params=_cparams(("parallel", "parallel", "arbitrary")),
        name="rwkv_prompt",
    )(u, u, u, u, u, u, pvec, mixl, w2p, a2p, sh_init, sh_init, sh_init, sh_init, sh_init, s0)


def _rwkv_step_kernel(rkv_ref, g_ref, wd_ref, ad_ref, prev_ref, mix_ref, pv_ref, w2_ref, a2_ref, s_ref,
                      y_ref, sout_ref, hs, *, bb):
    cur = jnp.concatenate([rkv_ref[...], wd_ref[...], ad_ref[...]], axis=1)
    z = cur + (prev_ref[...] - cur) * mix_ref[...]
    zr, zk, zv = z[:, 0:MIX_W], z[:, MIX_W:2 * MIX_W], z[:, 2 * MIX_W:3 * MIX_W]
    zwd, zad = z[:, 3 * MIX_W:3 * MIX_W + LANE], z[:, 3 * MIX_W + LANE:]
    pv = pv_ref[...]
    w0, a0 = pv[PV_W0:PV_W0 + 1], pv[PV_A0:PV_A0 + 1]
    k_k, k_a = pv[PV_KK:PV_KK + 1], pv[PV_KA:PV_KA + 1]
    logw = -_softplus(-(w0 + _dot(jnp.tanh(zwd), w2_ref[...], mode=P_STEP))) - 0.5
    decay = jnp.exp(-jnp.exp(logw))
    a = _sigmoid(a0 + _dot(zad, a2_ref[...], mode=P_STEP))
    kkp = zk * k_k
    kmod = zk * (1.0 + (a - 1.0) * k_a)
    gate = _silu(g_ref[...])
    for q, arr in enumerate((zr, decay, kkp, a, kmod, zv, gate)):
        for h in range(H_A):
            hs[q, h] = arr[:, h * HEAD_A:(h + 1) * HEAD_A]

    row64, col64 = _iota2(HEAD_A, HEAD_A)
    diag = row64 == col64
    rowb = lax.broadcasted_iota(jnp.int32, (bb, HEAD_A), 0)

    def body(b, outs):
        new_outs = []
        for h in range(H_A):
            r = hs[0, h, pl.ds(b, 1), :]
            w = hs[1, h, pl.ds(b, 1), :]
            kkp_h = hs[2, h, pl.ds(b, 1), :]
            a_h = hs[3, h, pl.ds(b, 1), :]
            km = hs[4, h, pl.ds(b, 1), :]
            v = hs[5, h, pl.ds(b, 1), :]
            gt = hs[6, h, pl.ds(b, 1), :]
            kk = kkp_h * lax.rsqrt(jnp.sum(kkp_h * kkp_h, axis=-1, keepdims=True) + 1e-6)
            bt = kk * a_h
            s = s_ref[b, h]
            sa = jnp.sum(s * kk, axis=-1, keepdims=True)
            vcol = jnp.sum(jnp.where(diag, jnp.broadcast_to(v, (HEAD_A, HEAD_A)), 0.0), axis=-1, keepdims=True)
            s_new = s * w - sa * bt + vcol * km
            sout_ref[b, h] = s_new
            ycol = jnp.sum(s_new * r, axis=-1, keepdims=True)
            y = jnp.sum(jnp.where(diag, jnp.broadcast_to(ycol, (HEAD_A, HEAD_A)), 0.0), axis=0, keepdims=True)
            mu = jnp.sum(y, axis=-1, keepdims=True) * (1.0 / HEAD_A)
            dlt = y - mu
            var = jnp.sum(dlt * dlt, axis=-1, keepdims=True) * (1.0 / HEAD_A)
            sl = slice(h * HEAD_A, (h + 1) * HEAD_A)
            yn = dlt * lax.rsqrt(var + GN_EPS_A) * pv[PV_LNG:PV_LNG + 1, sl] + pv[PV_LNB:PV_LNB + 1, sl]
            bonus = jnp.sum(r * km * pv[PV_RK:PV_RK + 1, sl], axis=-1, keepdims=True) * v
            o = (yn + bonus) * gt
            new_outs.append(jnp.where(rowb == b, jnp.broadcast_to(o, (bb, HEAD_A)), outs[h]))
        return tuple(new_outs)

    outs = lax.fori_loop(0, bb, body, tuple(jnp.zeros((bb, HEAD_A), F32) for _ in range(H_A)))
    y_ref[...] = jnp.concatenate(outs, axis=1)


def _rwkv_step(u, prev_pad, mix_pad, pvec, w2p, a2p, s0, bb=SUBLANE):
    nb = u.shape[0]
    full = lambda shape: pl.BlockSpec(shape, lambda i: tuple(0 for _ in shape))
    return pl.pallas_call(
        functools.partial(_rwkv_step_kernel, bb=bb),
        out_shape=(jax.ShapeDtypeStruct((nb, MIX_W), F32),
                   jax.ShapeDtypeStruct((nb, H_A, HEAD_A, HEAD_A), F32)),
        grid=(nb // bb,),
        in_specs=[pl.BlockSpec((bb, 3 * MIX_W), lambda i: (i, 0)),
                  pl.BlockSpec((bb, MIX_W), lambda i: (i, OFF_AG // MIX_W)),
                  pl.BlockSpec((bb, LANE), lambda i: (i, OFF_WD // LANE)),
                  pl.BlockSpec((bb, LANE), lambda i: (i, OFF_AD // LANE)),
                  pl.BlockSpec((bb, SHIFT_PAD), lambda i: (i, 0)),
                  full((1, SHIFT_PAD)), full((16, MIX_W)), full((LANE, MIX_W)), full((LANE, MIX_W)),
                  pl.BlockSpec((bb, H_A, HEAD_A, HEAD_A), lambda i: (i, 0, 0, 0))],
        out_specs=(pl.BlockSpec((bb, MIX_W), lambda i: (i, 0)),
                   pl.BlockSpec((bb, H_A, HEAD_A, HEAD_A), lambda i: (i, 0, 0, 0))),
        scratch_shapes=[pltpu.VMEM((7, H_A, bb, HEAD_A), F32)],
        compiler_params=_cparams(("parallel",)),
        name="rwkv_step",
    )(u, u, u, u, prev_pad, mix_pad, pvec, w2p, a2p, s0)


LP_CONVB, LP_BA, LP_BX, LP_LAM = range(4)


def _lru_gates(xc, gate, wa_ref, wx_ref, lp):
    r = _sigmoid(_mm_bf16(xc, wa_ref[...]) + lp[LP_BA:LP_BA + 1])
    i = _sigmoid(_mm_bf16(xc, wx_ref[...]) + lp[LP_BX:LP_BX + 1])
    log_a = -LRU_C * r * _softplus(-lp[LP_LAM:LP_LAM + 1])
    a = jnp.exp(log_a)
    b = jnp.sqrt(jnp.tanh(-log_a) * (a * a + 1.0)) * (i * xc)
    return a, b


def _lru_prompt_kernel(x_ref, g_ref, cw_ref, lp_ref, wa_ref, wx_ref, cinit_ref, h0_ref,
                       y_ref, hout_ref, xbuf, hcar, *, n_blocks, tc):
    ti = pl.program_id(1)

    @pl.when(ti == 0)
    def _():
        xbuf[0:SUBLANE, :] = cinit_ref[...]
        hcar[...] = h0_ref[...]

    x = x_ref[...]
    xbuf[SUBLANE:SUBLANE + tc, :] = x
    cw = cw_ref[...]
    lp = lp_ref[...]
    xc = (xbuf[pl.ds(SUBLANE - 3, tc), :] * cw[0:1] + xbuf[pl.ds(SUBLANE - 2, tc), :] * cw[1:2]
          + xbuf[pl.ds(SUBLANE - 1, tc), :] * cw[2:3] + x * cw[3:4] + lp[LP_CONVB:LP_CONVB + 1])
    xbuf[0:SUBLANE, :] = x[tc - SUBLANE:tc, :]
    a, b = _lru_gates(xc, None, wa_ref, wx_ref, lp)
    rows = lax.broadcasted_iota(jnp.int32, (tc, MIX_W), 0)
    s = 1
    while s < tc:
        a_sh = pltpu.roll(a, s, 0)
        b_sh = pltpu.roll(b, s, 0)
        m = rows >= s
        b = jnp.where(m, a * b_sh + b, b)
        a = jnp.where(m, a * a_sh, a)
        s *= 2
    h = b + a * hcar[...]
    hcar[...] = h[tc - 1:tc, :]
    y_ref[...] = (h * _silu(g_ref[...])).astype(y_ref.dtype)

    @pl.when(ti == n_blocks - 1)
    def _():
        hout_ref[...] = h[tc - 1:tc, :]


def _lru_prompt(u, nb, t, conv_w, lp, wa_d, wx_d, cinit, h0, tc=256):
    nt = t // tc
    const = lambda shape: pl.BlockSpec(shape, lambda b, i: tuple(0 for _ in shape))
    return pl.pallas_call(
        functools.partial(_lru_prompt_kernel, n_blocks=nt, tc=tc),
        out_shape=(jax.ShapeDtypeStruct((nb * t, MIX_W), BF16),
                   jax.ShapeDtypeStruct((nb, 1, MIX_W), F32)),
        grid=(nb, nt),
        in_specs=[pl.BlockSpec((tc, MIX_W), lambda b, i: (b * nt + i, OFF_BX // MIX_W)),
                  pl.BlockSpec((tc, MIX_W), lambda b, i: (b * nt + i, OFF_BG // MIX_W)),
                  const((CONV_W, MIX_W)), const((SUBLANE, MIX_W)),
                  const((MIX_W, MIX_W)), const((MIX_W, MIX_W)),
                  pl.BlockSpec((None, SUBLANE, MIX_W), lambda b, i: (b, 0, 0)),
                  pl.BlockSpec((None, 1, MIX_W), lambda b, i: (b, 0, 0))],
        out_specs=(pl.BlockSpec((tc, MIX_W), lambda b, i: (b * nt + i, 0)),
                   pl.BlockSpec((None, 1, MIX_W), lambda b, i: (b, 0, 0))),
        scratch_shapes=[pltpu.VMEM((tc + SUBLANE, MIX_W), F32), pltpu.VMEM((1, MIX_W), F32)],
        compiler_params=_cparams(("parallel", "arbitrary")),
        name="lru_prompt",
    )(u, u, conv_w, lp, wa_d, wx_d, cinit, h0)


def _lru_step_kernel(x_ref, g_ref, buf_ref, h0_ref, cw_ref, lp_ref, wa_ref, wx_ref, y_ref, hout_ref):
    x = x_ref[...]
    cw = cw_ref[...]
    lp = lp_ref[...]
    xc = buf_ref[0] * cw[0:1] + buf_ref[1] * cw[1:2] + buf_ref[2] * cw[2:3] + x * cw[3:4] + lp[LP_CONVB:LP_CONVB + 1]
    a, b = _lru_gates(xc, None, wa_ref, wx_ref, lp)
    h = a * h0_ref[...] + b
    hout_ref[...] = h
    y_ref[...] = h * _silu(g_ref[...])


def _lru_step(u, buf_t, h0, conv_w, lp, wa_d, wx_d):
    nb = u.shape[0]
    const = lambda shape: pl.BlockSpec(shape, lambda i: tuple(0 for _ in shape))
    return pl.pallas_call(
        _lru_step_kernel,
        out_shape=(jax.ShapeDtypeStruct((nb, MIX_W), F32), jax.ShapeDtypeStruct((nb, MIX_W), F32)),
        grid=(1,),
        in_specs=[pl.BlockSpec((nb, MIX_W), lambda i: (0, OFF_BX // MIX_W)),
                  pl.BlockSpec((nb, MIX_W), lambda i: (0, OFF_BG // MIX_W)),
                  const((CONV_W - 1, nb, MIX_W)), const((nb, MIX_W)),
                  const((CONV_W, MIX_W)), const((SUBLANE, MIX_W)),
                  const((MIX_W, MIX_W)), const((MIX_W, MIX_W))],
        out_specs=(const((nb, MIX_W)), const((nb, MIX_W))),
        compiler_params=_cparams(("arbitrary",)),
        name="lru_step",
    )(u, u, buf_t, h0, conv_w, lp, wa_d, wx_d)


GP_ALOG, GP_DTB, GP_NORM = range(3)


def _gdn_prompt_kernel(q_ref, k_ref, v_ref, z_ref, ab_ref, cwq_ref, cwk_ref, cwv_ref, gp_ref,
                       ciq_ref, cik_ref, civ_ref, s0_ref, y_ref, sout_ref, xbuf, st, *, n_chunks, heads):
    c = GDN_CHUNK
    hw = heads * LANE
    h_base = pl.program_id(1) * heads
    ti = pl.program_id(2)

    @pl.when(ti == 0)
    def _():
        xbuf[0:SUBLANE, :] = jnp.concatenate([ciq_ref[...], cik_ref[...], civ_ref[...]], axis=1)
        st[...] = s0_ref[...]

    cur = jnp.concatenate([q_ref[...], k_ref[...], v_ref[...]], axis=1)
    xbuf[SUBLANE:SUBLANE + c, :] = cur
    cw = jnp.concatenate([cwq_ref[...], cwk_ref[...], cwv_ref[...]], axis=1)
    conv = (xbuf[pl.ds(SUBLANE - 3, c), :] * cw[0:1] + xbuf[pl.ds(SUBLANE - 2, c), :] * cw[1:2]
            + xbuf[pl.ds(SUBLANE - 1, c), :] * cw[2:3] + cur * cw[3:4])
    xbuf[0:SUBLANE, :] = cur[c - SUBLANE:c, :]
    act = _silu(conv)
    zg = _silu(z_ref[...])

    gp = gp_ref[...]
    lane_c = lax.broadcasted_iota(jnp.int32, (c, LANE), 1)
    ab = ab_ref[...]
    g_all = -jnp.exp(gp[GP_ALOG:GP_ALOG + 1]) * _softplus(ab + gp[GP_DTB:GP_DTB + 1])
    beta_all = _sigmoid(ab)

    row, col = _iota2(c, c)
    incl = row >= col
    strict = row > col
    eye = row == col
    mm = functools.partial(_dot, mode=P_CHUNK)

    hs_ = range(heads)
    qh = [act[:, j * DK:(j + 1) * DK] for j in hs_]
    kh = [act[:, hw + j * DK:hw + (j + 1) * DK] for j in hs_]
    vh = [act[:, 2 * hw + j * DV:2 * hw + (j + 1) * DV] for j in hs_]
    qn = [x * lax.rsqrt(jnp.sum(x * x, axis=-1, keepdims=True) + 1e-6) * (DK ** -0.5) for x in qh]
    kn = [x * lax.rsqrt(jnp.sum(x * x, axis=-1, keepdims=True) + 1e-6) for x in kh]
    g = [jnp.sum(jnp.where(lane_c == h_base + j, g_all, 0.0), axis=-1, keepdims=True) for j in hs_]
    beta = [jnp.sum(jnp.where(lane_c == h_base + j + H_C, beta_all, 0.0), axis=-1, keepdims=True) for j in hs_]

    gc = [_dot_exact_lhs(incl, jnp.broadcast_to(x, (c, LANE))) for x in g]
    gc_row = [jnp.sum(jnp.where(eye, x, 0.0), axis=0, keepdims=True) for x in gc]
    decay = [jnp.where(incl, jnp.exp(jnp.where(incl, gc[j] - gc_row[j], 0.0)), 0.0) for j in hs_]
    g_last = [x[c - 1:c, :] for x in gc]
    eg = [jnp.exp(x) for x in gc]

    kb = [kn[j] * beta[j] for j in hs_]
    p_all = [mm(jnp.concatenate([kb[j], qn[j]], axis=0), kn[j], _NT) for j in hs_]
    a_mat = [jnp.where(strict, p_all[j][0:c] * decay[j], 0.0) for j in hs_]
    qk = [p_all[j][c:2 * c] * decay[j] for j in hs_]
    t_inv = _inv_unit_lower(a_mat, c, levels=3)
    wu = [mm(t_inv[j], jnp.concatenate([kb[j] * eg[j], vh[j] * beta[j]], axis=1)) for j in hs_]
    k_dec = [kn[j] * jnp.exp(g_last[j] - gc[j]) for j in hs_]
    w3 = [mm(k_dec[j], wu[j], _TN) for j in hs_]
    w4 = [mm(qk[j], wu[j]) for j in hs_]
    phi = [jnp.where(eye, jnp.exp(g_last[j]), 0.0) - w3[j][:, 0:DK] for j in hs_]
    q1 = [qn[j] * eg[j] - w4[j][:, 0:DK] for j in hs_]

    s_old = [st[j] for j in hs_]
    s_new = [_dot(phi[j], s_old[j], _NN, P_STATE) + w3[j][:, DK:] for j in hs_]
    o = [_dot(q1[j], s_old[j], _NN, P_STATE) + w4[j][:, DK:] for j in hs_]
    for j in hs_:
        st[j] = s_new[j]

    norm_g = gp[GP_NORM:GP_NORM + 1]
    outs = [x * lax.rsqrt(jnp.mean(x * x, axis=-1, keepdims=True) + NORM_EPS) * norm_g * zg[:, j * DV:(j + 1) * DV]
            for j, x in enumerate(o)]
    y_ref[...] = jnp.concatenate(outs, axis=1).astype(y_ref.dtype)

    @pl.when(ti == n_chunks - 1)
    def _():
        sout_ref[...] = st[...]


def _gdn_prompt(u, nb, t, conv_w, gpar, cinit, s0, heads=8):
    c = GDN_CHUNK
    assert c == LANE and DK == LANE and DV == LANE
    nt = t // c
    hw = heads * LANE

    def uspec(off):
        return pl.BlockSpec((c, hw), lambda b, h, i: (b * nt + i, off // hw + h))

    def cwspec(off):
        return pl.BlockSpec((CONV_W, hw), lambda b, h, i: (0, off // hw + h))

    def cispec(off):
        return pl.BlockSpec((None, SUBLANE, hw), lambda b, h, i: (b, 0, off // hw + h))

    return pl.pallas_call(
        functools.partial(_gdn_prompt_kernel, n_chunks=nt, heads=heads),
        out_shape=(jax.ShapeDtypeStruct((nb * t, MIX_W), BF16),
                   jax.ShapeDtypeStruct((nb, H_C, DK, DV), F32)),
        grid=(nb, H_C // heads, nt),
        in_specs=[uspec(OFF_Q), uspec(OFF_Q + MIX_W), uspec(OFF_Q + 2 * MIX_W), uspec(OFF_Z),
                  pl.BlockSpec((c, LANE), lambda b, h, i: (b * nt + i, OFF_AB // LANE)),
                  cwspec(0), cwspec(MIX_W), cwspec(2 * MIX_W),
                  pl.BlockSpec((SUBLANE, LANE), lambda b, h, i: (0, 0)),
                  cispec(0), cispec(MIX_W), cispec(2 * MIX_W),
                  pl.BlockSpec((None, heads, DK, DV), lambda b, h, i: (b, h, 0, 0))],
        out_specs=(pl.BlockSpec((c, hw), lambda b, h, i: (b * nt + i, h)),
                   pl.BlockSpec((None, heads, DK, DV), lambda b, h, i: (b, h, 0, 0))),
        scratch_shapes=[pltpu.VMEM((c + SUBLANE, 3 * hw), F32), pltpu.VMEM((heads, DK, DV), F32)],
        compiler_params=_cparams(("parallel", "parallel", "arbitrary")),
        name="gdn_prompt",
    )(u, u, u, u, u, conv_w, conv_w, conv_w, gpar, cinit, cinit, cinit, s0)


def _gdn_step_kernel(qkv_ref, z_ref, ab_ref, buf_ref, cw_ref, gp_ref, s_ref, y_ref, sout_ref, hs, *, bb):
    cw = cw_ref[...]
    cur = qkv_ref[...]
    act = _silu(buf_ref[0] * cw[0:1] + buf_ref[1] * cw[1:2] + buf_ref[2] * cw[2:3] + cur * cw[3:4])
    gp = gp_ref[...]
    ab = ab_ref[...]
    g_all = -jnp.exp(gp[GP_ALOG:GP_ALOG + 1]) * _softplus(ab + gp[GP_DTB:GP_DTB + 1])
    beta_all = _sigmoid(ab)
    zg = _silu(z_ref[...])
    for h in range(H_C):
        qh = act[:, h * DK:(h + 1) * DK]
        kh = act[:, MIX_W + h * DK:MIX_W + (h + 1) * DK]
        hs[0, h] = qh * lax.rsqrt(jnp.sum(qh * qh, axis=-1, keepdims=True) + 1e-6) * (DK ** -0.5)
        hs[1, h] = kh * lax.rsqrt(jnp.sum(kh * kh, axis=-1, keepdims=True) + 1e-6)
        hs[2, h] = act[:, 2 * MIX_W + h * DV:2 * MIX_W + (h + 1) * DV]
        hs[3, h] = jnp.broadcast_to(jnp.exp(g_all[:, h:h + 1]), (bb, LANE))
        hs[4, h] = jnp.broadcast_to(beta_all[:, H_C + h:H_C + h + 1], (bb, LANE))
        hs[5, h] = zg[:, h * DV:(h + 1) * DV]

    row, col = _iota2(DK, DV)
    diag = row == col
    rowb = lax.broadcasted_iota(jnp.int32, (bb, DV), 0)
    norm_g = gp[GP_NORM:GP_NORM + 1]

    def body(b, outs):
        new_outs = []
        for h in range(H_C):
            q = hs[0, h, pl.ds(b, 1), :]
            k = hs[1, h, pl.ds(b, 1), :]
            v = hs[2, h, pl.ds(b, 1), :]
            eg = hs[3, h, pl.ds(b, 1), :]
            beta = hs[4, h, pl.ds(b, 1), :]
            zg_h = hs[5, h, pl.ds(b, 1), :]
            s = s_ref[b, h]
            kcol = jnp.sum(jnp.where(diag, jnp.broadcast_to(k, (DK, DV)), 0.0), axis=-1, keepdims=True)
            qcol = jnp.sum(jnp.where(diag, jnp.broadcast_to(q, (DK, DV)), 0.0), axis=-1, keepdims=True)
            ks = jnp.sum(kcol * s, axis=0, keepdims=True)
            qs = jnp.sum(qcol * s, axis=0, keepdims=True)
            v_new = beta * (v - eg * ks)
            o = eg * qs + jnp.sum(q * k, axis=-1, keepdims=True) * v_new
            sout_ref[b, h] = s * eg + kcol * v_new
            on = o * lax.rsqrt(jnp.mean(o * o, axis=-1, keepdims=True) + NORM_EPS) * norm_g
            new_outs.append(jnp.where(rowb == b, jnp.broadcast_to(on * zg_h, (bb, DV)), outs[h]))
        return tuple(new_outs)

    outs = lax.fori_loop(0, bb, body, tuple(jnp.zeros((bb, DV), F32) for _ in range(H_C)))
    y_ref[...] = jnp.concatenate(outs, axis=1)


def _gdn_step(u, buf_t, conv_w, gpar, s0, bb=SUBLANE):
    nb = u.shape[0]
    const = lambda shape: pl.BlockSpec(shape, lambda i: tuple(0 for _ in shape))
    return pl.pallas_call(
        functools.partial(_gdn_step_kernel, bb=bb),
        out_shape=(jax.ShapeDtypeStruct((nb, MIX_W), F32),
                   jax.ShapeDtypeStruct((nb, H_C, DK, DV), F32)),
        grid=(nb // bb,),
        in_specs=[pl.BlockSpec((bb, QKV_W), lambda i: (i, OFF_Q // QKV_W)),
                  pl.BlockSpec((bb, MIX_W), lambda i: (i, OFF_Z // MIX_W)),
                  pl.BlockSpec((bb, LANE), lambda i: (i, OFF_AB // LANE)),
                  pl.BlockSpec((CONV_W - 1, bb, QKV_W), lambda i: (0, i, 0)),
                  const((CONV_W, QKV_W)), const((SUBLANE, LANE)),
                  pl.BlockSpec((bb, H_C, DK, DV), lambda i: (i, 0, 0, 0))],
        out_specs=(pl.BlockSpec((bb, MIX_W), lambda i: (i, 0)),
                   pl.BlockSpec((bb, H_C, DK, DV), lambda i: (i, 0, 0, 0))),
        scratch_shapes=[pltpu.VMEM((6, H_C, bb, LANE), F32)],
        compiler_params=_cparams(("parallel",)),
        name="gdn_step",
    )(u, u, u, buf_t, conv_w, gpar, s0)


def _pad_cols(w, width):
    pad = width - w.shape[-1]
    return jnp.pad(w, [(0, 0)] * (w.ndim - 1) + [(0, pad)])


def _layout_in_cols(w):
    a_sh_end = A_SHIFT_W
    o_ag = a_sh_end
    o_bx = o_ag + MIX_W
    o_bg = o_bx + MIX_W
    o_q = o_bg + MIX_W
    o_a = o_q + QKV_W
    o_z = o_a + 2 * H_C
    o_m = o_z + MIX_W
    parts = [w[..., 0:3 * MIX_W], w[..., o_ag:o_bx], w[..., o_bx:o_bg], w[..., o_bg:o_q], w[..., o_q:o_a],
             w[..., o_z:o_m], w[..., o_m:IN_COLS],
             _pad_cols(w[..., 3 * MIX_W:3 * MIX_W + LORA], LANE),
             _pad_cols(w[..., 3 * MIX_W + LORA:a_sh_end], LANE),
             _pad_cols(w[..., o_a:o_z], LANE)]
    out = jnp.concatenate(parts, axis=-1)
    return _pad_cols(out, N_PAD)


def _shift_to_padded(s):
    return jnp.concatenate([s[..., 0:3 * MIX_W], _pad_cols(s[..., 3 * MIX_W:3 * MIX_W + LORA], LANE),
                            _pad_cols(s[..., 3 * MIX_W + LORA:], LANE)], axis=-1)


def _shift_from_u(u_rows):
    return jnp.concatenate([u_rows[..., 0:3 * MIX_W], u_rows[..., OFF_WD:OFF_WD + LORA],
                            u_rows[..., OFF_AD:OFF_AD + LORA]], axis=-1)


def _block_diag(w):
    eye = jnp.eye(LRU_BLOCKS, dtype=w.dtype)
    return jnp.einsum('nij,nm->nimj', w, eye).reshape(MIX_W, MIX_W)


def _prep_layer_params(l, p):
    pvec = jnp.zeros((16, MIX_W), F32)
    rows = [p['rwkv_mix'][l, 0:MIX_W], p['rwkv_mix'][l, MIX_W:2 * MIX_W], p['rwkv_mix'][l, 2 * MIX_W:3 * MIX_W],
            p['rwkv_w0'][l], p['rwkv_a0'][l], p['rwkv_kk'][l], p['rwkv_ka'][l], p['rwkv_rk'][l].reshape(MIX_W),
            p['rwkv_lnx_g'][l], p['rwkv_lnx_b'][l]]
    pvec = pvec.at[0:len(rows)].set(jnp.stack(rows))
    mix_pad = _shift_to_padded(p['rwkv_mix'][l])[None]
    mixl = jnp.zeros((SUBLANE, LANE), F32).at[0].set(mix_pad[0, 3 * MIX_W:3 * MIX_W + LANE])
    mixl = mixl.at[1].set(mix_pad[0, 3 * MIX_W + LANE:])
    w2p = jnp.pad(p['rwkv_w2'][l], ((0, LANE - LORA), (0, 0)))
    a2p = jnp.pad(p['rwkv_a2'][l], ((0, LANE - LORA), (0, 0)))
    lp = jnp.zeros((SUBLANE, MIX_W), F32).at[0:4].set(
        jnp.stack([p['lru_conv_b'][l], p['lru_ba'][l], p['lru_bx'][l], p['lru_lambda'][l]]))
    wa_d = _block_diag(p['lru_wa'][l]).astype(BF16)
    wx_d = _block_diag(p['lru_wx'][l]).astype(BF16)
    gpar = jnp.zeros((SUBLANE, LANE), F32)
    gpar = gpar.at[GP_ALOG, 0:H_C].set(p['gdn_a_log'][l]).at[GP_DTB, 0:H_C].set(p['gdn_dt_bias'][l])
    gpar = gpar.at[GP_NORM].set(p['gdn_norm_g'][l])
    return dict(pvec=pvec, mix_pad=mix_pad, mixl=mixl, w2p=w2p, a2p=a2p, lru_cw=p['lru_conv_w'][l], lp=lp,
                wa_d=wa_d, wx_d=wx_d, gdn_cw=p['gdn_conv_w'][l], gpar=gpar)


def _prompt_layer(x2d, nb, t, l, lp_, mod_p, norm_g3, w_in_p, w_branch_b, w_out_b):
    u = _inproj(x2d, mod_p, norm_g3, w_in_p, l, rows_per_mod=t, tm=1024, tn=768)
    ya, wkv = _rwkv_prompt(u, nb, t, lp_['pvec'], lp_['mixl'], lp_['w2p'], lp_['a2p'],
                           jnp.zeros((nb, SUBLANE, SHIFT_PAD), F32),
                           jnp.zeros((nb, H_A, HEAD_A, HEAD_A), F32))
    yb, h_last = _lru_prompt(u, nb, t, lp_['lru_cw'], lp_['lp'], lp_['wa_d'], lp_['wx_d'],
                             jnp.zeros((nb, SUBLANE, MIX_W), F32), jnp.zeros((nb, 1, MIX_W), F32))
    yc, ssm = _gdn_prompt(u, nb, t, lp_['gdn_cw'], lp_['gpar'],
                          jnp.zeros((nb, SUBLANE, QKV_W), F32), jnp.zeros((nb, H_C, DK, DV), F32))
    merged = _merge(ya, yb, yc, u, w_branch_b, l, tm=512, tn=1024)
    x_new = _outproj(merged, w_out_b, x2d, mod_p, l, rows_per_mod=t, tm=512, tn=1024)
    u3 = u.reshape(nb, t, N_PAD)
    states = (_shift_from_u(u3[:, t - 1]), wkv,
              u3[:, t - (CONV_W - 1):, OFF_BX:OFF_BX + MIX_W], h_last.reshape(nb, MIX_W),
              u3[:, t - (CONV_W - 1):, OFF_Q:OFF_Q + QKV_W], ssm)
    return x_new, states


def _sample_layer(x2d, l, lp_, st, mod_s, norm_g3, w_in_p, w_branch_b, w_out_b):
    shift_prev, wkv0, lru_buf, lru_h0, gdn_buf, gdn_s0 = st
    nb = x2d.shape[0]
    u = _inproj(x2d, mod_s, norm_g3, w_in_p, l, rows_per_mod=1, tm=nb, tn=768)
    ya, wkv = _rwkv_step(u, _shift_to_padded(shift_prev), lp_['mix_pad'], lp_['pvec'], lp_['w2p'], lp_['a2p'], wkv0)
    yb, h_new = _lru_step(u, jnp.swapaxes(lru_buf, 0, 1), lru_h0, lp_['lru_cw'], lp_['lp'], lp_['wa_d'], lp_['wx_d'])
    yc, ssm = _gdn_step(u, jnp.swapaxes(gdn_buf, 0, 1), lp_['gdn_cw'], lp_['gpar'], gdn_s0)
    merged = _merge(ya, yb, yc, u, w_branch_b, l, tm=nb, tn=1024)
    x_new = _outproj(merged, w_out_b, x2d, mod_s, l, rows_per_mod=1, tm=nb, tn=1024)
    states = (_shift_from_u(u), wkv,
              jnp.concatenate([lru_buf[:, 1:], u[:, None, OFF_BX:OFF_BX + MIX_W]], axis=1), h_new,
              jnp.concatenate([gdn_buf[:, 1:], u[:, None, OFF_Q:OFF_Q + QKV_W]], axis=1), ssm)
    return x_new, states


def kernel(x_prompt, x_sample, state_rwkv_shift, state_rwkv_wkv, state_lru_conv, state_lru_h, state_gdn_conv, state_gdn_ssm, c_prompt, c_sample, ada_w, ada_b, norm_g, w_in, rwkv_mix, rwkv_w0, rwkv_w2, rwkv_a0, rwkv_a2, rwkv_kk, rwkv_ka, rwkv_rk, rwkv_lnx_g, rwkv_lnx_b, lru_conv_w, lru_conv_b, lru_wa, lru_ba, lru_wx, lru_bx, lru_lambda, gdn_conv_w, gdn_a_log, gdn_dt_bias, gdn_norm_g, w_branch, w_out, final_g):
    p = dict(rwkv_mix=rwkv_mix, rwkv_w0=rwkv_w0, rwkv_w2=rwkv_w2, rwkv_a0=rwkv_a0, rwkv_a2=rwkv_a2,
             rwkv_kk=rwkv_kk, rwkv_ka=rwkv_ka, rwkv_rk=rwkv_rk, rwkv_lnx_g=rwkv_lnx_g, rwkv_lnx_b=rwkv_lnx_b,
             lru_conv_w=lru_conv_w, lru_conv_b=lru_conv_b, lru_wa=lru_wa, lru_ba=lru_ba, lru_wx=lru_wx,
             lru_bx=lru_bx, lru_lambda=lru_lambda, gdn_conv_w=gdn_conv_w, gdn_a_log=gdn_a_log,
             gdn_dt_bias=gdn_dt_bias, gdn_norm_g=gdn_norm_g)
    depth = w_in.shape[0]
    bp, t, _ = x_prompt.shape
    bs = x_sample.shape[0]

    w_in_p = _layout_in_cols(w_in).astype(BF16)
    w_branch_b = w_branch.astype(BF16)
    w_out_b = w_out.astype(BF16)
    norm_g3 = norm_g.reshape(depth, 1, D_MODEL)

    rows = bp + bs
    rows_pad = -(-rows // SUBLANE) * SUBLANE
    c_all = jnp.pad(jnp.concatenate([c_prompt, c_sample], axis=0), ((0, rows_pad - rows), (0, 0)))
    mod = _ada_all(c_all, ada_w, ada_b)
    mod_p = mod[:, 0:bp].reshape(depth, bp, 1, 3 * D_MODEL)
    mod_s = mod[:, bp:bp + bs]

    layer_params = [_prep_layer_params(l, p) for l in range(depth)]

    xp = x_prompt.reshape(bp * t, D_MODEL)
    xs = x_sample.reshape(bs, D_MODEL)
    p_states, s_states = [], []
    for l in range(depth):
        xp, st = _prompt_layer(xp, bp, t, l, layer_params[l], mod_p, norm_g3, w_in_p, w_branch_b, w_out_b)
        p_states.append(st)
        st_in = (state_rwkv_shift[l], state_rwkv_wkv[l], state_lru_conv[l], state_lru_h[l],
                 state_gdn_conv[l], state_gdn_ssm[l])
        xs, st = _sample_layer(xs, l, layer_params[l], st_in, mod_s, norm_g3, w_in_p, w_branch_b, w_out_b)
        s_states.append(st)

    y_prompt = _final_norm(xp, final_g, tm=512).reshape(bp, t, D_MODEL)
    y_sample = _final_norm(xs, final_g, tm=bs).reshape(bs, 1, D_MODEL)
    p_out = tuple(jnp.stack([st[i] for st in p_states]) for i in range(6))
    s_out = tuple(jnp.stack([st[i] for st in s_states]) for i in range(6))
    return (y_prompt, y_sample) + p_out + s_out
```

```python
import functools
import math

import jax
import jax.numpy as jnp
import numpy as np
from jax import lax
from jax.experimental import pallas as pl
from jax.experimental.pallas import tpu as pltpu

F32 = jnp.float32
BF16 = jnp.bfloat16
HIGHEST = lax.Precision.HIGHEST

D_MODEL = 2048
DEPTH = 4
MIX_W = 1024
HEAD_A = 64
H_A = 16
LORA = 96
GN_EPS_A = 64e-5
A_SHIFT_W = 3 * MIX_W + 2 * LORA
LRU_BLOCKS = 16
LRU_BS = 64
LRU_C = 8.0
CONV_W = 4
DK = 128
DV = 128
H_C = 8
QKV_W = 3 * MIX_W
NORM_EPS = 1e-6
IN_COLS = 16592

LANE = 128
SUBLANE = 8

OFF_R, OFF_K, OFF_V = 0, 1024, 2048
OFF_AG = 3072
OFF_BX, OFF_BG = 4096, 5120
OFF_Q = 6144
OFF_Z = 9216
OFF_MERGE = 10240
OFF_WD, OFF_AD, OFF_AB = 16384, 16512, 16640
N_PAD = 16896
SHIFT_PAD = 3 * MIX_W + 2 * LANE

RWKV_CHUNK = 64
GDN_CHUNK = 128
VMEM_LIMIT = 56 * 1024 * 1024

PROMPT_TM = 1024
INPROJ_TN = 1536
MERGE_TN = 512
OUTPROJ_TN = 1024


def _cparams(sem):
    return pltpu.CompilerParams(dimension_semantics=sem, vmem_limit_bytes=VMEM_LIMIT)


def _sigmoid(x):
    return 1.0 / (1.0 + jnp.exp(-x))


def _silu(x):
    return x * _sigmoid(x)


def _softplus(x):
    return jnp.maximum(x, 0.0) + jnp.log1p(jnp.exp(-jnp.abs(x)))


_NN = (((1,), (0,)), ((), ()))
_NT = (((1,), (1,)), ((), ()))
_TN = (((0,), (0,)), ((), ()))

P_LORA = "bf16"
P_CHUNK = "bf16"
P_INV = "bf16"
P_STATE = "bf16"
P_STEP = "f32"


def _dg(a, b, dims):
    return lax.dot_general(a, b, dims, preferred_element_type=F32)


def _dot(a, b, dims=_NN, mode="bf16"):
    if mode == "f32":
        return lax.dot_general(a, b, dims, preferred_element_type=F32, precision=HIGHEST)
    ah, bh = a.astype(BF16), b.astype(BF16)
    if mode == "bf16":
        return _dg(ah, bh, dims)
    al = (a - ah.astype(F32)).astype(BF16)
    bl = (b - bh.astype(F32)).astype(BF16)
    return _dg(ah, bh, dims) + (_dg(al, bh, dims) + _dg(ah, bl, dims))


def _dot_exact_lhs(a01, x):
    a = a01.astype(BF16)
    x1 = x.astype(BF16)
    r1 = x - x1.astype(F32)
    x2 = r1.astype(BF16)
    x3 = (r1 - x2.astype(F32)).astype(BF16)
    return _dg(a, x1, _NN) + (_dg(a, x2, _NN) + _dg(a, x3, _NN))


def _mm_bf16(a, b):
    return jnp.dot(a.astype(BF16), b.astype(BF16), preferred_element_type=F32)


def _iota2(n, m):
    return (lax.broadcasted_iota(jnp.int32, (n, m), 0), lax.broadcasted_iota(jnp.int32, (n, m), 1))


def _inv_unit_lower(mats, n, levels):
    mm = functools.partial(_dot, mode=P_INV)
    row, col = _iota2(n, n)
    eye = (row == col).astype(F32)
    same = (row >> 4) == (col >> 4)
    d = [jnp.where(same, a, 0.0) for a in mats]
    o = [a - di for a, di in zip(mats, d)]
    x1 = [-di for di in d]
    x2 = [mm(x, x) for x in x1]
    td = [mm(eye + a, eye + b) for a, b in zip(x1, x2)]
    x4 = [mm(x, x) for x in x2]
    td = [mm(t, eye + x) for t, x in zip(td, x4)]
    x8 = [mm(x, x) for x in x4]
    td = [mm(t, eye + x) for t, x in zip(td, x8)]
    p = [-mm(t, oi) for t, oi in zip(td, o)]
    t = [eye + pi for pi in p]
    for _ in range(levels - 1):
        p = [mm(pi, pi) for pi in p]
        t = [mm(ti, eye + pi) for ti, pi in zip(t, p)]
    return [mm(ti, tdi) for ti, tdi in zip(t, td)]


def _ada_kernel(c_ref, w_ref, b_ref, o_ref):
    h = _silu(c_ref[...]).astype(BF16)
    o_ref[...] = jnp.dot(h, w_ref[...].astype(BF16), preferred_element_type=F32) + b_ref[...]


def _ada_all(c_all, ada_w, ada_b, tn=512):
    rows = c_all.shape[0]
    depth = ada_w.shape[0]
    n = ada_w.shape[2]
    return pl.pallas_call(
        _ada_kernel,
        out_shape=jax.ShapeDtypeStruct((depth, rows, n), F32),
        grid=(depth, n // tn),
        in_specs=[pl.BlockSpec((rows, D_MODEL), lambda l, j: (0, 0)),
                  pl.BlockSpec((None, D_MODEL, tn), lambda l, j: (l, 0, j)),
                  pl.BlockSpec((None, 1, tn), lambda l, j: (l, 0, j))],
        out_specs=pl.BlockSpec((None, rows, tn), lambda l, j: (l, 0, j)),
        compiler_params=_cparams(("parallel", "parallel")),
        name="ada_mod",
    )(c_all, ada_w, ada_b.reshape(depth, 1, n))


def _inproj_kernel(x_ref, shift_ref, scale_ref, g_ref, w_ref, o_ref, h_ref):
    @pl.when(pl.program_id(1) == 0)
    def _():
        x = x_ref[...]
        y = x * lax.rsqrt(jnp.mean(x * x, axis=-1, keepdims=True) + NORM_EPS) * g_ref[...]
        h_ref[...] = (y * (1.0 + scale_ref[...]) + shift_ref[...]).astype(BF16)

    o_ref[...] = jnp.dot(h_ref[...], w_ref[...], preferred_element_type=F32)


def _inproj(x2d, mod, norm_g3, w_in_p, l, rows_per_mod, tm, tn):
    m = x2d.shape[0]
    n = w_in_p.shape[2]
    if rows_per_mod > 1:
        def mspec(part):
            return pl.BlockSpec((None, None, 1, D_MODEL), lambda i, j: (l, (i * tm) // rows_per_mod, 0, part))
    else:
        def mspec(part):
            return pl.BlockSpec((None, tm, D_MODEL), lambda i, j: (l, i, part))
    return pl.pallas_call(
        _inproj_kernel,
        out_shape=jax.ShapeDtypeStruct((m, n), F32),
        grid=(m // tm, n // tn),
        in_specs=[pl.BlockSpec((tm, D_MODEL), lambda i, j: (i, 0)),
                  mspec(0), mspec(1),
                  pl.BlockSpec((None, 1, D_MODEL), lambda i, j: (l, 0, 0)),
                  pl.BlockSpec((None, D_MODEL, tn), lambda i, j: (l, 0, j))],
        out_specs=pl.BlockSpec((tm, tn), lambda i, j: (i, j)),
        scratch_shapes=[pltpu.VMEM((tm, D_MODEL), BF16)],
        compiler_params=_cparams(("parallel", "arbitrary")),
        name="inproj",
    )(x2d, mod, mod, norm_g3, w_in_p)


def _merge_kernel(ya_ref, yb_ref, yc_ref, ga_ref, gb_ref, gc_ref, wa_ref, wb_ref, wc_ref, o_ref):
    acc = _sigmoid(ga_ref[...]) * jnp.dot(ya_ref[...].astype(BF16), wa_ref[...], preferred_element_type=F32)
    acc += _sigmoid(gb_ref[...]) * jnp.dot(yb_ref[...].astype(BF16), wb_ref[...], preferred_element_type=F32)
    acc += _sigmoid(gc_ref[...]) * jnp.dot(yc_ref[...].astype(BF16), wc_ref[...], preferred_element_type=F32)
    o_ref[...] = acc.astype(BF16)


def _merge(ya, yb, yc, u, w_branch_b, l, tm, tn):
    m = ya.shape[0]
    yspec = pl.BlockSpec((tm, MIX_W), lambda i, j: (i, 0))

    def gspec(br):
        return pl.BlockSpec((tm, tn), lambda i, j: (i, (OFF_MERGE + br * D_MODEL) // tn + j))

    def wspec(br):
        return pl.BlockSpec((None, None, MIX_W, tn), lambda i, j: (l, br, 0, j))

    return pl.pallas_call(
        _merge_kernel,
        out_shape=jax.ShapeDtypeStruct((m, D_MODEL), BF16),
        grid=(m // tm, D_MODEL // tn),
        in_specs=[yspec, yspec, yspec, gspec(0), gspec(1), gspec(2), wspec(0), wspec(1), wspec(2)],
        out_specs=pl.BlockSpec((tm, tn), lambda i, j: (i, j)),
        compiler_params=_cparams(("parallel", "parallel")),
        name="branch_merge",
    )(ya, yb, yc, u, u, u, w_branch_b, w_branch_b, w_branch_b)


def _outproj_kernel(m_ref, w_ref, x_ref, gate_ref, o_ref):
    o_ref[...] = x_ref[...] + gate_ref[...] * jnp.dot(m_ref[...], w_ref[...], preferred_element_type=F32)


def _outproj(merged, w_out_b, x2d, mod, l, rows_per_mod, tm, tn):
    m = x2d.shape[0]
    gate_blk = (2 * D_MODEL) // tn
    if rows_per_mod > 1:
        gspec = pl.BlockSpec((None, None, 1, tn), lambda i, j: (l, (i * tm) // rows_per_mod, 0, gate_blk + j))
    else:
        gspec = pl.BlockSpec((None, tm, tn), lambda i, j: (l, i, gate_blk + j))
    return pl.pallas_call(
        _outproj_kernel,
        out_shape=jax.ShapeDtypeStruct((m, D_MODEL), F32),
        grid=(m // tm, D_MODEL // tn),
        in_specs=[pl.BlockSpec((tm, D_MODEL), lambda i, j: (i, 0)),
                  pl.BlockSpec((None, D_MODEL, tn), lambda i, j: (l, 0, j)),
                  pl.BlockSpec((tm, tn), lambda i, j: (i, j)),
                  gspec],
        out_specs=pl.BlockSpec((tm, tn), lambda i, j: (i, j)),
        compiler_params=_cparams(("parallel", "parallel")),
        name="outproj",
    )(merged, w_out_b, x2d, mod)


def _final_norm_kernel(x_ref, g_ref, o_ref):
    x = x_ref[...]
    o_ref[...] = x * lax.rsqrt(jnp.mean(x * x, axis=-1, keepdims=True) + NORM_EPS) * g_ref[...]


def _final_norm(x2d, g, tm):
    m = x2d.shape[0]
    return pl.pallas_call(
        _final_norm_kernel,
        out_shape=jax.ShapeDtypeStruct((m, D_MODEL), F32),
        grid=(m // tm,),
        in_specs=[pl.BlockSpec((tm, D_MODEL), lambda i: (i, 0)), pl.BlockSpec((1, D_MODEL), lambda i: (0, 0))],
        out_specs=pl.BlockSpec((tm, D_MODEL), lambda i: (i, 0)),
        compiler_params=_cparams(("parallel",)),
        name="final_norm",
    )(x2d, g.reshape(1, D_MODEL))


PV_MIX_R, PV_MIX_K, PV_MIX_V, PV_W0, PV_A0, PV_KK, PV_KA, PV_RK, PV_LNG, PV_LNB = range(10)


def _half_sum(x, lo_mask):
    s_lo = jnp.sum(jnp.where(lo_mask, x, 0.0), axis=-1, keepdims=True)
    s_hi = jnp.sum(jnp.where(lo_mask, 0.0, x), axis=-1, keepdims=True)
    return jnp.where(lo_mask, s_lo, s_hi)


def _rwkv_prompt_kernel(r_ref, k_ref, v_ref, g_ref, wd_ref, ad_ref, pv_ref, mixl_ref, w2_ref, a2_ref,
                        shr_ref, shk_ref, shv_ref, shwd_ref, shad_ref, s0_ref,
                        y_ref, sout_ref, xbuf, sb, *, n_chunks, pairs):
    c = RWKV_CHUNK
    gw = pairs * LANE
    ti = pl.program_id(2)
    lo = lax.broadcasted_iota(jnp.int32, (1, LANE), 1) < HEAD_A

    @pl.when(ti == 0)
    def _():
        xbuf[0:SUBLANE, :] = jnp.concatenate(
            [shr_ref[...], shk_ref[...], shv_ref[...], shwd_ref[...], shad_ref[...]], axis=1)
        zero = jnp.zeros((HEAD_A, HEAD_A), F32)
        for g in range(pairs):
            top = jnp.concatenate([s0_ref[2 * g], zero], axis=1)
            bot = jnp.concatenate([zero, s0_ref[2 * g + 1]], axis=1)
            sb[g] = jnp.concatenate([top, bot], axis=0)

    cur = jnp.concatenate([r_ref[...], k_ref[...], v_ref[...], wd_ref[...], ad_ref[...]], axis=1)
    xbuf[SUBLANE:SUBLANE + c, :] = cur
    prev = xbuf[pl.ds(SUBLANE - 1, c), :]
    pv = pv_ref[...]
    mixl = mixl_ref[...]
    mix = jnp.concatenate([pv[PV_MIX_R:PV_MIX_R + 1], pv[PV_MIX_K:PV_MIX_K + 1], pv[PV_MIX_V:PV_MIX_V + 1],
                           mixl[0:1], mixl[1:2]], axis=1)
    z = cur + (prev - cur) * mix
    xbuf[0:SUBLANE, :] = cur[c - SUBLANE:c, :]

    zr, zk, zv = z[:, 0:gw], z[:, gw:2 * gw], z[:, 2 * gw:3 * gw]
    zwd, zad = z[:, 3 * gw:3 * gw + LANE], z[:, 3 * gw + LANE:]
    w0, a0 = pv[PV_W0:PV_W0 + 1], pv[PV_A0:PV_A0 + 1]
    k_k, k_a, r_k = pv[PV_KK:PV_KK + 1], pv[PV_KA:PV_KA + 1], pv[PV_RK:PV_RK + 1]
    ln_g, ln_b = pv[PV_LNG:PV_LNG + 1], pv[PV_LNB:PV_LNB + 1]

    logw = -_softplus(-(w0 + _dot(jnp.tanh(zwd), w2_ref[...], mode=P_LORA))) - 0.5
    lw = -jnp.exp(logw)
    a = _sigmoid(a0 + _dot(zad, a2_ref[...], mode=P_LORA))
    kkp = zk * k_k
    kmod = zk * (1.0 + (a - 1.0) * k_a)

    rowc, colc = _iota2(c, c)
    gcum = _dot_exact_lhs(rowc >= colc, lw)
    g_end = gcum[c - 1:c, :]
    e_pos, e_neg = jnp.exp(gcum), jnp.exp(-gcum)
    e_prev, e_end = jnp.exp(gcum - lw), jnp.exp(g_end - gcum)
    eg_end = jnp.exp(g_end)
    rt, kb, kend = zr * e_pos, kmod * e_neg, kmod * e_end
    gate = _silu(g_ref[...])
    rk_prod = zr * kmod * r_k

    n = 2 * c
    row, col = _iota2(n, n)
    same_head = (row >> 6) == (col >> 6)
    strict = same_head & (row > col)
    incl = same_head & (row >= col)
    eye = row == col
    mm = functools.partial(_dot, mode=P_CHUNK)

    def stack(x):
        return jnp.concatenate([jnp.where(lo, x, 0.0), jnp.where(lo, 0.0, x)], axis=0)

    gs = range(pairs)
    sls = [slice(g * LANE, (g + 1) * LANE) for g in gs]
    kk = [kkp[:, sl] * lax.rsqrt(_half_sum(kkp[:, sl] * kkp[:, sl], lo) + 1e-6) for sl in sls]
    bt = [kk[g] * a[:, sls[g]] for g in gs]
    rt_s = [stack(rt[:, sl]) for sl in sls]
    kb_s = [stack(kb[:, sl]) for sl in sls]
    bb_s = [stack(bt[g] * e_neg[:, sls[g]]) for g in gs]
    kt_s = [stack(kk[g] * e_prev[:, sls[g]]) for g in gs]
    kend_s = [stack(kend[:, sl]) for sl in sls]
    bend_s = [stack(bt[g] * e_end[:, sls[g]]) for g in gs]
    v_s = [stack(zv[:, sl]) for sl in sls]

    p_all = [mm(jnp.concatenate([kt_s[g], rt_s[g]], axis=0), jnp.concatenate([bb_s[g], kb_s[g]], axis=0), _NT)
             for g in gs]
    a_ub = [jnp.where(strict, p[0:n, 0:n], 0.0) for p in p_all]
    a_vk = [jnp.where(strict, p[0:n, n:2 * n], 0.0) for p in p_all]
    p_rb = [jnp.where(incl, p[n:2 * n, 0:n], 0.0) for p in p_all]
    p_rk = [jnp.where(incl, p[n:2 * n, n:2 * n], 0.0) for p in p_all]

    av = [mm(a_vk[g], v_s[g]) for g in gs]
    pv_ = [mm(p_rk[g], v_s[g]) for g in gs]
    t_inv = _inv_unit_lower(a_ub, n, levels=2)
    m12 = [mm(t_inv[g], jnp.concatenate([kt_s[g], av[g]], axis=1)) for g in gs]
    w2m = [mm(p_rb[g], m12[g]) for g in gs]
    q1 = [rt_s[g] - w2m[g][:, 0:LANE] for g in gs]
    q2 = [pv_[g] - w2m[g][:, LANE:2 * LANE] for g in gs]
    phi = [jnp.where(eye, eg_end[:, sls[g]], 0.0) - mm(bend_s[g], m12[g][:, 0:LANE], _TN) for g in gs]
    psi_t = [mm(jnp.concatenate([v_s[g], m12[g][:, LANE:2 * LANE]], axis=0),
                jnp.concatenate([kend_s[g], -bend_s[g]], axis=0), _TN) for g in gs]

    s_old = [sb[g] for g in gs]
    s_new = [_dot(s_old[g], phi[g], _NT, P_STATE) + psi_t[g] for g in gs]
    y_s = [_dot(q1[g], s_old[g], _NT, P_STATE) + q2[g] for g in gs]
    for g in gs:
        sb[g] = s_new[g]

    outs = []
    for g in gs:
        sl = sls[g]
        y2 = y_s[g][0:c, :] + y_s[g][c:n, :]
        mu = _half_sum(y2, lo) * (1.0 / HEAD_A)
        dlt = y2 - mu
        var = _half_sum(dlt * dlt, lo) * (1.0 / HEAD_A)
        yn = dlt * lax.rsqrt(var + GN_EPS_A) * ln_g[:, sl] + ln_b[:, sl]
        bonus = _half_sum(rk_prod[:, sl], lo) * zv[:, sl]
        outs.append((yn + bonus) * gate[:, sl])

    y_ref[...] = jnp.concatenate(outs, axis=1).astype(y_ref.dtype)

    @pl.when(ti == n_chunks - 1)
    def _():
        for g in range(pairs):
            s_fin = sb[g]
            sout_ref[2 * g] = s_fin[0:HEAD_A, 0:HEAD_A]
            sout_ref[2 * g + 1] = s_fin[HEAD_A:2 * HEAD_A, HEAD_A:2 * HEAD_A]


def _rwkv_prompt(u, nb, t, pvec, mixl, w2p, a2p, sh_init, s0, l, pairs=8):
    c = RWKV_CHUNK
    nt = t // c
    gw = pairs * LANE
    groups = (H_A // 2) // pairs

    def uspec(off):
        return pl.BlockSpec((c, gw), lambda b, p, i: (b * nt + i, off // gw + p))

    def lspec(off):
        return pl.BlockSpec((c, LANE), lambda b, p, i: (b * nt + i, off // LANE))

    def shspec(off):
        return pl.BlockSpec((None, SUBLANE, gw), lambda b, p, i: (b, 0, off // gw + p))

    def shlspec(off):
        return pl.BlockSpec((None, SUBLANE, LANE), lambda b, p, i: (b, 0, off // LANE))

    group_cols = lambda rows: pl.BlockSpec((None, rows, gw), lambda b, p, i: (l, 0, p))
    return pl.pallas_call(
        functools.partial(_rwkv_prompt_kernel, n_chunks=nt, pairs=pairs),
        out_shape=(jax.ShapeDtypeStruct((nb * t, MIX_W), BF16),
                   jax.ShapeDtypeStruct((nb, H_A, HEAD_A, HEAD_A), F32)),
        grid=(nb, groups, nt),
        in_specs=[uspec(OFF_R), uspec(OFF_K), uspec(OFF_V), uspec(OFF_AG), lspec(OFF_WD), lspec(OFF_AD),
                  group_cols(16),
                  pl.BlockSpec((None, SUBLANE, LANE), lambda b, p, i: (l, 0, 0)),
                  group_cols(LANE), group_cols(LANE),
                  shspec(0), shspec(MIX_W), shspec(2 * MIX_W), shlspec(3 * MIX_W), shlspec(3 * MIX_W + LANE),
                  pl.BlockSpec((None, 2 * pairs, HEAD_A, HEAD_A), lambda b, p, i: (b, p, 0, 0))],
        out_specs=(pl.BlockSpec((c, gw), lambda b, p, i: (b * nt + i, p)),
                   pl.BlockSpec((None, 2 * pairs, HEAD_A, HEAD_A), lambda b, p, i: (b, p, 0, 0))),
        scratch_shapes=[pltpu.VMEM((c + SUBLANE, 3 * gw + 2 * LANE), F32),
                        pltpu.VMEM((pairs, 2 * HEAD_A, 2 * HEAD_A), F32)],
        compiler_params=_cparams(("parallel", "parallel", "arbitrary")),
        name="rwkv_prompt",
    )(u, u, u, u, u, u, pvec, mixl, w2p, a2p, sh_init, sh_init, sh_init, sh_init, sh_init, s0)


def _rwkv_step_kernel(rkv_ref, g_ref, wd_ref, ad_ref, prev_ref, mix_ref, pv_ref, w2_ref, a2_ref, s_ref, acc_ref,
                      y_ref, sout_ref, hs, *, bb):
    del acc_ref
    cur = jnp.concatenate([rkv_ref[...], wd_ref[...], ad_ref[...]], axis=1)
    z = cur + (prev_ref[...] - cur) * mix_ref[...]
    zr, zk, zv = z[:, 0:MIX_W], z[:, MIX_W:2 * MIX_W], z[:, 2 * MIX_W:3 * MIX_W]
    zwd, zad = z[:, 3 * MIX_W:3 * MIX_W + LANE], z[:, 3 * MIX_W + LANE:]
    pv = pv_ref[...]
    w0, a0 = pv[PV_W0:PV_W0 + 1], pv[PV_A0:PV_A0 + 1]
    k_k, k_a = pv[PV_KK:PV_KK + 1], pv[PV_KA:PV_KA + 1]
    logw = -_softplus(-(w0 + _dot(jnp.tanh(zwd), w2_ref[...], mode=P_STEP))) - 0.5
    decay = jnp.exp(-jnp.exp(logw))
    a = _sigmoid(a0 + _dot(zad, a2_ref[...], mode=P_STEP))
    kkp = zk * k_k
    kmod = zk * (1.0 + (a - 1.0) * k_a)
    gate = _silu(g_ref[...])
    rk_prod = zr * kmod * pv[PV_RK:PV_RK + 1]
    ln_g, ln_b = pv[PV_LNG:PV_LNG + 1], pv[PV_LNB:PV_LNB + 1]
    hs_r, hs_w, hs_kk, hs_bt, hs_km, hs_v, hs_gate, hs_bonus, hs_lng, hs_lnb = range(10)
    for h in range(H_A):
        sl = slice(h * HEAD_A, (h + 1) * HEAD_A)
        kkp_h = kkp[:, sl]
        kk_h = kkp_h * lax.rsqrt(jnp.sum(kkp_h * kkp_h, axis=-1, keepdims=True) + 1e-6)
        hs[hs_r, h] = zr[:, sl]
        hs[hs_w, h] = decay[:, sl]
        hs[hs_kk, h] = kk_h
        hs[hs_bt, h] = kk_h * a[:, sl]
        hs[hs_km, h] = kmod[:, sl]
        hs[hs_v, h] = zv[:, sl]
        hs[hs_gate, h] = gate[:, sl]
        hs[hs_bonus, h] = jnp.sum(rk_prod[:, sl], axis=-1, keepdims=True) * zv[:, sl]
        hs[hs_lng, h] = jnp.broadcast_to(ln_g[:, sl], (bb, HEAD_A))
        hs[hs_lnb, h] = jnp.broadcast_to(ln_b[:, sl], (bb, HEAD_A))

    row64, col64 = _iota2(HEAD_A, HEAD_A)
    diag = row64 == col64
    rowb = lax.broadcasted_iota(jnp.int32, (bb, HEAD_A), 0)
    heads = range(H_A)

    def body(b, outs):
        def rows(q):
            return [hs[q, h, pl.ds(b, 1), :] for h in heads]

        r, w, kk, bt, km, v = rows(hs_r), rows(hs_w), rows(hs_kk), rows(hs_bt), rows(hs_km), rows(hs_v)
        s = [s_ref[b, h] for h in heads]
        sa = [jnp.sum(s[h] * kk[h], axis=-1, keepdims=True) for h in heads]
        vcol = [jnp.sum(jnp.where(diag, jnp.broadcast_to(v[h], (HEAD_A, HEAD_A)), 0.0), axis=-1, keepdims=True)
                for h in heads]
        s_new = [s[h] * w[h] - sa[h] * bt[h] + vcol[h] * km[h] for h in heads]
        for h in heads:
            sout_ref[b, h] = s_new[h]
        ycol = [jnp.sum(s_new[h] * r[h], axis=-1, keepdims=True) for h in heads]
        mu = [jnp.sum(y, axis=0, keepdims=True) * (1.0 / HEAD_A) for y in ycol]
        dlt = [ycol[h] - mu[h] for h in heads]
        var = [jnp.sum(d * d, axis=0, keepdims=True) * (1.0 / HEAD_A) for d in dlt]
        yn_col = [dlt[h] * lax.rsqrt(var[h] + GN_EPS_A) for h in heads]
        yn = [jnp.sum(jnp.where(diag, jnp.broadcast_to(yn_col[h], (HEAD_A, HEAD_A)), 0.0), axis=0, keepdims=True)
              for h in heads]
        lng, lnb, bonus, gt = rows(hs_lng), rows(hs_lnb), rows(hs_bonus), rows(hs_gate)
        o = [(yn[h] * lng[h] + lnb[h] + bonus[h]) * gt[h] for h in heads]
        return tuple(jnp.where(rowb == b, jnp.broadcast_to(o[h], (bb, HEAD_A)), outs[h]) for h in heads)

    outs = lax.fori_loop(0, bb, body, tuple(jnp.zeros((bb, HEAD_A), F32) for _ in heads))
    y_ref[...] = jnp.concatenate(outs, axis=1)


def _rwkv_step(u, prev_pad, mix_pad, pvec, w2p, a2p, s_all, s_acc, l, bb=SUBLANE):
    nb = u.shape[0]
    lfull = lambda shape: pl.BlockSpec((None,) + shape, lambda i: (l,) + tuple(0 for _ in shape))
    st_spec = pl.BlockSpec((None, bb, H_A, HEAD_A, HEAD_A), lambda i: (l, i, 0, 0, 0))
    return pl.pallas_call(
        functools.partial(_rwkv_step_kernel, bb=bb),
        out_shape=(jax.ShapeDtypeStruct((nb, MIX_W), F32),
                   jax.ShapeDtypeStruct(s_acc.shape, F32)),
        grid=(nb // bb,),
        in_specs=[pl.BlockSpec((bb, 3 * MIX_W), lambda i: (i, 0)),
                  pl.BlockSpec((bb, MIX_W), lambda i: (i, OFF_AG // MIX_W)),
                  pl.BlockSpec((bb, LANE), lambda i: (i, OFF_WD // LANE)),
                  pl.BlockSpec((bb, LANE), lambda i: (i, OFF_AD // LANE)),
                  pl.BlockSpec((None, bb, SHIFT_PAD), lambda i: (l, i, 0)),
                  lfull((1, SHIFT_PAD)), lfull((16, MIX_W)), lfull((LANE, MIX_W)), lfull((LANE, MIX_W)),
                  st_spec,
                  pl.BlockSpec(memory_space=pl.ANY)],
        out_specs=(pl.BlockSpec((bb, MIX_W), lambda i: (i, 0)), st_spec),
        scratch_shapes=[pltpu.VMEM((10, H_A, bb, HEAD_A), F32)],
        input_output_aliases={10: 1},
        compiler_params=_cparams(("parallel",)),
        name="rwkv_step",
    )(u, u, u, u, prev_pad, mix_pad, pvec, w2p, a2p, s_all, s_acc)


LP_CONVB, LP_BA, LP_BX, LP_LAM = range(4)


def _lru_gates(xc, wa_ref, wx_ref, lp):
    r = _sigmoid(_mm_bf16(xc, wa_ref[...]) + lp[LP_BA:LP_BA + 1])
    i = _sigmoid(_mm_bf16(xc, wx_ref[...]) + lp[LP_BX:LP_BX + 1])
    log_a = -LRU_C * r * _softplus(-lp[LP_LAM:LP_LAM + 1])
    a = jnp.exp(log_a)
    b = jnp.sqrt(jnp.tanh(-log_a) * (a * a + 1.0)) * (i * xc)
    return a, b


def _lru_prompt_kernel(x_ref, g_ref, cw_ref, lp_ref, wa_ref, wx_ref, cinit_ref, h0_ref,
                       y_ref, hout_ref, xbuf, hcar, *, n_blocks, tc):
    ti = pl.program_id(1)

    @pl.when(ti == 0)
    def _():
        xbuf[0:SUBLANE, :] = cinit_ref[...]
        hcar[...] = h0_ref[...]

    x = x_ref[...]
    xbuf[SUBLANE:SUBLANE + tc, :] = x
    cw = cw_ref[...]
    lp = lp_ref[...]
    xc = (xbuf[pl.ds(SUBLANE - 3, tc), :] * cw[0:1] + xbuf[pl.ds(SUBLANE - 2, tc), :] * cw[1:2]
          + xbuf[pl.ds(SUBLANE - 1, tc), :] * cw[2:3] + x * cw[3:4] + lp[LP_CONVB:LP_CONVB + 1])
    xbuf[0:SUBLANE, :] = x[tc - SUBLANE:tc, :]
    a, b = _lru_gates(xc, wa_ref, wx_ref, lp)
    rows = lax.broadcasted_iota(jnp.int32, (tc, MIX_W), 0)
    s = 1
    while s < tc:
        a_sh = pltpu.roll(a, s, 0)
        b_sh = pltpu.roll(b, s, 0)
        m = rows >= s
        b = jnp.where(m, a * b_sh + b, b)
        a = jnp.where(m, a * a_sh, a)
        s *= 2
    h = b + a * hcar[...]
    hcar[...] = h[tc - 1:tc, :]
    y_ref[...] = (h * _silu(g_ref[...])).astype(y_ref.dtype)

    @pl.when(ti == n_blocks - 1)
    def _():
        hout_ref[...] = h[tc - 1:tc, :]


def _lru_prompt(u, nb, t, conv_w, lp, wa_d, wx_d, cinit, h0, l, tc=256):
    nt = t // tc
    const = lambda shape: pl.BlockSpec((None,) + shape, lambda b, i: (l,) + tuple(0 for _ in shape))
    return pl.pallas_call(
        functools.partial(_lru_prompt_kernel, n_blocks=nt, tc=tc),
        out_shape=(jax.ShapeDtypeStruct((nb * t, MIX_W), BF16),
                   jax.ShapeDtypeStruct((nb, 1, MIX_W), F32)),
        grid=(nb, nt),
        in_specs=[pl.BlockSpec((tc, MIX_W), lambda b, i: (b * nt + i, OFF_BX // MIX_W)),
                  pl.BlockSpec((tc, MIX_W), lambda b, i: (b * nt + i, OFF_BG // MIX_W)),
                  const((CONV_W, MIX_W)), const((SUBLANE, MIX_W)),
                  const((MIX_W, MIX_W)), const((MIX_W, MIX_W)),
                  pl.BlockSpec((None, SUBLANE, MIX_W), lambda b, i: (b, 0, 0)),
                  pl.BlockSpec((None, 1, MIX_W), lambda b, i: (b, 0, 0))],
        out_specs=(pl.BlockSpec((tc, MIX_W), lambda b, i: (b * nt + i, 0)),
                   pl.BlockSpec((None, 1, MIX_W), lambda b, i: (b, 0, 0))),
        scratch_shapes=[pltpu.VMEM((tc + SUBLANE, MIX_W), F32), pltpu.VMEM((1, MIX_W), F32)],
        compiler_params=_cparams(("parallel", "arbitrary")),
        name="lru_prompt",
    )(u, u, conv_w, lp, wa_d, wx_d, cinit, h0)


def _lru_step_kernel(x_ref, g_ref, buf_ref, h0_ref, cw_ref, lp_ref, wa_ref, wx_ref, y_ref, hout_ref):
    x = x_ref[...]
    cw = cw_ref[...]
    lp = lp_ref[...]
    xc = buf_ref[0] * cw[0:1] + buf_ref[1] * cw[1:2] + buf_ref[2] * cw[2:3] + x * cw[3:4] + lp[LP_CONVB:LP_CONVB + 1]
    a, b = _lru_gates(xc, wa_ref, wx_ref, lp)
    h = a * h0_ref[...] + b
    hout_ref[...] = h
    y_ref[...] = h * _silu(g_ref[...])


def _lru_step(u, buf_t, h0, conv_w, lp, wa_d, wx_d, l):
    nb = u.shape[0]
    const = lambda shape: pl.BlockSpec((None,) + shape, lambda i: (l,) + tuple(0 for _ in shape))
    out = pl.BlockSpec((nb, MIX_W), lambda i: (0, 0))
    return pl.pallas_call(
        _lru_step_kernel,
        out_shape=(jax.ShapeDtypeStruct((nb, MIX_W), F32), jax.ShapeDtypeStruct((nb, MIX_W), F32)),
        grid=(1,),
        in_specs=[pl.BlockSpec((nb, MIX_W), lambda i: (0, OFF_BX // MIX_W)),
                  pl.BlockSpec((nb, MIX_W), lambda i: (0, OFF_BG // MIX_W)),
                  const((CONV_W - 1, nb, MIX_W)), const((nb, MIX_W)),
                  const((CONV_W, MIX_W)), const((SUBLANE, MIX_W)),
                  const((MIX_W, MIX_W)), const((MIX_W, MIX_W))],
        out_specs=(out, out),
        compiler_params=_cparams(("arbitrary",)),
        name="lru_step",
    )(u, u, buf_t, h0, conv_w, lp, wa_d, wx_d)


GP_ALOG, GP_DTB, GP_NORM = range(3)


def _gdn_prompt_kernel(q_ref, k_ref, v_ref, z_ref, ab_ref, cwq_ref, cwk_ref, cwv_ref, gp_ref,
                       ciq_ref, cik_ref, civ_ref, s0_ref, y_ref, sout_ref, xbuf, st, *, n_chunks, heads):
    c = GDN_CHUNK
    hw = heads * LANE
    h_base = pl.program_id(1) * heads
    ti = pl.program_id(2)

    @pl.when(ti == 0)
    def _():
        xbuf[0:SUBLANE, :] = jnp.concatenate([ciq_ref[...], cik_ref[...], civ_ref[...]], axis=1)
        st[...] = s0_ref[...]

    cur = jnp.concatenate([q_ref[...], k_ref[...], v_ref[...]], axis=1)
    xbuf[SUBLANE:SUBLANE + c, :] = cur
    cw = jnp.concatenate([cwq_ref[...], cwk_ref[...], cwv_ref[...]], axis=1)
    conv = (xbuf[pl.ds(SUBLANE - 3, c), :] * cw[0:1] + xbuf[pl.ds(SUBLANE - 2, c), :] * cw[1:2]
            + xbuf[pl.ds(SUBLANE - 1, c), :] * cw[2:3] + cur * cw[3:4])
    xbuf[0:SUBLANE, :] = cur[c - SUBLANE:c, :]
    act = _silu(conv)
    zg = _silu(z_ref[...])

    gp = gp_ref[...]
    lane_c = lax.broadcasted_iota(jnp.int32, (c, LANE), 1)
    ab = ab_ref[...]
    g_all = -jnp.exp(gp[GP_ALOG:GP_ALOG + 1]) * _softplus(ab + gp[GP_DTB:GP_DTB + 1])
    beta_all = _sigmoid(ab)

    row, col = _iota2(c, c)
    incl = row >= col
    strict = row > col
    eye = row == col
    mm = functools.partial(_dot, mode=P_CHUNK)

    hs_ = range(heads)
    qh = [act[:, j * DK:(j + 1) * DK] for j in hs_]
    kh = [act[:, hw + j * DK:hw + (j + 1) * DK] for j in hs_]
    vh = [act[:, 2 * hw + j * DV:2 * hw + (j + 1) * DV] for j in hs_]
    qn = [x * lax.rsqrt(jnp.sum(x * x, axis=-1, keepdims=True) + 1e-6) * (DK ** -0.5) for x in qh]
    kn = [x * lax.rsqrt(jnp.sum(x * x, axis=-1, keepdims=True) + 1e-6) for x in kh]
    g = [jnp.sum(jnp.where(lane_c == h_base + j, g_all, 0.0), axis=-1, keepdims=True) for j in hs_]
    beta = [jnp.sum(jnp.where(lane_c == h_base + j + H_C, beta_all, 0.0), axis=-1, keepdims=True) for j in hs_]

    gc = [_dot_exact_lhs(incl, jnp.broadcast_to(x, (c, LANE))) for x in g]
    gc_row = [jnp.sum(jnp.where(eye, x, 0.0), axis=0, keepdims=True) for x in gc]
    decay = [jnp.where(incl, jnp.exp(jnp.where(incl, gc[j] - gc_row[j], 0.0)), 0.0) for j in hs_]
    g_last = [x[c - 1:c, :] for x in gc]
    eg = [jnp.exp(x) for x in gc]

    kb = [kn[j] * beta[j] for j in hs_]
    p_all = [mm(jnp.concatenate([kb[j], qn[j]], axis=0), kn[j], _NT) for j in hs_]
    a_mat = [jnp.where(strict, p_all[j][0:c] * decay[j], 0.0) for j in hs_]
    qk = [p_all[j][c:2 * c] * decay[j] for j in hs_]
    t_inv = _inv_unit_lower(a_mat, c, levels=3)
    wu = [mm(t_inv[j], jnp.concatenate([kb[j] * eg[j], vh[j] * beta[j]], axis=1)) for j in hs_]
    k_dec = [kn[j] * jnp.exp(g_last[j] - gc[j]) for j in hs_]
    w3 = [mm(k_dec[j], wu[j], _TN) for j in hs_]
    w4 = [mm(qk[j], wu[j]) for j in hs_]
    phi = [jnp.where(eye, jnp.exp(g_last[j]), 0.0) - w3[j][:, 0:DK] for j in hs_]
    q1 = [qn[j] * eg[j] - w4[j][:, 0:DK] for j in hs_]

    s_old = [st[j] for j in hs_]
    s_new = [_dot(phi[j], s_old[j], _NN, P_STATE) + w3[j][:, DK:] for j in hs_]
    o = [_dot(q1[j], s_old[j], _NN, P_STATE) + w4[j][:, DK:] for j in hs_]
    for j in hs_:
        st[j] = s_new[j]

    norm_g = gp[GP_NORM:GP_NORM + 1]
    outs = [x * lax.rsqrt(jnp.mean(x * x, axis=-1, keepdims=True) + NORM_EPS) * norm_g * zg[:, j * DV:(j + 1) * DV]
            for j, x in enumerate(o)]
    y_ref[...] = jnp.concatenate(outs, axis=1).astype(y_ref.dtype)

    @pl.when(ti == n_chunks - 1)
    def _():
        sout_ref[...] = st[...]


def _gdn_prompt(u, nb, t, conv_w, gpar, cinit, s0, l, heads=8):
    c = GDN_CHUNK
    assert c == LANE and DK == LANE and DV == LANE
    nt = t // c
    hw = heads * LANE

    def uspec(off):
        return pl.BlockSpec((c, hw), lambda b, h, i: (b * nt + i, off // hw + h))

    def cwspec(off):
        return pl.BlockSpec((None, CONV_W, hw), lambda b, h, i: (l, 0, off // hw + h))

    def cispec(off):
        return pl.BlockSpec((None, SUBLANE, hw), lambda b, h, i: (b, 0, off // hw + h))

    return pl.pallas_call(
        functools.partial(_gdn_prompt_kernel, n_chunks=nt, heads=heads),
        out_shape=(jax.ShapeDtypeStruct((nb * t, MIX_W), BF16),
                   jax.ShapeDtypeStruct((nb, H_C, DK, DV), F32)),
        grid=(nb, H_C // heads, nt),
        in_specs=[uspec(OFF_Q), uspec(OFF_Q + MIX_W), uspec(OFF_Q + 2 * MIX_W), uspec(OFF_Z),
                  pl.BlockSpec((c, LANE), lambda b, h, i: (b * nt + i, OFF_AB // LANE)),
                  cwspec(0), cwspec(MIX_W), cwspec(2 * MIX_W),
                  pl.BlockSpec((None, SUBLANE, LANE), lambda b, h, i: (l, 0, 0)),
                  cispec(0), cispec(MIX_W), cispec(2 * MIX_W),
                  pl.BlockSpec((None, heads, DK, DV), lambda b, h, i: (b, h, 0, 0))],
        out_specs=(pl.BlockSpec((c, hw), lambda b, h, i: (b * nt + i, h)),
                   pl.BlockSpec((None, heads, DK, DV), lambda b, h, i: (b, h, 0, 0))),
        scratch_shapes=[pltpu.VMEM((c + SUBLANE, 3 * hw), F32), pltpu.VMEM((heads, DK, DV), F32)],
        compiler_params=_cparams(("parallel", "parallel", "arbitrary")),
        name="gdn_prompt",
    )(u, u, u, u, u, conv_w, conv_w, conv_w, gpar, cinit, cinit, cinit, s0)


def _gdn_step_kernel(qkv_ref, z_ref, ab_ref, buf_ref, cw_ref, gp_ref, s_ref, acc_ref, y_ref, sout_ref, hs, *, bb):
    del acc_ref
    cw = cw_ref[...]
    cur = qkv_ref[...]
    act = _silu(buf_ref[0] * cw[0:1] + buf_ref[1] * cw[1:2] + buf_ref[2] * cw[2:3] + cur * cw[3:4])
    gp = gp_ref[...]
    ab = ab_ref[...]
    g_all = -jnp.exp(gp[GP_ALOG:GP_ALOG + 1]) * _softplus(ab + gp[GP_DTB:GP_DTB + 1])
    beta_all = _sigmoid(ab)
    zg = _silu(z_ref[...])
    hs_q, hs_k, hs_v, hs_eg, hs_beta, hs_zg, hs_qk = range(7)
    for h in range(H_C):
        qh = act[:, h * DK:(h + 1) * DK]
        kh = act[:, MIX_W + h * DK:MIX_W + (h + 1) * DK]
        qn = qh * lax.rsqrt(jnp.sum(qh * qh, axis=-1, keepdims=True) + 1e-6) * (DK ** -0.5)
        kn = kh * lax.rsqrt(jnp.sum(kh * kh, axis=-1, keepdims=True) + 1e-6)
        hs[hs_q, h] = qn
        hs[hs_k, h] = kn
        hs[hs_v, h] = act[:, 2 * MIX_W + h * DV:2 * MIX_W + (h + 1) * DV]
        hs[hs_eg, h] = jnp.broadcast_to(jnp.exp(g_all[:, h:h + 1]), (bb, LANE))
        hs[hs_beta, h] = jnp.broadcast_to(beta_all[:, H_C + h:H_C + h + 1], (bb, LANE))
        hs[hs_zg, h] = zg[:, h * DV:(h + 1) * DV]
        hs[hs_qk, h] = jnp.broadcast_to(jnp.sum(qn * kn, axis=-1, keepdims=True), (bb, LANE))

    row, col = _iota2(DK, DV)
    diag = row == col
    rowb = lax.broadcasted_iota(jnp.int32, (bb, DV), 0)
    norm_g = gp[GP_NORM:GP_NORM + 1]
    heads = range(H_C)

    def body(b, outs):
        def rows(q):
            return [hs[q, h, pl.ds(b, 1), :] for h in heads]

        q, k, v, eg, beta = rows(hs_q), rows(hs_k), rows(hs_v), rows(hs_eg), rows(hs_beta)
        s = [s_ref[b, h] for h in heads]
        kcol = [jnp.sum(jnp.where(diag, jnp.broadcast_to(x, (DK, DV)), 0.0), axis=-1, keepdims=True) for x in k]
        qcol = [jnp.sum(jnp.where(diag, jnp.broadcast_to(x, (DK, DV)), 0.0), axis=-1, keepdims=True) for x in q]
        ks = [jnp.sum(kcol[h] * s[h], axis=0, keepdims=True) for h in heads]
        qs = [jnp.sum(qcol[h] * s[h], axis=0, keepdims=True) for h in heads]
        v_new = [beta[h] * (v[h] - eg[h] * ks[h]) for h in heads]
        for h in heads:
            sout_ref[b, h] = s[h] * eg[h] + kcol[h] * v_new[h]
        qk = rows(hs_qk)
        o = [eg[h] * qs[h] + qk[h] * v_new[h] for h in heads]
        ms = [jnp.mean(x * x, axis=-1, keepdims=True) for x in o]
        zg_h = rows(hs_zg)
        res = [o[h] * lax.rsqrt(ms[h] + NORM_EPS) * norm_g * zg_h[h] for h in heads]
        return tuple(jnp.where(rowb == b, jnp.broadcast_to(res[h], (bb, DV)), outs[h]) for h in heads)

    outs = lax.fori_loop(0, bb, body, tuple(jnp.zeros((bb, DV), F32) for _ in heads))
    y_ref[...] = jnp.concatenate(outs, axis=1)


def _gdn_step(u, buf_t, conv_w, gpar, s_all, s_acc, l, bb=SUBLANE):
    nb = u.shape[0]
    lfull = lambda shape: pl.BlockSpec((None,) + shape, lambda i: (l,) + tuple(0 for _ in shape))
    st_spec = pl.BlockSpec((None, bb, H_C, DK, DV), lambda i: (l, i, 0, 0, 0))
    return pl.pallas_call(
        functools.partial(_gdn_step_kernel, bb=bb),
        out_shape=(jax.ShapeDtypeStruct((nb, MIX_W), F32),
                   jax.ShapeDtypeStruct(s_acc.shape, F32)),
        grid=(nb // bb,),
        in_specs=[pl.BlockSpec((bb, QKV_W), lambda i: (i, OFF_Q // QKV_W)),
                  pl.BlockSpec((bb, MIX_W), lambda i: (i, OFF_Z // MIX_W)),
                  pl.BlockSpec((bb, LANE), lambda i: (i, OFF_AB // LANE)),
                  pl.BlockSpec((None, CONV_W - 1, bb, QKV_W), lambda i: (l, 0, i, 0)),
                  lfull((CONV_W, QKV_W)), lfull((SUBLANE, LANE)),
                  st_spec,
                  pl.BlockSpec(memory_space=pl.ANY)],
        out_specs=(pl.BlockSpec((bb, MIX_W), lambda i: (i, 0)), st_spec),
        scratch_shapes=[pltpu.VMEM((7, H_C, bb, LANE), F32)],
        input_output_aliases={7: 1},
        compiler_params=_cparams(("parallel",)),
        name="gdn_step",
    )(u, u, u, buf_t, conv_w, gpar, s_all, s_acc)


def _pad_cols(w, width):
    pad = width - w.shape[-1]
    return jnp.pad(w, [(0, 0)] * (w.ndim - 1) + [(0, pad)])


def _layout_in_cols(w):
    a_sh_end = A_SHIFT_W
    o_ag = a_sh_end
    o_bx = o_ag + MIX_W
    o_bg = o_bx + MIX_W
    o_q = o_bg + MIX_W
    o_a = o_q + QKV_W
    o_z = o_a + 2 * H_C
    o_m = o_z + MIX_W
    parts = [w[..., 0:3 * MIX_W], w[..., o_ag:o_bx], w[..., o_bx:o_bg], w[..., o_bg:o_q], w[..., o_q:o_a],
             w[..., o_z:o_m], w[..., o_m:IN_COLS],
             _pad_cols(w[..., 3 * MIX_W:3 * MIX_W + LORA], LANE),
             _pad_cols(w[..., 3 * MIX_W + LORA:a_sh_end], LANE),
             _pad_cols(w[..., o_a:o_z], LANE)]
    used = sum(part.shape[-1] for part in parts)
    parts.append(jnp.zeros(w.shape[:-1] + (N_PAD - used,), w.dtype))
    return jnp.concatenate(parts, axis=-1)


def _shift_to_padded(s):
    return jnp.concatenate([s[..., 0:3 * MIX_W], _pad_cols(s[..., 3 * MIX_W:3 * MIX_W + LORA], LANE),
                            _pad_cols(s[..., 3 * MIX_W + LORA:], LANE)], axis=-1)


def _shift_from_u(u_rows):
    return jnp.concatenate([u_rows[..., 0:3 * MIX_W], u_rows[..., OFF_WD:OFF_WD + LORA],
                            u_rows[..., OFF_AD:OFF_AD + LORA]], axis=-1)


def _block_diag(w):
    eye = jnp.eye(LRU_BLOCKS, dtype=w.dtype)
    return jnp.einsum('lnij,nm->lnimj', w, eye).reshape(w.shape[0], MIX_W, MIX_W)


def _rows_table(rows, n_rows):
    tab = jnp.stack(rows, axis=1)
    return jnp.pad(tab, ((0, 0), (0, n_rows - len(rows)), (0, 0)))


def _prep_params(p):
    depth = p['rwkv_mix'].shape[0]
    mix = p['rwkv_mix']
    pvec = _rows_table([mix[:, 0:MIX_W], mix[:, MIX_W:2 * MIX_W], mix[:, 2 * MIX_W:3 * MIX_W],
                        p['rwkv_w0'], p['rwkv_a0'], p['rwkv_kk'], p['rwkv_ka'], p['rwkv_rk'].reshape(depth, MIX_W),
                        p['rwkv_lnx_g'], p['rwkv_lnx_b']], 16)
    mix_pad = _shift_to_padded(mix)
    mixl = _rows_table([mix_pad[:, 3 * MIX_W:3 * MIX_W + LANE], mix_pad[:, 3 * MIX_W + LANE:]], SUBLANE)
    lora_pad = ((0, 0), (0, LANE - LORA), (0, 0))
    lp = _rows_table([p['lru_conv_b'], p['lru_ba'], p['lru_bx'], p['lru_lambda']], SUBLANE)
    gpar = _rows_table([_pad_cols(p['gdn_a_log'], LANE), _pad_cols(p['gdn_dt_bias'], LANE), p['gdn_norm_g']], SUBLANE)
    return dict(pvec=pvec, mix_pad=mix_pad[:, None], mixl=mixl,
                w2p=jnp.pad(p['rwkv_w2'], lora_pad), a2p=jnp.pad(p['rwkv_a2'], lora_pad),
                lru_cw=p['lru_conv_w'], lp=lp,
                wa_d=_block_diag(p['lru_wa']).astype(BF16), wx_d=_block_diag(p['lru_wx']).astype(BF16),
                gdn_cw=p['gdn_conv_w'], gpar=gpar)


def _prompt_layer(x2d, nb, t, l, pp, zeros, mod_p, norm_g3, w_in_p, w_branch_b, w_out_b):
    u = _inproj(x2d, mod_p, norm_g3, w_in_p, l, rows_per_mod=t, tm=PROMPT_TM, tn=INPROJ_TN)
    ya, wkv = _rwkv_prompt(u, nb, t, pp['pvec'], pp['mixl'], pp['w2p'], pp['a2p'], zeros['shift'], zeros['wkv'], l)
    yb, h_last = _lru_prompt(u, nb, t, pp['lru_cw'], pp['lp'], pp['wa_d'], pp['wx_d'],
                             zeros['lru_conv'], zeros['lru_h'], l)
    yc, ssm = _gdn_prompt(u, nb, t, pp['gdn_cw'], pp['gpar'], zeros['gdn_conv'], zeros['ssm'], l)
    merged = _merge(ya, yb, yc, u, w_branch_b, l, tm=PROMPT_TM, tn=MERGE_TN)
    x_new = _outproj(merged, w_out_b, x2d, mod_p, l, rows_per_mod=t, tm=PROMPT_TM, tn=OUTPROJ_TN)
    u3 = u.reshape(nb, t, N_PAD)
    states = (_shift_from_u(u3[:, t - 1]), wkv,
              u3[:, t - (CONV_W - 1):, OFF_BX:OFF_BX + MIX_W], h_last.reshape(nb, MIX_W),
              u3[:, t - (CONV_W - 1):, OFF_Q:OFF_Q + QKV_W], ssm)
    return x_new, states


def _sample_layer(x2d, l, pp, st, acc, mod_s, norm_g3, w_in_p, w_branch_b, w_out_b):
    nb = x2d.shape[0]
    u = _inproj(x2d, mod_s, norm_g3, w_in_p, l, rows_per_mod=1, tm=nb, tn=INPROJ_TN)
    ya, wkv_acc = _rwkv_step(u, st['shift'], pp['mix_pad'], pp['pvec'], pp['w2p'], pp['a2p'], st['wkv'], acc[0], l)
    yb, h_new = _lru_step(u, st['lru_conv_t'], st['lru_h'], pp['lru_cw'], pp['lp'], pp['wa_d'], pp['wx_d'], l)
    yc, ssm_acc = _gdn_step(u, st['gdn_conv_t'], pp['gdn_cw'], pp['gpar'], st['ssm'], acc[1], l)
    merged = _merge(ya, yb, yc, u, w_branch_b, l, tm=nb, tn=OUTPROJ_TN)
    x_new = _outproj(merged, w_out_b, x2d, mod_s, l, rows_per_mod=1, tm=nb, tn=OUTPROJ_TN)
    rows = (_shift_from_u(u), u[:, OFF_BX:OFF_BX + MIX_W], h_new, u[:, OFF_Q:OFF_Q + QKV_W])
    return x_new, rows, (wkv_acc, ssm_acc)


def kernel(x_prompt, x_sample, state_rwkv_shift, state_rwkv_wkv, state_lru_conv, state_lru_h, state_gdn_conv, state_gdn_ssm, c_prompt, c_sample, ada_w, ada_b, norm_g, w_in, rwkv_mix, rwkv_w0, rwkv_w2, rwkv_a0, rwkv_a2, rwkv_kk, rwkv_ka, rwkv_rk, rwkv_lnx_g, rwkv_lnx_b, lru_conv_w, lru_conv_b, lru_wa, lru_ba, lru_wx, lru_bx, lru_lambda, gdn_conv_w, gdn_a_log, gdn_dt_bias, gdn_norm_g, w_branch, w_out, final_g):
    p = dict(rwkv_mix=rwkv_mix, rwkv_w0=rwkv_w0, rwkv_w2=rwkv_w2, rwkv_a0=rwkv_a0, rwkv_a2=rwkv_a2,
             rwkv_kk=rwkv_kk, rwkv_ka=rwkv_ka, rwkv_rk=rwkv_rk, rwkv_lnx_g=rwkv_lnx_g, rwkv_lnx_b=rwkv_lnx_b,
             lru_conv_w=lru_conv_w, lru_conv_b=lru_conv_b, lru_wa=lru_wa, lru_ba=lru_ba, lru_wx=lru_wx,
             lru_bx=lru_bx, lru_lambda=lru_lambda, gdn_conv_w=gdn_conv_w, gdn_a_log=gdn_a_log,
             gdn_dt_bias=gdn_dt_bias, gdn_norm_g=gdn_norm_g)
    depth = w_in.shape[0]
    bp, t, _ = x_prompt.shape
    bs = x_sample.shape[0]

    w_in_p = _layout_in_cols(w_in).astype(BF16)
    w_branch_b = w_branch.astype(BF16)
    w_out_b = w_out.astype(BF16)
    norm_g3 = norm_g.reshape(depth, 1, D_MODEL)

    rows = bp + bs
    rows_pad = -(-rows // SUBLANE) * SUBLANE
    c_all = jnp.pad(jnp.concatenate([c_prompt, c_sample], axis=0), ((0, rows_pad - rows), (0, 0)))
    mod = _ada_all(c_all, ada_w, ada_b)
    mod_p = mod[:, 0:bp].reshape(depth, bp, 1, 3 * D_MODEL)
    mod_s = mod[:, bp:bp + bs]

    pp = _prep_params(p)
    zeros = dict(shift=jnp.zeros((bp, SUBLANE, SHIFT_PAD), F32), wkv=jnp.zeros((bp, H_A, HEAD_A, HEAD_A), F32),
                 lru_conv=jnp.zeros((bp, SUBLANE, MIX_W), F32), lru_h=jnp.zeros((bp, 1, MIX_W), F32),
                 gdn_conv=jnp.zeros((bp, SUBLANE, QKV_W), F32), ssm=jnp.zeros((bp, H_C, DK, DV), F32))
    st = dict(shift=_shift_to_padded(state_rwkv_shift), wkv=state_rwkv_wkv,
              lru_conv_t=jnp.swapaxes(state_lru_conv, 1, 2), lru_h=state_lru_h,
              gdn_conv_t=jnp.swapaxes(state_gdn_conv, 1, 2), ssm=state_gdn_ssm)
    acc = (jnp.zeros(state_rwkv_wkv.shape, F32), jnp.zeros(state_gdn_ssm.shape, F32))

    xp = x_prompt.reshape(bp * t, D_MODEL)
    xs = x_sample.reshape(bs, D_MODEL)
    p_states, s_rows = [], []
    for l in range(depth):
        xp, pst = _prompt_layer(xp, bp, t, l, pp, zeros, mod_p, norm_g3, w_in_p, w_branch_b, w_out_b)
        p_states.append(pst)
        xs, rows_l, acc = _sample_layer(xs, l, pp, st, acc, mod_s, norm_g3, w_in_p, w_branch_b, w_out_b)
        s_rows.append(rows_l)

    y_prompt = _final_norm(xp, final_g, tm=512).reshape(bp, t, D_MODEL)
    y_sample = _final_norm(xs, final_g, tm=bs).reshape(bs, 1, D_MODEL)
    p_out = tuple(jnp.stack([s[i] for s in p_states]) for i in range(6))
    s_shift, s_bx, s_h, s_qkv = (jnp.stack([r[i] for r in s_rows]) for i in range(4))
    s_lru_conv = jnp.concatenate([state_lru_conv[:, :, 1:], s_bx[:, :, None]], axis=2)
    s_gdn_conv = jnp.concatenate([state_gdn_conv[:, :, 1:], s_qkv[:, :, None]], axis=2)
    return (y_prompt, y_sample) + p_out + (s_shift, acc[0], s_lru_conv, s_h, s_gdn_conv, acc[1])
```

```python
import functools
import math

import jax
import jax.numpy as jnp
import numpy as np
from jax import lax
from jax.experimental import pallas as pl
from jax.experimental.pallas import tpu as pltpu

F32 = jnp.float32
BF16 = jnp.bfloat16
HIGHEST = lax.Precision.HIGHEST

D_MODEL = 2048
DEPTH = 4
MIX_W = 1024
HEAD_A = 64
H_A = 16
LORA = 96
GN_EPS_A = 64e-5
A_SHIFT_W = 3 * MIX_W + 2 * LORA
LRU_BLOCKS = 16
LRU_BS = 64
LRU_C = 8.0
CONV_W = 4
DK = 128
DV = 128
H_C = 8
QKV_W = 3 * MIX_W
NORM_EPS = 1e-6
IN_COLS = 16592

LANE = 128
SUBLANE = 8

OFF_R, OFF_K, OFF_V = 0, 1024, 2048
OFF_AG = 3072
OFF_BX, OFF_BG = 4096, 5120
OFF_Q = 6144
OFF_Z = 9216
OFF_MERGE = 10240
OFF_WD, OFF_AD, OFF_AB = 16384, 16512, 16640
N_PAD = 16896
SHIFT_PAD = 3 * MIX_W + 2 * LANE

RWKV_CHUNK = 64
GDN_CHUNK = 128
VMEM_LIMIT = 56 * 1024 * 1024

PROMPT_TM = 1024
INPROJ_TN = 1536
MERGE_TN = 512
OUTPROJ_TN = 1024


def _cparams(sem):
    return pltpu.CompilerParams(dimension_semantics=sem, vmem_limit_bytes=VMEM_LIMIT)


def _sigmoid(x):
    return 1.0 / (1.0 + jnp.exp(-x))


def _silu(x):
    return x * _sigmoid(x)


def _softplus(x):
    return jnp.maximum(x, 0.0) + jnp.log1p(jnp.exp(-jnp.abs(x)))


_NN = (((1,), (0,)), ((), ()))
_NT = (((1,), (1,)), ((), ()))
_TN = (((0,), (0,)), ((), ()))

P_LORA = "bf16"
P_CHUNK = "bf16"
P_INV = "bf16"
P_STATE = "bf16"
P_STEP = "f32"


def _dg(a, b, dims):
    return lax.dot_general(a, b, dims, preferred_element_type=F32)


def _dot(a, b, dims=_NN, mode="bf16"):
    if mode == "f32":
        return lax.dot_general(a, b, dims, preferred_element_type=F32, precision=HIGHEST)
    ah, bh = a.astype(BF16), b.astype(BF16)
    if mode == "bf16":
        return _dg(ah, bh, dims)
    al = (a - ah.astype(F32)).astype(BF16)
    bl = (b - bh.astype(F32)).astype(BF16)
    return _dg(ah, bh, dims) + (_dg(al, bh, dims) + _dg(ah, bl, dims))


def _dot_exact_lhs(a01, x):
    a = a01
    x1 = x.astype(BF16)
    r1 = x - x1.astype(F32)
    x2 = r1.astype(BF16)
    x3 = (r1 - x2.astype(F32)).astype(BF16)
    return _dg(a, x1, _NN) + (_dg(a, x2, _NN) + _dg(a, x3, _NN))


def _mm_bf16(a, b):
    return jnp.dot(a.astype(BF16), b.astype(BF16), preferred_element_type=F32)


def _iota2(n, m):
    return (lax.broadcasted_iota(jnp.int32, (n, m), 0), lax.broadcasted_iota(jnp.int32, (n, m), 1))


def _inv_unit_lower(mats, n, levels):
    mm = functools.partial(_dot, mode=P_INV)
    row, col = _iota2(n, n)
    eye = (row == col).astype(F32)
    same = (row >> 4) == (col >> 4)
    d = [jnp.where(same, a, 0.0) for a in mats]
    o = [a - di for a, di in zip(mats, d)]
    x1 = [-di for di in d]
    x2 = [mm(x, x) for x in x1]
    td = [mm(eye + a, eye + b) for a, b in zip(x1, x2)]
    x4 = [mm(x, x) for x in x2]
    td = [mm(t, eye + x) for t, x in zip(td, x4)]
    x8 = [mm(x, x) for x in x4]
    td = [mm(t, eye + x) for t, x in zip(td, x8)]
    p = [-mm(t, oi) for t, oi in zip(td, o)]
    t = [eye + pi for pi in p]
    for _ in range(levels - 1):
        p = [mm(pi, pi) for pi in p]
        t = [mm(ti, eye + pi) for ti, pi in zip(t, p)]
    return [mm(ti, tdi) for ti, tdi in zip(t, td)]


def _ada_kernel(c_ref, w_ref, b_ref, o_ref):
    h = _silu(c_ref[...]).astype(BF16)
    o_ref[...] = jnp.dot(h, w_ref[...].astype(BF16), preferred_element_type=F32) + b_ref[...]


def _ada_all(c_all, ada_w, ada_b, tn=512):
    rows = c_all.shape[0]
    depth = ada_w.shape[0]
    n = ada_w.shape[2]
    return pl.pallas_call(
        _ada_kernel,
        out_shape=jax.ShapeDtypeStruct((depth, rows, n), F32),
        grid=(depth, n // tn),
        in_specs=[pl.BlockSpec((rows, D_MODEL), lambda l, j: (0, 0)),
                  pl.BlockSpec((None, D_MODEL, tn), lambda l, j: (l, 0, j)),
                  pl.BlockSpec((None, 1, tn), lambda l, j: (l, 0, j))],
        out_specs=pl.BlockSpec((None, rows, tn), lambda l, j: (l, 0, j)),
        compiler_params=_cparams(("parallel", "parallel")),
        name="ada_mod",
    )(c_all, ada_w, ada_b.reshape(depth, 1, n))


def _inproj_kernel(x_ref, shift_ref, scale_ref, g_ref, w_ref, o_ref, h_ref):
    @pl.when(pl.program_id(1) == 0)
    def _():
        x = x_ref[...]
        y = x * lax.rsqrt(jnp.mean(x * x, axis=-1, keepdims=True) + NORM_EPS) * g_ref[...]
        h_ref[...] = (y * (1.0 + scale_ref[...]) + shift_ref[...]).astype(BF16)

    o_ref[...] = jnp.dot(h_ref[...], w_ref[...], preferred_element_type=F32)


def _inproj(x2d, mod, norm_g3, w_in_p, l, rows_per_mod, tm, tn):
    m = x2d.shape[0]
    n = w_in_p.shape[2]
    if rows_per_mod > 1:
        def mspec(part):
            return pl.BlockSpec((None, None, 1, D_MODEL), lambda i, j: (l, (i * tm) // rows_per_mod, 0, part))
    else:
        def mspec(part):
            return pl.BlockSpec((None, tm, D_MODEL), lambda i, j: (l, i, part))
    return pl.pallas_call(
        _inproj_kernel,
        out_shape=jax.ShapeDtypeStruct((m, n), F32),
        grid=(m // tm, n // tn),
        in_specs=[pl.BlockSpec((tm, D_MODEL), lambda i, j: (i, 0)),
                  mspec(0), mspec(1),
                  pl.BlockSpec((None, 1, D_MODEL), lambda i, j: (l, 0, 0)),
                  pl.BlockSpec((None, D_MODEL, tn), lambda i, j: (l, 0, j))],
        out_specs=pl.BlockSpec((tm, tn), lambda i, j: (i, j)),
        scratch_shapes=[pltpu.VMEM((tm, D_MODEL), BF16)],
        compiler_params=_cparams(("parallel", "arbitrary")),
        name="inproj",
    )(x2d, mod, mod, norm_g3, w_in_p)


def _merge_kernel(ya_ref, yb_ref, yc_ref, ga_ref, gb_ref, gc_ref, wa_ref, wb_ref, wc_ref, o_ref):
    acc = _sigmoid(ga_ref[...]) * jnp.dot(ya_ref[...].astype(BF16), wa_ref[...], preferred_element_type=F32)
    acc += _sigmoid(gb_ref[...]) * jnp.dot(yb_ref[...].astype(BF16), wb_ref[...], preferred_element_type=F32)
    acc += _sigmoid(gc_ref[...]) * jnp.dot(yc_ref[...].astype(BF16), wc_ref[...], preferred_element_type=F32)
    o_ref[...] = acc.astype(BF16)


def _merge(ya, yb, yc, u, w_branch_b, l, tm, tn):
    m = ya.shape[0]
    yspec = pl.BlockSpec((tm, MIX_W), lambda i, j: (i, 0))

    def gspec(br):
        return pl.BlockSpec((tm, tn), lambda i, j: (i, (OFF_MERGE + br * D_MODEL) // tn + j))

    def wspec(br):
        return pl.BlockSpec((None, None, MIX_W, tn), lambda i, j: (l, br, 0, j))

    return pl.pallas_call(
        _merge_kernel,
        out_shape=jax.ShapeDtypeStruct((m, D_MODEL), BF16),
        grid=(m // tm, D_MODEL // tn),
        in_specs=[yspec, yspec, yspec, gspec(0), gspec(1), gspec(2), wspec(0), wspec(1), wspec(2)],
        out_specs=pl.BlockSpec((tm, tn), lambda i, j: (i, j)),
        compiler_params=_cparams(("parallel", "parallel")),
        name="branch_merge",
    )(ya, yb, yc, u, u, u, w_branch_b, w_branch_b, w_branch_b)


def _outproj_kernel(m_ref, w_ref, x_ref, gate_ref, o_ref):
    o_ref[...] = x_ref[...] + gate_ref[...] * jnp.dot(m_ref[...], w_ref[...], preferred_element_type=F32)


def _outproj(merged, w_out_b, x2d, mod, l, rows_per_mod, tm, tn):
    m = x2d.shape[0]
    gate_blk = (2 * D_MODEL) // tn
    if rows_per_mod > 1:
        gspec = pl.BlockSpec((None, None, 1, tn), lambda i, j: (l, (i * tm) // rows_per_mod, 0, gate_blk + j))
    else:
        gspec = pl.BlockSpec((None, tm, tn), lambda i, j: (l, i, gate_blk + j))
    return pl.pallas_call(
        _outproj_kernel,
        out_shape=jax.ShapeDtypeStruct((m, D_MODEL), F32),
        grid=(m // tm, D_MODEL // tn),
        in_specs=[pl.BlockSpec((tm, D_MODEL), lambda i, j: (i, 0)),
                  pl.BlockSpec((None, D_MODEL, tn), lambda i, j: (l, 0, j)),
                  pl.BlockSpec((tm, tn), lambda i, j: (i, j)),
                  gspec],
        out_specs=pl.BlockSpec((tm, tn), lambda i, j: (i, j)),
        compiler_params=_cparams(("parallel", "parallel")),
        name="outproj",
    )(merged, w_out_b, x2d, mod)


def _final_norm_kernel(x_ref, g_ref, o_ref):
    x = x_ref[...]
    o_ref[...] = x * lax.rsqrt(jnp.mean(x * x, axis=-1, keepdims=True) + NORM_EPS) * g_ref[...]


def _final_norm(x2d, g, tm):
    m = x2d.shape[0]
    return pl.pallas_call(
        _final_norm_kernel,
        out_shape=jax.ShapeDtypeStruct((m, D_MODEL), F32),
        grid=(m // tm,),
        in_specs=[pl.BlockSpec((tm, D_MODEL), lambda i: (i, 0)), pl.BlockSpec((1, D_MODEL), lambda i: (0, 0))],
        out_specs=pl.BlockSpec((tm, D_MODEL), lambda i: (i, 0)),
        compiler_params=_cparams(("parallel",)),
        name="final_norm",
    )(x2d, g.reshape(1, D_MODEL))


PV_MIX_R, PV_MIX_K, PV_MIX_V, PV_W0, PV_A0, PV_KK, PV_KA, PV_RK, PV_LNG, PV_LNB = range(10)


def _half_sum(x, lo_mask):
    s_lo = jnp.sum(jnp.where(lo_mask, x, 0.0), axis=-1, keepdims=True)
    s_hi = jnp.sum(jnp.where(lo_mask, 0.0, x), axis=-1, keepdims=True)
    return jnp.where(lo_mask, s_lo, s_hi)


def _rwkv_prompt_kernel(r_ref, k_ref, v_ref, g_ref, wd_ref, ad_ref, pv_ref, mixl_ref, w2_ref, a2_ref,
                        shr_ref, shk_ref, shv_ref, shwd_ref, shad_ref, s0_ref,
                        y_ref, sout_ref, xbuf, sb, *, n_chunks, pairs, nck):
    c = RWKV_CHUNK
    gw = pairs * LANE
    ti = pl.program_id(2)
    lo = lax.broadcasted_iota(jnp.int32, (1, LANE), 1) < HEAD_A

    @pl.when(ti == 0)
    def _():
        xbuf[0:SUBLANE, :] = jnp.concatenate(
            [shr_ref[...], shk_ref[...], shv_ref[...], shwd_ref[...], shad_ref[...]], axis=1)
        zero = jnp.zeros((HEAD_A, HEAD_A), F32)
        for g in range(pairs):
            top = jnp.concatenate([s0_ref[2 * g], zero], axis=1)
            bot = jnp.concatenate([zero, s0_ref[2 * g + 1]], axis=1)
            sb[g] = jnp.concatenate([top, bot], axis=0)

    tc = nck * c
    cur = jnp.concatenate([r_ref[...], k_ref[...], v_ref[...], wd_ref[...], ad_ref[...]], axis=1)
    xbuf[SUBLANE:SUBLANE + tc, :] = cur
    prev = xbuf[pl.ds(SUBLANE - 1, tc), :]
    pv = pv_ref[...]
    mixl = mixl_ref[...]
    mix = jnp.concatenate([pv[PV_MIX_R:PV_MIX_R + 1], pv[PV_MIX_K:PV_MIX_K + 1], pv[PV_MIX_V:PV_MIX_V + 1],
                           mixl[0:1], mixl[1:2]], axis=1)
    z = cur + (prev - cur) * mix
    xbuf[0:SUBLANE, :] = cur[tc - SUBLANE:tc, :]

    zr, zk, zv = z[:, 0:gw], z[:, gw:2 * gw], z[:, 2 * gw:3 * gw]
    zwd, zad = z[:, 3 * gw:3 * gw + LANE], z[:, 3 * gw + LANE:]
    w0, a0 = pv[PV_W0:PV_W0 + 1], pv[PV_A0:PV_A0 + 1]
    k_k, k_a, r_k = pv[PV_KK:PV_KK + 1], pv[PV_KA:PV_KA + 1], pv[PV_RK:PV_RK + 1]
    ln_g, ln_b = pv[PV_LNG:PV_LNG + 1], pv[PV_LNB:PV_LNB + 1]

    logw = -_softplus(-(w0 + _dot(jnp.tanh(zwd), w2_ref[...], mode=P_LORA))) - 0.5
    lw = -jnp.exp(logw)
    a = _sigmoid(a0 + _dot(zad, a2_ref[...], mode=P_LORA))
    kkp = zk * k_k
    kmod = zk * (1.0 + (a - 1.0) * k_a)

    rowc, colc = _iota2(tc, tc)
    chunk_lower = ((rowc >= colc) & ((rowc >> 6) == (colc >> 6))).astype(BF16)
    gcum = _dot_exact_lhs(chunk_lower, lw)
    g_end_rows = [gcum[k * c + c - 1:k * c + c, :] for k in range(nck)]
    g_end = jnp.concatenate([jnp.broadcast_to(x, (c, gw)) for x in g_end_rows], axis=0)
    e_pos, e_neg = jnp.exp(gcum), jnp.exp(-gcum)
    e_prev, e_end = jnp.exp(gcum - lw), jnp.exp(g_end - gcum)
    rt, kb, kend = zr * e_pos, kmod * e_neg, kmod * e_end
    gate = _silu(g_ref[...])
    rk_prod = zr * kmod * r_k

    n = 2 * c
    row, col = _iota2(n, n)
    same_head = (row >> 6) == (col >> 6)
    strict = same_head & (row > col)
    incl = same_head & (row >= col)
    eye = row == col
    mm = functools.partial(_dot, mode=P_CHUNK)

    def stack(x):
        return jnp.concatenate([jnp.where(lo, x, 0.0), jnp.where(lo, 0.0, x)], axis=0)

    probs = [(k, g) for k in range(nck) for g in range(pairs)]
    ps = range(len(probs))

    def blk(x, p):
        k, g = probs[p]
        return x[k * c:(k + 1) * c, g * LANE:(g + 1) * LANE]

    kk = [blk(kkp, p) * lax.rsqrt(_half_sum(blk(kkp, p) * blk(kkp, p), lo) + 1e-6) for p in ps]
    bt = [kk[p] * blk(a, p) for p in ps]
    rt_s = [stack(blk(rt, p)) for p in ps]
    kb_s = [stack(blk(kb, p)) for p in ps]
    bb_s = [stack(bt[p] * blk(e_neg, p)) for p in ps]
    kt_s = [stack(kk[p] * blk(e_prev, p)) for p in ps]
    kend_s = [stack(blk(kend, p)) for p in ps]
    bend_s = [stack(bt[p] * blk(e_end, p)) for p in ps]
    v_s = [stack(blk(zv, p)) for p in ps]

    p_all = [mm(jnp.concatenate([kt_s[p], rt_s[p]], axis=0), jnp.concatenate([bb_s[p], kb_s[p]], axis=0), _NT)
             for p in ps]
    a_ub = [jnp.where(strict, x[0:n, 0:n], 0.0) for x in p_all]
    a_vk = [jnp.where(strict, x[0:n, n:2 * n], 0.0) for x in p_all]
    p_rb = [jnp.where(incl, x[n:2 * n, 0:n], 0.0) for x in p_all]
    p_rk = [jnp.where(incl, x[n:2 * n, n:2 * n], 0.0) for x in p_all]

    av = [mm(a_vk[p], v_s[p]) for p in ps]
    pv_ = [mm(p_rk[p], v_s[p]) for p in ps]
    t_inv = _inv_unit_lower(a_ub, n, levels=2)
    m12 = [mm(t_inv[p], jnp.concatenate([kt_s[p], av[p]], axis=1)) for p in ps]
    w2m = [mm(p_rb[p], m12[p]) for p in ps]
    q1 = [rt_s[p] - w2m[p][:, 0:LANE] for p in ps]
    q2 = [pv_[p] - w2m[p][:, LANE:2 * LANE] for p in ps]
    phi = [jnp.where(eye, jnp.exp(blk(g_end, p)[0:1]), 0.0) - mm(bend_s[p], m12[p][:, 0:LANE], _TN)
           for p in ps]
    psi_t = [mm(jnp.concatenate([v_s[p], m12[p][:, LANE:2 * LANE]], axis=0),
                jnp.concatenate([kend_s[p], -bend_s[p]], axis=0), _TN) for p in ps]

    s_cur = [sb[g] for g in range(pairs)]
    y_s = [None] * len(probs)
    for k in range(nck):
        idx = [k * pairs + g for g in range(pairs)]
        for g, p in enumerate(idx):
            y_s[p] = _dot(q1[p], s_cur[g], _NT, P_STATE) + q2[p]
        s_cur = [_dot(s_cur[g], phi[p], _NT, P_STATE) + psi_t[p] for g, p in enumerate(idx)]
    for g in range(pairs):
        sb[g] = s_cur[g]

    out_rows = []
    for k in range(nck):
        outs = []
        for g in range(pairs):
            p = k * pairs + g
            y2 = y_s[p][0:c, :] + y_s[p][c:n, :]
            mu = _half_sum(y2, lo) * (1.0 / HEAD_A)
            dlt = y2 - mu
            var = _half_sum(dlt * dlt, lo) * (1.0 / HEAD_A)
            yn = dlt * lax.rsqrt(var + GN_EPS_A) * ln_g[:, g * LANE:(g + 1) * LANE] + ln_b[:, g * LANE:(g + 1) * LANE]
            bonus = _half_sum(blk(rk_prod, p), lo) * blk(zv, p)
            outs.append((yn + bonus) * blk(gate, p))
        out_rows.append(jnp.concatenate(outs, axis=1))
    y_ref[...] = jnp.concatenate(out_rows, axis=0).astype(y_ref.dtype)

    @pl.when(ti == n_chunks - 1)
    def _():
        for g in range(pairs):
            s_fin = sb[g]
            sout_ref[2 * g] = s_fin[0:HEAD_A, 0:HEAD_A]
            sout_ref[2 * g + 1] = s_fin[HEAD_A:2 * HEAD_A, HEAD_A:2 * HEAD_A]


def _rwkv_prompt(u, nb, t, pvec, mixl, w2p, a2p, sh_init, s0, l, pairs=8, nck=2):
    c = nck * RWKV_CHUNK
    nt = t // c
    gw = pairs * LANE
    groups = (H_A // 2) // pairs

    def uspec(off):
        return pl.BlockSpec((c, gw), lambda b, p, i: (b * nt + i, off // gw + p))

    def lspec(off):
        return pl.BlockSpec((c, LANE), lambda b, p, i: (b * nt + i, off // LANE))

    def shspec(off):
        return pl.BlockSpec((None, SUBLANE, gw), lambda b, p, i: (b, 0, off // gw + p))

    def shlspec(off):
        return pl.BlockSpec((None, SUBLANE, LANE), lambda b, p, i: (b, 0, off // LANE))

    group_cols = lambda rows: pl.BlockSpec((None, rows, gw), lambda b, p, i: (l, 0, p))
    return pl.pallas_call(
        functools.partial(_rwkv_prompt_kernel, n_chunks=nt, pairs=pairs, nck=nck),
        out_shape=(jax.ShapeDtypeStruct((nb * t, MIX_W), BF16),
                   jax.ShapeDtypeStruct((nb, H_A, HEAD_A, HEAD_A), F32)),
        grid=(nb, groups, nt),
        in_specs=[uspec(OFF_R), uspec(OFF_K), uspec(OFF_V), uspec(OFF_AG), lspec(OFF_WD), lspec(OFF_AD),
                  group_cols(16),
                  pl.BlockSpec((None, SUBLANE, LANE), lambda b, p, i: (l, 0, 0)),
                  group_cols(LANE), group_cols(LANE),
                  shspec(0), shspec(MIX_W), shspec(2 * MIX_W), shlspec(3 * MIX_W), shlspec(3 * MIX_W + LANE),
                  pl.BlockSpec((None, 2 * pairs, HEAD_A, HEAD_A), lambda b, p, i: (b, p, 0, 0))],
        out_specs=(pl.BlockSpec((c, gw), lambda b, p, i: (b * nt + i, p)),
                   pl.BlockSpec((None, 2 * pairs, HEAD_A, HEAD_A), lambda b, p, i: (b, p, 0, 0))),
        scratch_shapes=[pltpu.VMEM((c + SUBLANE, 3 * gw + 2 * LANE), F32),
                        pltpu.VMEM((pairs, 2 * HEAD_A, 2 * HEAD_A), F32)],
        compiler_params=_cparams(("parallel", "parallel", "arbitrary")),
        name="rwkv_prompt",
    )(u, u, u, u, u, u, pvec, mixl, w2p, a2p, sh_init, sh_init, sh_init, sh_init, sh_init, s0)


TB_R, TB_W, TB_KKP, TB_A, TB_KM, TB_V, TB_GATE = range(7)
PT_RK, PT_LNG, PT_LNB = range(3)


def _rwkv_step_kernel(rkv_ref, g_ref, wd_ref, ad_ref, prev_ref, mix_ref, pv_ref, w2_ref, a2_ref, pt_ref,
                      s_ref, acc_ref, y_ref, sout_ref, tab, ybuf):
    del acc_ref
    h = pl.program_id(0)

    @pl.when(h == 0)
    def _():
        cur = jnp.concatenate([rkv_ref[...], wd_ref[...], ad_ref[...]], axis=1)
        z = cur + (prev_ref[...] - cur) * mix_ref[...]
        zr, zk, zv = z[:, 0:MIX_W], z[:, MIX_W:2 * MIX_W], z[:, 2 * MIX_W:3 * MIX_W]
        zwd, zad = z[:, 3 * MIX_W:3 * MIX_W + LANE], z[:, 3 * MIX_W + LANE:]
        pv = pv_ref[...]
        logw = -_softplus(-(pv[PV_W0:PV_W0 + 1] + _dot(jnp.tanh(zwd), w2_ref[...], mode=P_STEP))) - 0.5
        a = _sigmoid(pv[PV_A0:PV_A0 + 1] + _dot(zad, a2_ref[...], mode=P_STEP))
        tab[TB_R] = zr.T
        tab[TB_W] = jnp.exp(-jnp.exp(logw)).T
        tab[TB_KKP] = (zk * pv[PV_KK:PV_KK + 1]).T
        tab[TB_A] = a.T
        tab[TB_KM] = (zk * (1.0 + (a - 1.0) * pv[PV_KA:PV_KA + 1])).T
        tab[TB_V] = zv.T
        tab[TB_GATE] = _silu(g_ref[...]).T

    rows = pl.ds(pl.multiple_of(h * HEAD_A, HEAD_A), HEAD_A)
    r_t, w_t, km_t = tab[TB_R, rows, :], tab[TB_W, rows, :], tab[TB_KM, rows, :]
    kkp_t = tab[TB_KKP, rows, :]
    kk_t = kkp_t * lax.rsqrt(jnp.sum(kkp_t * kkp_t, axis=0, keepdims=True) + 1e-6)
    bt_t = kk_t * tab[TB_A, rows, :]
    v_t = tab[TB_V, rows, :]
    for v in range(HEAD_A):
        s = s_ref[v]
        sa = jnp.sum(s * kk_t, axis=0, keepdims=True)
        s_new = s * w_t - sa * bt_t + v_t[v:v + 1, :] * km_t
        sout_ref[v] = s_new
        ybuf[v:v + 1, :] = jnp.sum(s_new * r_t, axis=0, keepdims=True)
    y = ybuf[...]
    mu = jnp.mean(y, axis=0, keepdims=True)
    dlt = y - mu
    var = jnp.mean(dlt * dlt, axis=0, keepdims=True)
    yn = dlt * lax.rsqrt(var + GN_EPS_A) * pt_ref[PT_LNG] + pt_ref[PT_LNB]
    bonus = jnp.sum(r_t * km_t * pt_ref[PT_RK], axis=0, keepdims=True) * v_t
    y_ref[...] = (yn + bonus) * tab[TB_GATE, rows, :]


def _rwkv_step(u, prev_pad, mix_pad, pvec, w2p, a2p, ptab, s_all, s_acc, l):
    nb = u.shape[0]
    lfull = lambda shape: pl.BlockSpec((None,) + shape, lambda h: (l,) + tuple(0 for _ in shape))
    st_spec = pl.BlockSpec((None, None, HEAD_A, HEAD_A, nb), lambda h: (l, h, 0, 0, 0))
    return pl.pallas_call(
        _rwkv_step_kernel,
        out_shape=(jax.ShapeDtypeStruct((MIX_W, nb), F32),
                   jax.ShapeDtypeStruct(s_acc.shape, F32)),
        grid=(H_A,),
        in_specs=[pl.BlockSpec((nb, 3 * MIX_W), lambda h: (0, 0)),
                  pl.BlockSpec((nb, MIX_W), lambda h: (0, OFF_AG // MIX_W)),
                  pl.BlockSpec((nb, LANE), lambda h: (0, OFF_WD // LANE)),
                  pl.BlockSpec((nb, LANE), lambda h: (0, OFF_AD // LANE)),
                  lfull((nb, SHIFT_PAD)),
                  lfull((1, SHIFT_PAD)), lfull((16, MIX_W)), lfull((LANE, MIX_W)), lfull((LANE, MIX_W)),
                  pl.BlockSpec((None, 3, HEAD_A, LANE), lambda h: (l, 0, h, 0)),
                  st_spec,
                  pl.BlockSpec(memory_space=pl.ANY)],
        out_specs=(pl.BlockSpec((HEAD_A, nb), lambda h: (h, 0)), st_spec),
        scratch_shapes=[pltpu.VMEM((7, MIX_W, nb), F32), pltpu.VMEM((HEAD_A, nb), F32)],
        input_output_aliases={11: 1},
        compiler_params=_cparams(("arbitrary",)),
        name="rwkv_step",
    )(u, u, u, u, prev_pad, mix_pad, pvec, w2p, a2p, ptab, s_all, s_acc)


LP_CONVB, LP_BA, LP_BX, LP_LAM = range(4)


def _lru_gates(xc, wa_ref, wx_ref, lp):
    r = _sigmoid(_mm_bf16(xc, wa_ref[...]) + lp[LP_BA:LP_BA + 1])
    i = _sigmoid(_mm_bf16(xc, wx_ref[...]) + lp[LP_BX:LP_BX + 1])
    log_a = -LRU_C * r * _softplus(-lp[LP_LAM:LP_LAM + 1])
    a = jnp.exp(log_a)
    b = jnp.sqrt(jnp.tanh(-log_a) * (a * a + 1.0)) * (i * xc)
    return a, b


def _lru_prompt_kernel(x_ref, g_ref, cw_ref, lp_ref, wa_ref, wx_ref, cinit_ref, h0_ref,
                       y_ref, hout_ref, xbuf, hcar, *, n_blocks, tc):
    ti = pl.program_id(1)

    @pl.when(ti == 0)
    def _():
        xbuf[0:SUBLANE, :] = cinit_ref[...]
        hcar[...] = h0_ref[...]

    x = x_ref[...]
    xbuf[SUBLANE:SUBLANE + tc, :] = x
    cw = cw_ref[...]
    lp = lp_ref[...]
    xc = (xbuf[pl.ds(SUBLANE - 3, tc), :] * cw[0:1] + xbuf[pl.ds(SUBLANE - 2, tc), :] * cw[1:2]
          + xbuf[pl.ds(SUBLANE - 1, tc), :] * cw[2:3] + x * cw[3:4] + lp[LP_CONVB:LP_CONVB + 1])
    xbuf[0:SUBLANE, :] = x[tc - SUBLANE:tc, :]
    a, b = _lru_gates(xc, wa_ref, wx_ref, lp)
    rows = lax.broadcasted_iota(jnp.int32, (tc, MIX_W), 0)
    s = 1
    while s < tc:
        a_sh = pltpu.roll(a, s, 0)
        b_sh = pltpu.roll(b, s, 0)
        m = rows >= s
        b = jnp.where(m, a * b_sh + b, b)
        a = jnp.where(m, a * a_sh, a)
        s *= 2
    h = b + a * hcar[...]
    hcar[...] = h[tc - 1:tc, :]
    y_ref[...] = (h * _silu(g_ref[...])).astype(y_ref.dtype)

    @pl.when(ti == n_blocks - 1)
    def _():
        hout_ref[...] = h[tc - 1:tc, :]


def _lru_prompt(u, nb, t, conv_w, lp, wa_d, wx_d, cinit, h0, l, tc=256):
    nt = t // tc
    const = lambda shape: pl.BlockSpec((None,) + shape, lambda b, i: (l,) + tuple(0 for _ in shape))
    return pl.pallas_call(
        functools.partial(_lru_prompt_kernel, n_blocks=nt, tc=tc),
        out_shape=(jax.ShapeDtypeStruct((nb * t, MIX_W), BF16),
                   jax.ShapeDtypeStruct((nb, 1, MIX_W), F32)),
        grid=(nb, nt),
        in_specs=[pl.BlockSpec((tc, MIX_W), lambda b, i: (b * nt + i, OFF_BX // MIX_W)),
                  pl.BlockSpec((tc, MIX_W), lambda b, i: (b * nt + i, OFF_BG // MIX_W)),
                  const((CONV_W, MIX_W)), const((SUBLANE, MIX_W)),
                  const((MIX_W, MIX_W)), const((MIX_W, MIX_W)),
                  pl.BlockSpec((None, SUBLANE, MIX_W), lambda b, i: (b, 0, 0)),
                  pl.BlockSpec((None, 1, MIX_W), lambda b, i: (b, 0, 0))],
        out_specs=(pl.BlockSpec((tc, MIX_W), lambda b, i: (b * nt + i, 0)),
                   pl.BlockSpec((None, 1, MIX_W), lambda b, i: (b, 0, 0))),
        scratch_shapes=[pltpu.VMEM((tc + SUBLANE, MIX_W), F32), pltpu.VMEM((1, MIX_W), F32)],
        compiler_params=_cparams(("parallel", "arbitrary")),
        name="lru_prompt",
    )(u, u, conv_w, lp, wa_d, wx_d, cinit, h0)


def _lru_step_kernel(x_ref, g_ref, buf_ref, h0_ref, cw_ref, lp_ref, wa_ref, wx_ref, y_ref, hout_ref):
    x = x_ref[...]
    cw = cw_ref[...]
    lp = lp_ref[...]
    xc = buf_ref[0] * cw[0:1] + buf_ref[1] * cw[1:2] + buf_ref[2] * cw[2:3] + x * cw[3:4] + lp[LP_CONVB:LP_CONVB + 1]
    a, b = _lru_gates(xc, wa_ref, wx_ref, lp)
    h = a * h0_ref[...] + b
    hout_ref[...] = h
    y_ref[...] = h * _silu(g_ref[...])


def _lru_step(u, buf_t, h0, conv_w, lp, wa_d, wx_d, l):
    nb = u.shape[0]
    const = lambda shape: pl.BlockSpec((None,) + shape, lambda i: (l,) + tuple(0 for _ in shape))
    out = pl.BlockSpec((nb, MIX_W), lambda i: (0, 0))
    return pl.pallas_call(
        _lru_step_kernel,
        out_shape=(jax.ShapeDtypeStruct((nb, MIX_W), F32), jax.ShapeDtypeStruct((nb, MIX_W), F32)),
        grid=(1,),
        in_specs=[pl.BlockSpec((nb, MIX_W), lambda i: (0, OFF_BX // MIX_W)),
                  pl.BlockSpec((nb, MIX_W), lambda i: (0, OFF_BG // MIX_W)),
                  const((CONV_W - 1, nb, MIX_W)), const((nb, MIX_W)),
                  const((CONV_W, MIX_W)), const((SUBLANE, MIX_W)),
                  const((MIX_W, MIX_W)), const((MIX_W, MIX_W))],
        out_specs=(out, out),
        compiler_params=_cparams(("arbitrary",)),
        name="lru_step",
    )(u, u, buf_t, h0, conv_w, lp, wa_d, wx_d)


GP_ALOG, GP_DTB, GP_NORM = range(3)


def _gdn_prompt_kernel(q_ref, k_ref, v_ref, z_ref, ab_ref, cwq_ref, cwk_ref, cwv_ref, gp_ref,
                       ciq_ref, cik_ref, civ_ref, s0_ref, y_ref, sout_ref, xbuf, st, *, n_chunks, heads, nck):
    c = GDN_CHUNK
    tc = nck * c
    hw = heads * LANE
    h_base = pl.program_id(1) * heads
    ti = pl.program_id(2)

    @pl.when(ti == 0)
    def _():
        xbuf[0:SUBLANE, :] = jnp.concatenate([ciq_ref[...], cik_ref[...], civ_ref[...]], axis=1)
        st[...] = s0_ref[...]

    cur = jnp.concatenate([q_ref[...], k_ref[...], v_ref[...]], axis=1)
    xbuf[SUBLANE:SUBLANE + tc, :] = cur
    cw = jnp.concatenate([cwq_ref[...], cwk_ref[...], cwv_ref[...]], axis=1)
    conv = (xbuf[pl.ds(SUBLANE - 3, tc), :] * cw[0:1] + xbuf[pl.ds(SUBLANE - 2, tc), :] * cw[1:2]
            + xbuf[pl.ds(SUBLANE - 1, tc), :] * cw[2:3] + cur * cw[3:4])
    xbuf[0:SUBLANE, :] = cur[tc - SUBLANE:tc, :]
    act = _silu(conv)
    zg = _silu(z_ref[...])

    gp = gp_ref[...]
    lane_c = lax.broadcasted_iota(jnp.int32, (c, LANE), 1)
    ab = ab_ref[...]
    g_all = -jnp.exp(gp[GP_ALOG:GP_ALOG + 1]) * _softplus(ab + gp[GP_DTB:GP_DTB + 1])
    beta_all = _sigmoid(ab)

    row, col = _iota2(c, c)
    incl = row >= col
    strict = row > col
    eye = row == col
    mm = functools.partial(_dot, mode=P_CHUNK)

    probs = [(k, j) for k in range(nck) for j in range(heads)]
    hs_ = range(len(probs))

    def rows(x, k):
        return x[k * c:(k + 1) * c]

    qh = [rows(act, k)[:, j * DK:(j + 1) * DK] for k, j in probs]
    kh = [rows(act, k)[:, hw + j * DK:hw + (j + 1) * DK] for k, j in probs]
    vh = [rows(act, k)[:, 2 * hw + j * DV:2 * hw + (j + 1) * DV] for k, j in probs]
    qn = [x * lax.rsqrt(jnp.sum(x * x, axis=-1, keepdims=True) + 1e-6) * (DK ** -0.5) for x in qh]
    kn = [x * lax.rsqrt(jnp.sum(x * x, axis=-1, keepdims=True) + 1e-6) for x in kh]
    beta = [jnp.sum(jnp.where(lane_c == h_base + j + H_C, rows(beta_all, k), 0.0), axis=-1, keepdims=True)
            for k, j in probs]

    incl_01 = incl.astype(BF16)
    gc_all = [_dot_exact_lhs(incl_01, rows(g_all, k)) for k in range(nck)]
    gc = [jnp.broadcast_to(jnp.sum(jnp.where(lane_c == h_base + j, gc_all[k], 0.0), axis=-1, keepdims=True),
                           (c, LANE)) for k, j in probs]
    gc_row = [jnp.sum(jnp.where(eye, x, 0.0), axis=0, keepdims=True) for x in gc]
    decay = [jnp.where(incl, jnp.exp(jnp.where(incl, gc[j] - gc_row[j], 0.0)), 0.0) for j in hs_]
    g_last = [x[c - 1:c, :] for x in gc]
    eg = [jnp.exp(x) for x in gc]

    kb = [kn[j] * beta[j] for j in hs_]
    p_all = [mm(jnp.concatenate([kb[j], qn[j]], axis=0), kn[j], _NT) for j in hs_]
    a_mat = [jnp.where(strict, p_all[j][0:c] * decay[j], 0.0) for j in hs_]
    qk = [p_all[j][c:2 * c] * decay[j] for j in hs_]
    t_inv = _inv_unit_lower(a_mat, c, levels=3)
    wu = [mm(t_inv[j], jnp.concatenate([kb[j] * eg[j], vh[j] * beta[j]], axis=1)) for j in hs_]
    k_dec = [kn[j] * jnp.exp(g_last[j] - gc[j]) for j in hs_]
    w3 = [mm(k_dec[j], wu[j], _TN) for j in hs_]
    w4 = [mm(qk[j], wu[j]) for j in hs_]
    phi = [jnp.where(eye, jnp.exp(g_last[j]), 0.0) - w3[j][:, 0:DK] for j in hs_]
    q1 = [qn[j] * eg[j] - w4[j][:, 0:DK] for j in hs_]

    s_cur = [st[j] for j in range(heads)]
    o = [None] * len(probs)
    for k in range(nck):
        idx = [k * heads + j for j in range(heads)]
        for j, p in enumerate(idx):
            o[p] = _dot(q1[p], s_cur[j], _NN, P_STATE) + w4[p][:, DK:]
        s_cur = [_dot(phi[p], s_cur[j], _NN, P_STATE) + w3[p][:, DK:] for j, p in enumerate(idx)]
    for j in range(heads):
        st[j] = s_cur[j]

    norm_g = gp[GP_NORM:GP_NORM + 1]
    out_rows = []
    for k in range(nck):
        outs = [o[k * heads + j] * lax.rsqrt(jnp.mean(o[k * heads + j] * o[k * heads + j], axis=-1, keepdims=True)
                                              + NORM_EPS) * norm_g * rows(zg, k)[:, j * DV:(j + 1) * DV]
                for j in range(heads)]
        out_rows.append(jnp.concatenate(outs, axis=1))
    y_ref[...] = jnp.concatenate(out_rows, axis=0).astype(y_ref.dtype)

    @pl.when(ti == n_chunks - 1)
    def _():
        sout_ref[...] = st[...]


def _gdn_prompt(u, nb, t, conv_w, gpar, cinit, s0, l, heads=8, nck=2):
    assert GDN_CHUNK == LANE and DK == LANE and DV == LANE
    c = nck * GDN_CHUNK
    nt = t // c
    hw = heads * LANE

    def uspec(off):
        return pl.BlockSpec((c, hw), lambda b, h, i: (b * nt + i, off // hw + h))

    def cwspec(off):
        return pl.BlockSpec((None, CONV_W, hw), lambda b, h, i: (l, 0, off // hw + h))

    def cispec(off):
        return pl.BlockSpec((None, SUBLANE, hw), lambda b, h, i: (b, 0, off // hw + h))

    return pl.pallas_call(
        functools.partial(_gdn_prompt_kernel, n_chunks=nt, heads=heads, nck=nck),
        out_shape=(jax.ShapeDtypeStruct((nb * t, MIX_W), BF16),
                   jax.ShapeDtypeStruct((nb, H_C, DK, DV), F32)),
        grid=(nb, H_C // heads, nt),
        in_specs=[uspec(OFF_Q), uspec(OFF_Q + MIX_W), uspec(OFF_Q + 2 * MIX_W), uspec(OFF_Z),
                  pl.BlockSpec((c, LANE), lambda b, h, i: (b * nt + i, OFF_AB // LANE)),
                  cwspec(0), cwspec(MIX_W), cwspec(2 * MIX_W),
                  pl.BlockSpec((None, SUBLANE, LANE), lambda b, h, i: (l, 0, 0)),
                  cispec(0), cispec(MIX_W), cispec(2 * MIX_W),
                  pl.BlockSpec((None, heads, DK, DV), lambda b, h, i: (b, h, 0, 0))],
        out_specs=(pl.BlockSpec((c, hw), lambda b, h, i: (b * nt + i, h)),
                   pl.BlockSpec((None, heads, DK, DV), lambda b, h, i: (b, h, 0, 0))),
        scratch_shapes=[pltpu.VMEM((c + SUBLANE, 3 * hw), F32), pltpu.VMEM((heads, DK, DV), F32)],
        compiler_params=_cparams(("parallel", "parallel", "arbitrary")),
        name="gdn_prompt",
    )(u, u, u, u, u, conv_w, conv_w, conv_w, gpar, cinit, cinit, cinit, s0)


def _gdn_step_kernel(qkv_ref, z_ref, ab_ref, buf_ref, cw_ref, gp_ref, s_ref, acc_ref, y_ref, sout_ref, hs, *, bb):
    del acc_ref
    cw = cw_ref[...]
    cur = qkv_ref[...]
    act = _silu(buf_ref[0] * cw[0:1] + buf_ref[1] * cw[1:2] + buf_ref[2] * cw[2:3] + cur * cw[3:4])
    gp = gp_ref[...]
    ab = ab_ref[...]
    g_all = -jnp.exp(gp[GP_ALOG:GP_ALOG + 1]) * _softplus(ab + gp[GP_DTB:GP_DTB + 1])
    beta_all = _sigmoid(ab)
    zg = _silu(z_ref[...])
    hs_q, hs_k, hs_v, hs_eg, hs_beta, hs_zg, hs_qk = range(7)
    for h in range(H_C):
        qh = act[:, h * DK:(h + 1) * DK]
        kh = act[:, MIX_W + h * DK:MIX_W + (h + 1) * DK]
        qn = qh * lax.rsqrt(jnp.sum(qh * qh, axis=-1, keepdims=True) + 1e-6) * (DK ** -0.5)
        kn = kh * lax.rsqrt(jnp.sum(kh * kh, axis=-1, keepdims=True) + 1e-6)
        hs[hs_q, h] = qn
        hs[hs_k, h] = kn
        hs[hs_v, h] = act[:, 2 * MIX_W + h * DV:2 * MIX_W + (h + 1) * DV]
        hs[hs_eg, h] = jnp.broadcast_to(jnp.exp(g_all[:, h:h + 1]), (bb, LANE))
        hs[hs_beta, h] = jnp.broadcast_to(beta_all[:, H_C + h:H_C + h + 1], (bb, LANE))
        hs[hs_zg, h] = zg[:, h * DV:(h + 1) * DV]
        hs[hs_qk, h] = jnp.broadcast_to(jnp.sum(qn * kn, axis=-1, keepdims=True), (bb, LANE))

    row, col = _iota2(DK, DV)
    diag = row == col
    rowb = lax.broadcasted_iota(jnp.int32, (bb, DV), 0)
    norm_g = gp[GP_NORM:GP_NORM + 1]
    heads = range(H_C)

    def body(b, outs):
        def rows(q):
            return [hs[q, h, pl.ds(b, 1), :] for h in heads]

        q, k, v, eg, beta = rows(hs_q), rows(hs_k), rows(hs_v), rows(hs_eg), rows(hs_beta)
        s = [s_ref[b, h] for h in heads]
        kcol = [jnp.sum(jnp.where(diag, jnp.broadcast_to(x, (DK, DV)), 0.0), axis=-1, keepdims=True) for x in k]
        qcol = [jnp.sum(jnp.where(diag, jnp.broadcast_to(x, (DK, DV)), 0.0), axis=-1, keepdims=True) for x in q]
        ks = [jnp.sum(kcol[h] * s[h], axis=0, keepdims=True) for h in heads]
        qs = [jnp.sum(qcol[h] * s[h], axis=0, keepdims=True) for h in heads]
        v_new = [beta[h] * (v[h] - eg[h] * ks[h]) for h in heads]
        for h in heads:
            sout_ref[b, h] = s[h] * eg[h] + kcol[h] * v_new[h]
        qk = rows(hs_qk)
        o = [eg[h] * qs[h] + qk[h] * v_new[h] for h in heads]
        ms = [jnp.mean(x * x, axis=-1, keepdims=True) for x in o]
        zg_h = rows(hs_zg)
        res = [o[h] * lax.rsqrt(ms[h] + NORM_EPS) * norm_g * zg_h[h] for h in heads]
        return tuple(jnp.where(rowb == b, jnp.broadcast_to(res[h], (bb, DV)), outs[h]) for h in heads)

    outs = lax.fori_loop(0, bb, body, tuple(jnp.zeros((bb, DV), F32) for _ in heads))
    y_ref[...] = jnp.concatenate(outs, axis=1)


def _gdn_step(u, buf_t, conv_w, gpar, s_all, s_acc, l, bb=SUBLANE):
    nb = u.shape[0]
    lfull = lambda shape: pl.BlockSpec((None,) + shape, lambda i: (l,) + tuple(0 for _ in shape))
    st_spec = pl.BlockSpec((None, bb, H_C, DK, DV), lambda i: (l, i, 0, 0, 0))
    return pl.pallas_call(
        functools.partial(_gdn_step_kernel, bb=bb),
        out_shape=(jax.ShapeDtypeStruct((nb, MIX_W), F32),
                   jax.ShapeDtypeStruct(s_acc.shape, F32)),
        grid=(nb // bb,),
        in_specs=[pl.BlockSpec((bb, QKV_W), lambda i: (i, OFF_Q // QKV_W)),
                  pl.BlockSpec((bb, MIX_W), lambda i: (i, OFF_Z // MIX_W)),
                  pl.BlockSpec((bb, LANE), lambda i: (i, OFF_AB // LANE)),
                  pl.BlockSpec((None, CONV_W - 1, bb, QKV_W), lambda i: (l, 0, i, 0)),
                  lfull((CONV_W, QKV_W)), lfull((SUBLANE, LANE)),
                  st_spec,
                  pl.BlockSpec(memory_space=pl.ANY)],
        out_specs=(pl.BlockSpec((bb, MIX_W), lambda i: (i, 0)), st_spec),
        scratch_shapes=[pltpu.VMEM((7, H_C, bb, LANE), F32)],
        input_output_aliases={7: 1},
        compiler_params=_cparams(("parallel",)),
        name="gdn_step",
    )(u, u, u, buf_t, conv_w, gpar, s_all, s_acc)


def _pad_cols(w, width):
    pad = width - w.shape[-1]
    return jnp.pad(w, [(0, 0)] * (w.ndim - 1) + [(0, pad)])


def _layout_in_cols(w):
    a_sh_end = A_SHIFT_W
    o_ag = a_sh_end
    o_bx = o_ag + MIX_W
    o_bg = o_bx + MIX_W
    o_q = o_bg + MIX_W
    o_a = o_q + QKV_W
    o_z = o_a + 2 * H_C
    o_m = o_z + MIX_W
    parts = [w[..., 0:3 * MIX_W], w[..., o_ag:o_bx], w[..., o_bx:o_bg], w[..., o_bg:o_q], w[..., o_q:o_a],
             w[..., o_z:o_m], w[..., o_m:IN_COLS],
             _pad_cols(w[..., 3 * MIX_W:3 * MIX_W + LORA], LANE),
             _pad_cols(w[..., 3 * MIX_W + LORA:a_sh_end], LANE),
             _pad_cols(w[..., o_a:o_z], LANE)]
    used = sum(part.shape[-1] for part in parts)
    parts.append(jnp.zeros(w.shape[:-1] + (N_PAD - used,), w.dtype))
    return jnp.concatenate(parts, axis=-1)


def _shift_to_padded(s):
    return jnp.concatenate([s[..., 0:3 * MIX_W], _pad_cols(s[..., 3 * MIX_W:3 * MIX_W + LORA], LANE),
                            _pad_cols(s[..., 3 * MIX_W + LORA:], LANE)], axis=-1)


def _shift_from_u(u_rows):
    return jnp.concatenate([u_rows[..., 0:3 * MIX_W], u_rows[..., OFF_WD:OFF_WD + LORA],
                            u_rows[..., OFF_AD:OFF_AD + LORA]], axis=-1)


def _block_diag(w):
    eye = jnp.eye(LRU_BLOCKS, dtype=w.dtype)
    return jnp.einsum('lnij,nm->lnimj', w, eye).reshape(w.shape[0], MIX_W, MIX_W)


def _rows_table(rows, n_rows):
    tab = jnp.stack(rows, axis=1)
    return jnp.pad(tab, ((0, 0), (0, n_rows - len(rows)), (0, 0)))


def _prep_params(p):
    depth = p['rwkv_mix'].shape[0]
    mix = p['rwkv_mix']
    pvec = _rows_table([mix[:, 0:MIX_W], mix[:, MIX_W:2 * MIX_W], mix[:, 2 * MIX_W:3 * MIX_W],
                        p['rwkv_w0'], p['rwkv_a0'], p['rwkv_kk'], p['rwkv_ka'], p['rwkv_rk'].reshape(depth, MIX_W),
                        p['rwkv_lnx_g'], p['rwkv_lnx_b']], 16)
    mix_pad = _shift_to_padded(mix)
    mixl = _rows_table([mix_pad[:, 3 * MIX_W:3 * MIX_W + LANE], mix_pad[:, 3 * MIX_W + LANE:]], SUBLANE)
    lora_pad = ((0, 0), (0, LANE - LORA), (0, 0))
    lp = _rows_table([p['lru_conv_b'], p['lru_ba'], p['lru_bx'], p['lru_lambda']], SUBLANE)
    gpar = _rows_table([_pad_cols(p['gdn_a_log'], LANE), _pad_cols(p['gdn_dt_bias'], LANE), p['gdn_norm_g']], SUBLANE)
    ptab = jnp.stack([p['rwkv_rk'].reshape(depth, MIX_W), p['rwkv_lnx_g'], p['rwkv_lnx_b']], axis=1)
    ptab = jnp.broadcast_to(ptab[..., None], (depth, 3, MIX_W, LANE))
    return dict(pvec=pvec, mix_pad=mix_pad[:, None], mixl=mixl, ptab=ptab,
                w2p=jnp.pad(p['rwkv_w2'], lora_pad), a2p=jnp.pad(p['rwkv_a2'], lora_pad),
                lru_cw=p['lru_conv_w'], lp=lp,
                wa_d=_block_diag(p['lru_wa']).astype(BF16), wx_d=_block_diag(p['lru_wx']).astype(BF16),
                gdn_cw=p['gdn_conv_w'], gpar=gpar)


def _prompt_layer(x2d, nb, t, l, pp, zeros, mod_p, norm_g3, w_in_p, w_branch_b, w_out_b):
    u = _inproj(x2d, mod_p, norm_g3, w_in_p, l, rows_per_mod=t, tm=PROMPT_TM, tn=INPROJ_TN)
    ya, wkv = _rwkv_prompt(u, nb, t, pp['pvec'], pp['mixl'], pp['w2p'], pp['a2p'], zeros['shift'], zeros['wkv'], l)
    yb, h_last = _lru_prompt(u, nb, t, pp['lru_cw'], pp['lp'], pp['wa_d'], pp['wx_d'],
                             zeros['lru_conv'], zeros['lru_h'], l)
    yc, ssm = _gdn_prompt(u, nb, t, pp['gdn_cw'], pp['gpar'], zeros['gdn_conv'], zeros['ssm'], l)
    merged = _merge(ya, yb, yc, u, w_branch_b, l, tm=PROMPT_TM, tn=MERGE_TN)
    x_new = _outproj(merged, w_out_b, x2d, mod_p, l, rows_per_mod=t, tm=PROMPT_TM, tn=OUTPROJ_TN)
    u3 = u.reshape(nb, t, N_PAD)
    states = (_shift_from_u(u3[:, t - 1]), wkv,
              u3[:, t - (CONV_W - 1):, OFF_BX:OFF_BX + MIX_W], h_last.reshape(nb, MIX_W),
              u3[:, t - (CONV_W - 1):, OFF_Q:OFF_Q + QKV_W], ssm)
    return x_new, states


def _sample_layer(x2d, l, pp, st, acc, mod_s, norm_g3, w_in_p, w_branch_b, w_out_b):
    nb = x2d.shape[0]
    u = _inproj(x2d, mod_s, norm_g3, w_in_p, l, rows_per_mod=1, tm=nb, tn=INPROJ_TN)
    ya_t, wkv_acc = _rwkv_step(u, st['shift'], pp['mix_pad'], pp['pvec'], pp['w2p'], pp['a2p'], pp['ptab'],
                               st['wkv_t'], acc[0], l)
    ya = ya_t.T
    yb, h_new = _lru_step(u, st['lru_conv_t'], st['lru_h'], pp['lru_cw'], pp['lp'], pp['wa_d'], pp['wx_d'], l)
    yc, ssm_acc = _gdn_step(u, st['gdn_conv_t'], pp['gdn_cw'], pp['gpar'], st['ssm'], acc[1], l)
    merged = _merge(ya, yb, yc, u, w_branch_b, l, tm=nb, tn=OUTPROJ_TN)
    x_new = _outproj(merged, w_out_b, x2d, mod_s, l, rows_per_mod=1, tm=nb, tn=OUTPROJ_TN)
    rows = (_shift_from_u(u), u[:, OFF_BX:OFF_BX + MIX_W], h_new, u[:, OFF_Q:OFF_Q + QKV_W])
    return x_new, rows, (wkv_acc, ssm_acc)


def kernel(x_prompt, x_sample, state_rwkv_shift, state_rwkv_wkv, state_lru_conv, state_lru_h, state_gdn_conv, state_gdn_ssm, c_prompt, c_sample, ada_w, ada_b, norm_g, w_in, rwkv_mix, rwkv_w0, rwkv_w2, rwkv_a0, rwkv_a2, rwkv_kk, rwkv_ka, rwkv_rk, rwkv_lnx_g, rwkv_lnx_b, lru_conv_w, lru_conv_b, lru_wa, lru_ba, lru_wx, lru_bx, lru_lambda, gdn_conv_w, gdn_a_log, gdn_dt_bias, gdn_norm_g, w_branch, w_out, final_g):
    p = dict(rwkv_mix=rwkv_mix, rwkv_w0=rwkv_w0, rwkv_w2=rwkv_w2, rwkv_a0=rwkv_a0, rwkv_a2=rwkv_a2,
             rwkv_kk=rwkv_kk, rwkv_ka=rwkv_ka, rwkv_rk=rwkv_rk, rwkv_lnx_g=rwkv_lnx_g, rwkv_lnx_b=rwkv_lnx_b,
             lru_conv_w=lru_conv_w, lru_conv_b=lru_conv_b, lru_wa=lru_wa, lru_ba=lru_ba, lru_wx=lru_wx,
             lru_bx=lru_bx, lru_lambda=lru_lambda, gdn_conv_w=gdn_conv_w, gdn_a_log=gdn_a_log,
             gdn_dt_bias=gdn_dt_bias, gdn_norm_g=gdn_norm_g)
    depth = w_in.shape[0]
    bp, t, _ = x_prompt.shape
    bs = x_sample.shape[0]

    w_in_p = _layout_in_cols(w_in).astype(BF16)
    w_branch_b = w_branch.astype(BF16)
    w_out_b = w_out.astype(BF16)
    norm_g3 = norm_g.reshape(depth, 1, D_MODEL)

    rows = bp + bs
    rows_pad = -(-rows // SUBLANE) * SUBLANE
    c_all = jnp.pad(jnp.concatenate([c_prompt, c_sample], axis=0), ((0, rows_pad - rows), (0, 0)))
    mod = _ada_all(c_all, ada_w, ada_b)
    mod_p = mod[:, 0:bp].reshape(depth, bp, 1, 3 * D_MODEL)
    mod_s = mod[:, bp:bp + bs]

    pp = _prep_params(p)
    zeros = dict(shift=jnp.zeros((bp, SUBLANE, SHIFT_PAD), F32), wkv=jnp.zeros((bp, H_A, HEAD_A, HEAD_A), F32),
                 lru_conv=jnp.zeros((bp, SUBLANE, MIX_W), F32), lru_h=jnp.zeros((bp, 1, MIX_W), F32),
                 gdn_conv=jnp.zeros((bp, SUBLANE, QKV_W), F32), ssm=jnp.zeros((bp, H_C, DK, DV), F32))
    wkv_t = jnp.transpose(state_rwkv_wkv, (0, 2, 3, 4, 1))
    st = dict(shift=_shift_to_padded(state_rwkv_shift), wkv_t=wkv_t,
              lru_conv_t=jnp.swapaxes(state_lru_conv, 1, 2), lru_h=state_lru_h,
              gdn_conv_t=jnp.swapaxes(state_gdn_conv, 1, 2), ssm=state_gdn_ssm)
    acc = (jnp.zeros(wkv_t.shape, F32), jnp.zeros(state_gdn_ssm.shape, F32))

    xp = x_prompt.reshape(bp * t, D_MODEL)
    xs = x_sample.reshape(bs, D_MODEL)
    p_states, s_rows = [], []
    for l in range(depth):
        xp, pst = _prompt_layer(xp, bp, t, l, pp, zeros, mod_p, norm_g3, w_in_p, w_branch_b, w_out_b)
        p_states.append(pst)
        xs, rows_l, acc = _sample_layer(xs, l, pp, st, acc, mod_s, norm_g3, w_in_p, w_branch_b, w_out_b)
        s_rows.append(rows_l)

    y_prompt = _final_norm(xp, final_g, tm=512).reshape(bp, t, D_MODEL)
    y_sample = _final_norm(xs, final_g, tm=bs).reshape(bs, 1, D_MODEL)
    p_out = tuple(jnp.stack([s[i] for s in p_states]) for i in range(6))
    s_shift, s_bx, s_h, s_qkv = (jnp.stack([r[i] for r in s_rows]) for i in range(4))
    s_lru_conv = jnp.concatenate([state_lru_conv[:, :, 1:], s_bx[:, :, None]], axis=2)
    s_gdn_conv = jnp.concatenate([state_gdn_conv[:, :, 1:], s_qkv[:, :, None]], axis=2)
    s_wkv = jnp.transpose(acc[0], (0, 4, 1, 2, 3))
    return (y_prompt, y_sample) + p_out + (s_shift, s_wkv, s_lru_conv, s_h, s_gdn_conv, acc[1])
```

```python
import functools
import math

import jax
import jax.numpy as jnp
import numpy as np
from jax import lax
from jax.experimental import pallas as pl
from jax.experimental.pallas import tpu as pltpu

F32 = jnp.float32
BF16 = jnp.bfloat16
HIGHEST = lax.Precision.HIGHEST

D_MODEL = 2048
DEPTH = 4
MIX_W = 1024
HEAD_A = 64
H_A = 16
LORA = 96
GN_EPS_A = 64e-5
A_SHIFT_W = 3 * MIX_W + 2 * LORA
LRU_BLOCKS = 16
LRU_BS = 64
LRU_C = 8.0
CONV_W = 4
DK = 128
DV = 128
H_C = 8
QKV_W = 3 * MIX_W
NORM_EPS = 1e-6
IN_COLS = 16592

LANE = 128
SUBLANE = 8

OFF_R, OFF_K, OFF_V = 0, 1024, 2048
OFF_AG = 3072
OFF_BX, OFF_BG = 4096, 5120
OFF_Q = 6144
OFF_Z = 9216
OFF_MERGE = 10240
OFF_WD, OFF_AD, OFF_AB = 16384, 16512, 16640
N_PAD = 16896
SHIFT_PAD = 3 * MIX_W + 2 * LANE

RWKV_CHUNK = 64
GDN_CHUNK = 128
RWKV_LEAD = 4
GDN_LEAD = 4
VMEM_LIMIT = 56 * 1024 * 1024

PROMPT_TM = 1024
INPROJ_TN = 1536
MERGE_TN = 512
OUTPROJ_TN = 1024


def _cparams(sem):
    return pltpu.CompilerParams(dimension_semantics=sem, vmem_limit_bytes=VMEM_LIMIT)


def _sigmoid(x):
    return 1.0 / (1.0 + jnp.exp(-x))


def _silu(x):
    return x * _sigmoid(x)


def _softplus(x):
    return jnp.maximum(x, 0.0) + jnp.log1p(jnp.exp(-jnp.abs(x)))


_NN = (((1,), (0,)), ((), ()))
_NT = (((1,), (1,)), ((), ()))
_TN = (((0,), (0,)), ((), ()))

P_LORA = "bf16"
P_CHUNK = "bf16"
P_INV = "bf16"
P_STATE = "bf16"
P_STEP = "f32"


def _dg(a, b, dims):
    return lax.dot_general(a, b, dims, preferred_element_type=F32)


def _dot(a, b, dims=_NN, mode="bf16"):
    if mode == "f32":
        return lax.dot_general(a, b, dims, preferred_element_type=F32, precision=HIGHEST)
    ah, bh = a.astype(BF16), b.astype(BF16)
    if mode == "bf16":
        return _dg(ah, bh, dims)
    al = (a - ah.astype(F32)).astype(BF16)
    bl = (b - bh.astype(F32)).astype(BF16)
    return _dg(ah, bh, dims) + (_dg(al, bh, dims) + _dg(ah, bl, dims))


def _dot_exact_lhs(a01, x):
    a = a01
    x1 = x.astype(BF16)
    r1 = x - x1.astype(F32)
    x2 = r1.astype(BF16)
    x3 = (r1 - x2.astype(F32)).astype(BF16)
    return _dg(a, x1, _NN) + (_dg(a, x2, _NN) + _dg(a, x3, _NN))


def _mm_bf16(a, b):
    return jnp.dot(a.astype(BF16), b.astype(BF16), preferred_element_type=F32)


def _iota2(n, m):
    return (lax.broadcasted_iota(jnp.int32, (n, m), 0), lax.broadcasted_iota(jnp.int32, (n, m), 1))


def _inv_unit_lower(mats, n, levels):
    mm = functools.partial(_dot, mode=P_INV)
    row, col = _iota2(n, n)
    eye = (row == col).astype(F32)
    same = (row >> 4) == (col >> 4)
    d = [jnp.where(same, a, 0.0) for a in mats]
    o = [a - di for a, di in zip(mats, d)]
    x1 = [-di for di in d]
    x2 = [mm(x, x) for x in x1]
    yield
    td = [mm(eye + a, eye + b) for a, b in zip(x1, x2)]
    x4 = [mm(x, x) for x in x2]
    yield
    td = [mm(t, eye + x) for t, x in zip(td, x4)]
    x8 = [mm(x, x) for x in x4]
    yield
    td = [mm(t, eye + x) for t, x in zip(td, x8)]
    yield
    p = [-mm(t, oi) for t, oi in zip(td, o)]
    t = [eye + pi for pi in p]
    for _ in range(levels - 1):
        yield
        p = [mm(pi, pi) for pi in p]
        yield
        t = [mm(ti, eye + pi) for ti, pi in zip(t, p)]
    yield
    return [mm(ti, tdi) for ti, tdi in zip(t, td)]


def _interleave(gens, lead):
    done = [False] * len(gens)
    step = 0
    while not all(done):
        for i, g in enumerate(gens):
            if not done[i] and step >= i * lead:
                try:
                    next(g)
                except StopIteration:
                    done[i] = True
        step += 1


def _ada_kernel(c_ref, w_ref, b_ref, o_ref):
    h = _silu(c_ref[...]).astype(BF16)
    o_ref[...] = jnp.dot(h, w_ref[...].astype(BF16), preferred_element_type=F32) + b_ref[...]


def _ada_all(c_all, ada_w, ada_b, tn=512):
    rows = c_all.shape[0]
    depth = ada_w.shape[0]
    n = ada_w.shape[2]
    return pl.pallas_call(
        _ada_kernel,
        out_shape=jax.ShapeDtypeStruct((depth, rows, n), F32),
        grid=(depth, n // tn),
        in_specs=[pl.BlockSpec((rows, D_MODEL), lambda l, j: (0, 0)),
                  pl.BlockSpec((None, D_MODEL, tn), lambda l, j: (l, 0, j)),
                  pl.BlockSpec((None, 1, tn), lambda l, j: (l, 0, j))],
        out_specs=pl.BlockSpec((None, rows, tn), lambda l, j: (l, 0, j)),
        compiler_params=_cparams(("parallel", "parallel")),
        name="ada_mod",
    )(c_all, ada_w, ada_b.reshape(depth, 1, n))


def _inproj_kernel(x_ref, shift_ref, scale_ref, g_ref, w_ref, o_ref, h_ref):
    @pl.when(pl.program_id(1) == 0)
    def _():
        x = x_ref[...]
        y = x * lax.rsqrt(jnp.mean(x * x, axis=-1, keepdims=True) + NORM_EPS) * g_ref[...]
        h_ref[...] = (y * (1.0 + scale_ref[...]) + shift_ref[...]).astype(BF16)

    o_ref[...] = _dg(h_ref[...], w_ref[...], _NT)


def _inproj(x2d, mod, norm_g3, w_in_p, l, rows_per_mod, tm, tn):
    m = x2d.shape[0]
    n = w_in_p.shape[1]
    if rows_per_mod > 1:
        def mspec(part):
            return pl.BlockSpec((None, None, 1, D_MODEL), lambda i, j: (l, (i * tm) // rows_per_mod, 0, part))
    else:
        def mspec(part):
            return pl.BlockSpec((None, tm, D_MODEL), lambda i, j: (l, i, part))
    return pl.pallas_call(
        _inproj_kernel,
        out_shape=jax.ShapeDtypeStruct((m, n), F32),
        grid=(m // tm, n // tn),
        in_specs=[pl.BlockSpec((tm, D_MODEL), lambda i, j: (i, 0)),
                  mspec(0), mspec(1),
                  pl.BlockSpec((None, 1, D_MODEL), lambda i, j: (l, 0, 0)),
                  pl.BlockSpec((None, tn, D_MODEL), lambda i, j: (l, j, 0))],
        out_specs=pl.BlockSpec((tm, tn), lambda i, j: (i, j)),
        scratch_shapes=[pltpu.VMEM((tm, D_MODEL), BF16)],
        compiler_params=_cparams(("parallel", "arbitrary")),
        name="inproj",
    )(x2d, mod, mod, norm_g3, w_in_p)


def _merge_kernel(ya_ref, yb_ref, yc_ref, ga_ref, gb_ref, gc_ref, wa_ref, wb_ref, wc_ref, o_ref):
    acc = _sigmoid(ga_ref[...]) * jnp.dot(ya_ref[...].astype(BF16), wa_ref[...], preferred_element_type=F32)
    acc += _sigmoid(gb_ref[...]) * jnp.dot(yb_ref[...].astype(BF16), wb_ref[...], preferred_element_type=F32)
    acc += _sigmoid(gc_ref[...]) * jnp.dot(yc_ref[...].astype(BF16), wc_ref[...], preferred_element_type=F32)
    o_ref[...] = acc.astype(BF16)


def _merge(ya, yb, yc, u, w_branch_b, l, tm, tn):
    m = ya.shape[0]
    yspec = pl.BlockSpec((tm, MIX_W), lambda i, j: (i, 0))

    def gspec(br):
        return pl.BlockSpec((tm, tn), lambda i, j: (i, (OFF_MERGE + br * D_MODEL) // tn + j))

    def wspec(br):
        return pl.BlockSpec((None, None, MIX_W, tn), lambda i, j: (l, br, 0, j))

    return pl.pallas_call(
        _merge_kernel,
        out_shape=jax.ShapeDtypeStruct((m, D_MODEL), BF16),
        grid=(m // tm, D_MODEL // tn),
        in_specs=[yspec, yspec, yspec, gspec(0), gspec(1), gspec(2), wspec(0), wspec(1), wspec(2)],
        out_specs=pl.BlockSpec((tm, tn), lambda i, j: (i, j)),
        compiler_params=_cparams(("parallel", "parallel")),
        name="branch_merge",
    )(ya, yb, yc, u, u, u, w_branch_b, w_branch_b, w_branch_b)


def _outproj_kernel(m_ref, w_ref, x_ref, gate_ref, o_ref):
    o_ref[...] = x_ref[...] + gate_ref[...] * jnp.dot(m_ref[...], w_ref[...], preferred_element_type=F32)


def _outproj(merged, w_out_b, x2d, mod, l, rows_per_mod, tm, tn):
    m = x2d.shape[0]
    gate_blk = (2 * D_MODEL) // tn
    if rows_per_mod > 1:
        gspec = pl.BlockSpec((None, None, 1, tn), lambda i, j: (l, (i * tm) // rows_per_mod, 0, gate_blk + j))
    else:
        gspec = pl.BlockSpec((None, tm, tn), lambda i, j: (l, i, gate_blk + j))
    return pl.pallas_call(
        _outproj_kernel,
        out_shape=jax.ShapeDtypeStruct((m, D_MODEL), F32),
        grid=(m // tm, D_MODEL // tn),
        in_specs=[pl.BlockSpec((tm, D_MODEL), lambda i, j: (i, 0)),
                  pl.BlockSpec((None, D_MODEL, tn), lambda i, j: (l, 0, j)),
                  pl.BlockSpec((tm, tn), lambda i, j: (i, j)),
                  gspec],
        out_specs=pl.BlockSpec((tm, tn), lambda i, j: (i, j)),
        compiler_params=_cparams(("parallel", "parallel")),
        name="outproj",
    )(merged, w_out_b, x2d, mod)


def _final_norm_kernel(x_ref, g_ref, o_ref):
    x = x_ref[...]
    o_ref[...] = x * lax.rsqrt(jnp.mean(x * x, axis=-1, keepdims=True) + NORM_EPS) * g_ref[...]


def _final_norm(x2d, g, tm):
    m = x2d.shape[0]
    return pl.pallas_call(
        _final_norm_kernel,
        out_shape=jax.ShapeDtypeStruct((m, D_MODEL), F32),
        grid=(m // tm,),
        in_specs=[pl.BlockSpec((tm, D_MODEL), lambda i: (i, 0)), pl.BlockSpec((1, D_MODEL), lambda i: (0, 0))],
        out_specs=pl.BlockSpec((tm, D_MODEL), lambda i: (i, 0)),
        compiler_params=_cparams(("parallel",)),
        name="final_norm",
    )(x2d, g.reshape(1, D_MODEL))


PV_MIX_R, PV_MIX_K, PV_MIX_V, PV_W0, PV_A0, PV_KK, PV_KA, PV_RK, PV_LNG, PV_LNB = range(10)


def _half_sum(x, lo_mask):
    s_lo = jnp.sum(jnp.where(lo_mask, x, 0.0), axis=-1, keepdims=True)
    s_hi = jnp.sum(jnp.where(lo_mask, 0.0, x), axis=-1, keepdims=True)
    return jnp.where(lo_mask, s_lo, s_hi)


def _rwkv_prompt_kernel(r_ref, k_ref, v_ref, g_ref, wd_ref, ad_ref, pv_ref, mixl_ref, w2_ref, a2_ref,
                        shr_ref, shk_ref, shv_ref, shwd_ref, shad_ref, s0_ref,
                        y_ref, sout_ref, xbuf, sb, *, n_chunks, pairs, nck):
    c = RWKV_CHUNK
    gw = pairs * LANE
    ti = pl.program_id(2)
    lo = lax.broadcasted_iota(jnp.int32, (1, LANE), 1) < HEAD_A

    @pl.when(ti == 0)
    def _():
        xbuf[0:SUBLANE, :] = jnp.concatenate(
            [shr_ref[...], shk_ref[...], shv_ref[...], shwd_ref[...], shad_ref[...]], axis=1)
        zero = jnp.zeros((HEAD_A, HEAD_A), F32)
        for g in range(pairs):
            top = jnp.concatenate([s0_ref[2 * g], zero], axis=1)
            bot = jnp.concatenate([zero, s0_ref[2 * g + 1]], axis=1)
            sb[g] = jnp.concatenate([top, bot], axis=0)

    tc = nck * c
    cur = jnp.concatenate([r_ref[...], k_ref[...], v_ref[...], wd_ref[...], ad_ref[...]], axis=1)
    xbuf[SUBLANE:SUBLANE + tc, :] = cur
    prev = xbuf[pl.ds(SUBLANE - 1, tc), :]
    pv = pv_ref[...]
    mixl = mixl_ref[...]
    mix = jnp.concatenate([pv[PV_MIX_R:PV_MIX_R + 1], pv[PV_MIX_K:PV_MIX_K + 1], pv[PV_MIX_V:PV_MIX_V + 1],
                           mixl[0:1], mixl[1:2]], axis=1)
    xbuf[0:SUBLANE, :] = cur[tc - SUBLANE:tc, :]

    w0, a0 = pv[PV_W0:PV_W0 + 1], pv[PV_A0:PV_A0 + 1]
    k_k, k_a, r_k = pv[PV_KK:PV_KK + 1], pv[PV_KA:PV_KA + 1], pv[PV_RK:PV_RK + 1]
    ln_g, ln_b = pv[PV_LNG:PV_LNG + 1], pv[PV_LNB:PV_LNB + 1]
    rowc, colc = _iota2(c, c)
    lower_01 = (rowc >= colc).astype(BF16)

    n = 2 * c
    row, col = _iota2(n, n)
    same_head = (row >> 6) == (col >> 6)
    strict = same_head & (row > col)
    incl = same_head & (row >= col)
    eye = row == col
    mm = functools.partial(_dot, mode=P_CHUNK)

    def stack(x):
        return jnp.concatenate([jnp.where(lo, x, 0.0), jnp.where(lo, 0.0, x)], axis=0)

    gs = range(pairs)
    sls = [slice(g * LANE, (g + 1) * LANE) for g in gs]
    state = {"s": [sb[g] for g in gs]}
    out_rows = [None] * nck

    def chunk_stages(k):
        rows = slice(k * c, (k + 1) * c)
        cur_k = cur[rows]
        z = cur_k + (prev[rows] - cur_k) * mix
        zr, zk, zv = z[:, 0:gw], z[:, gw:2 * gw], z[:, 2 * gw:3 * gw]
        zwd, zad = z[:, 3 * gw:3 * gw + LANE], z[:, 3 * gw + LANE:]
        logw = -_softplus(-(w0 + _dot(jnp.tanh(zwd), w2_ref[...], mode=P_LORA))) - 0.5
        lw = -jnp.exp(logw)
        a = _sigmoid(a0 + _dot(zad, a2_ref[...], mode=P_LORA))
        kkp = zk * k_k
        kmod = zk * (1.0 + (a - 1.0) * k_a)
        yield
        gcum = _dot_exact_lhs(lower_01, lw)
        g_end = gcum[c - 1:c, :]
        e_pos, e_neg = jnp.exp(gcum), jnp.exp(-gcum)
        e_prev, e_end = jnp.exp(gcum - lw), jnp.exp(g_end - gcum)
        eg_end = jnp.exp(g_end)
        rt, kb, kend = zr * e_pos, kmod * e_neg, kmod * e_end
        yield
        kk = [kkp[:, sl] * lax.rsqrt(_half_sum(kkp[:, sl] * kkp[:, sl], lo) + 1e-6) for sl in sls]
        bt = [kk[g] * a[:, sls[g]] for g in gs]
        rt_s = [stack(rt[:, sl]) for sl in sls]
        kb_s = [stack(kb[:, sl]) for sl in sls]
        bb_s = [stack(bt[g] * e_neg[:, sls[g]]) for g in gs]
        kt_s = [stack(kk[g] * e_prev[:, sls[g]]) for g in gs]
        yield
        kend_s = [stack(kend[:, sl]) for sl in sls]
        bend_s = [stack(bt[g] * e_end[:, sls[g]]) for g in gs]
        v_s = [stack(zv[:, sl]) for sl in sls]
        p_all = [mm(jnp.concatenate([kt_s[g], rt_s[g]], axis=0), jnp.concatenate([bb_s[g], kb_s[g]], axis=0), _NT)
                 for g in gs]
        yield
        a_ub = [jnp.where(strict, x[0:n, 0:n], 0.0) for x in p_all]
        a_vk = [jnp.where(strict, x[0:n, n:2 * n], 0.0) for x in p_all]
        p_rb = [jnp.where(incl, x[n:2 * n, 0:n], 0.0) for x in p_all]
        p_rk = [jnp.where(incl, x[n:2 * n, n:2 * n], 0.0) for x in p_all]
        av = [mm(a_vk[g], v_s[g]) for g in gs]
        pv_ = [mm(p_rk[g], v_s[g]) for g in gs]
        t_inv = yield from _inv_unit_lower(a_ub, n, levels=2)
        yield
        m12 = [mm(t_inv[g], jnp.concatenate([kt_s[g], av[g]], axis=1)) for g in gs]
        yield
        w2m = [mm(p_rb[g], m12[g]) for g in gs]
        q1 = [rt_s[g] - w2m[g][:, 0:LANE] for g in gs]
        q2 = [pv_[g] - w2m[g][:, LANE:2 * LANE] for g in gs]
        phi = [jnp.where(eye, eg_end[:, sls[g]], 0.0) - mm(bend_s[g], m12[g][:, 0:LANE], _TN) for g in gs]
        psi_t = [mm(jnp.concatenate([v_s[g], m12[g][:, LANE:2 * LANE]], axis=0),
                    jnp.concatenate([kend_s[g], -bend_s[g]], axis=0), _TN) for g in gs]
        yield
        s_old = state["s"]
        y_s = [_dot(q1[g], s_old[g], _NT, P_STATE) + q2[g] for g in gs]
        state["s"] = [_dot(s_old[g], phi[g], _NT, P_STATE) + psi_t[g] for g in gs]
        yield
        gate = _silu(g_ref[rows, :])
        rk_prod = zr * kmod * r_k
        outs = []
        for g in gs:
            sl = sls[g]
            y2 = y_s[g][0:c, :] + y_s[g][c:n, :]
            mu = _half_sum(y2, lo) * (1.0 / HEAD_A)
            dlt = y2 - mu
            var = _half_sum(dlt * dlt, lo) * (1.0 / HEAD_A)
            yn = dlt * lax.rsqrt(var + GN_EPS_A) * ln_g[:, sl] + ln_b[:, sl]
            bonus = _half_sum(rk_prod[:, sl], lo) * zv[:, sl]
            outs.append((yn + bonus) * gate[:, sl])
        out_rows[k] = jnp.concatenate(outs, axis=1)

    _interleave([chunk_stages(k) for k in range(nck)], lead=RWKV_LEAD)
    for g in gs:
        sb[g] = state["s"][g]
    y_ref[...] = jnp.concatenate(out_rows, axis=0).astype(y_ref.dtype)

    @pl.when(ti == n_chunks - 1)
    def _():
        for g in range(pairs):
            s_fin = sb[g]
            sout_ref[2 * g] = s_fin[0:HEAD_A, 0:HEAD_A]
            sout_ref[2 * g + 1] = s_fin[HEAD_A:2 * HEAD_A, HEAD_A:2 * HEAD_A]


def _rwkv_prompt(u, nb, t, pvec, mixl, w2p, a2p, sh_init, s0, l, pairs=8, nck=2):
    c = nck * RWKV_CHUNK
    nt = t // c
    gw = pairs * LANE
    groups = (H_A // 2) // pairs

    def uspec(off):
        return pl.BlockSpec((c, gw), lambda b, p, i: (b * nt + i, off // gw + p))

    def lspec(off):
        return pl.BlockSpec((c, LANE), lambda b, p, i: (b * nt + i, off // LANE))

    def shspec(off):
        return pl.BlockSpec((None, SUBLANE, gw), lambda b, p, i: (b, 0, off // gw + p))

    def shlspec(off):
        return pl.BlockSpec((None, SUBLANE, LANE), lambda b, p, i: (b, 0, off // LANE))

    group_cols = lambda rows: pl.BlockSpec((None, rows, gw), lambda b, p, i: (l, 0, p))
    return pl.pallas_call(
        functools.partial(_rwkv_prompt_kernel, n_chunks=nt, pairs=pairs, nck=nck),
        out_shape=(jax.ShapeDtypeStruct((nb * t, MIX_W), BF16),
                   jax.ShapeDtypeStruct((nb, H_A, HEAD_A, HEAD_A), F32)),
        grid=(nb, groups, nt),
        in_specs=[uspec(OFF_R), uspec(OFF_K), uspec(OFF_V), uspec(OFF_AG), lspec(OFF_WD), lspec(OFF_AD),
                  group_cols(16),
                  pl.BlockSpec((None, SUBLANE, LANE), lambda b, p, i: (l, 0, 0)),
                  group_cols(LANE), group_cols(LANE),
                  shspec(0), shspec(MIX_W), shspec(2 * MIX_W), shlspec(3 * MIX_W), shlspec(3 * MIX_W + LANE),
                  pl.BlockSpec((None, 2 * pairs, HEAD_A, HEAD_A), lambda b, p, i: (b, p, 0, 0))],
        out_specs=(pl.BlockSpec((c, gw), lambda b, p, i: (b * nt + i, p)),
                   pl.BlockSpec((None, 2 * pairs, HEAD_A, HEAD_A), lambda b, p, i: (b, p, 0, 0))),
        scratch_shapes=[pltpu.VMEM((c + SUBLANE, 3 * gw + 2 * LANE), F32),
                        pltpu.VMEM((pairs, 2 * HEAD_A, 2 * HEAD_A), F32)],
        compiler_params=_cparams(("parallel", "parallel", "arbitrary")),
        name="rwkv_prompt",
    )(u, u, u, u, u, u, pvec, mixl, w2p, a2p, sh_init, sh_init, sh_init, sh_init, sh_init, s0)


TB_R, TB_W, TB_KKP, TB_A, TB_KM, TB_V, TB_GATE = range(7)
PT_RK, PT_LNG, PT_LNB = range(3)


def _rwkv_step_kernel(rkv_ref, g_ref, wd_ref, ad_ref, prev_ref, mix_ref, pv_ref, w2_ref, a2_ref, pt_ref,
                      s_ref, acc_ref, y_ref, sout_ref, tab, ybuf):
    del acc_ref
    h = pl.program_id(0)

    @pl.when(h == 0)
    def _():
        cur = jnp.concatenate([rkv_ref[...], wd_ref[...], ad_ref[...]], axis=1)
        z = cur + (prev_ref[...] - cur) * mix_ref[...]
        zr, zk, zv = z[:, 0:MIX_W], z[:, MIX_W:2 * MIX_W], z[:, 2 * MIX_W:3 * MIX_W]
        zwd, zad = z[:, 3 * MIX_W:3 * MIX_W + LANE], z[:, 3 * MIX_W + LANE:]
        pv = pv_ref[...]
        logw = -_softplus(-(pv[PV_W0:PV_W0 + 1] + _dot(jnp.tanh(zwd), w2_ref[...], mode=P_STEP))) - 0.5
        a = _sigmoid(pv[PV_A0:PV_A0 + 1] + _dot(zad, a2_ref[...], mode=P_STEP))
        tab[TB_R] = zr.T
        tab[TB_W] = jnp.exp(-jnp.exp(logw)).T
        tab[TB_KKP] = (zk * pv[PV_KK:PV_KK + 1]).T
        tab[TB_A] = a.T
        tab[TB_KM] = (zk * (1.0 + (a - 1.0) * pv[PV_KA:PV_KA + 1])).T
        tab[TB_V] = zv.T
        tab[TB_GATE] = _silu(g_ref[...]).T

    rows = pl.ds(pl.multiple_of(h * HEAD_A, HEAD_A), HEAD_A)
    r_t, w_t, km_t = tab[TB_R, rows, :], tab[TB_W, rows, :], tab[TB_KM, rows, :]
    kkp_t = tab[TB_KKP, rows, :]
    kk_t = kkp_t * lax.rsqrt(jnp.sum(kkp_t * kkp_t, axis=0, keepdims=True) + 1e-6)
    bt_t = kk_t * tab[TB_A, rows, :]
    v_t = tab[TB_V, rows, :]
    for v in range(HEAD_A):
        s = s_ref[v]
        sa = jnp.sum(s * kk_t, axis=0, keepdims=True)
        s_new = s * w_t - sa * bt_t + v_t[v:v + 1, :] * km_t
        sout_ref[v] = s_new
        ybuf[v:v + 1, :] = jnp.sum(s_new * r_t, axis=0, keepdims=True)
    y = ybuf[...]
    mu = jnp.mean(y, axis=0, keepdims=True)
    dlt = y - mu
    var = jnp.mean(dlt * dlt, axis=0, keepdims=True)
    yn = dlt * lax.rsqrt(var + GN_EPS_A) * pt_ref[PT_LNG] + pt_ref[PT_LNB]
    bonus = jnp.sum(r_t * km_t * pt_ref[PT_RK], axis=0, keepdims=True) * v_t
    y_ref[...] = (yn + bonus) * tab[TB_GATE, rows, :]


def _rwkv_step(u, prev_pad, mix_pad, pvec, w2p, a2p, ptab, s_all, s_acc, l):
    nb = u.shape[0]
    lfull = lambda shape: pl.BlockSpec((None,) + shape, lambda h: (l,) + tuple(0 for _ in shape))
    st_spec = pl.BlockSpec((None, None, HEAD_A, HEAD_A, nb), lambda h: (l, h, 0, 0, 0))
    return pl.pallas_call(
        _rwkv_step_kernel,
        out_shape=(jax.ShapeDtypeStruct((MIX_W, nb), F32),
                   jax.ShapeDtypeStruct(s_acc.shape, F32)),
        grid=(H_A,),
        in_specs=[pl.BlockSpec((nb, 3 * MIX_W), lambda h: (0, 0)),
                  pl.BlockSpec((nb, MIX_W), lambda h: (0, OFF_AG // MIX_W)),
                  pl.BlockSpec((nb, LANE), lambda h: (0, OFF_WD // LANE)),
                  pl.BlockSpec((nb, LANE), lambda h: (0, OFF_AD // LANE)),
                  lfull((nb, SHIFT_PAD)),
                  lfull((1, SHIFT_PAD)), lfull((16, MIX_W)), lfull((LANE, MIX_W)), lfull((LANE, MIX_W)),
                  pl.BlockSpec((None, 3, HEAD_A, LANE), lambda h: (l, 0, h, 0)),
                  st_spec,
                  pl.BlockSpec(memory_space=pl.ANY)],
        out_specs=(pl.BlockSpec((HEAD_A, nb), lambda h: (h, 0)), st_spec),
        scratch_shapes=[pltpu.VMEM((7, MIX_W, nb), F32), pltpu.VMEM((HEAD_A, nb), F32)],
        input_output_aliases={11: 1},
        compiler_params=_cparams(("arbitrary",)),
        name="rwkv_step",
    )(u, u, u, u, prev_pad, mix_pad, pvec, w2p, a2p, ptab, s_all, s_acc)


LP_CONVB, LP_BA, LP_BX, LP_LAM = range(4)


def _lru_gates(xc, wa_ref, wx_ref, lp):
    r = _sigmoid(_mm_bf16(xc, wa_ref[...]) + lp[LP_BA:LP_BA + 1])
    i = _sigmoid(_mm_bf16(xc, wx_ref[...]) + lp[LP_BX:LP_BX + 1])
    log_a = -LRU_C * r * _softplus(-lp[LP_LAM:LP_LAM + 1])
    a = jnp.exp(log_a)
    b = jnp.sqrt(jnp.tanh(-log_a) * (a * a + 1.0)) * (i * xc)
    return a, b


def _lru_prompt_kernel(x_ref, g_ref, cw_ref, lp_ref, wa_ref, wx_ref, cinit_ref, h0_ref,
                       y_ref, hout_ref, xbuf, hcar, *, n_blocks, tc):
    ti = pl.program_id(1)

    @pl.when(ti == 0)
    def _():
        xbuf[0:SUBLANE, :] = cinit_ref[...]
        hcar[...] = h0_ref[...]

    x = x_ref[...]
    xbuf[SUBLANE:SUBLANE + tc, :] = x
    cw = cw_ref[...]
    lp = lp_ref[...]
    xc = (xbuf[pl.ds(SUBLANE - 3, tc), :] * cw[0:1] + xbuf[pl.ds(SUBLANE - 2, tc), :] * cw[1:2]
          + xbuf[pl.ds(SUBLANE - 1, tc), :] * cw[2:3] + x * cw[3:4] + lp[LP_CONVB:LP_CONVB + 1])
    xbuf[0:SUBLANE, :] = x[tc - SUBLANE:tc, :]
    a, b = _lru_gates(xc, wa_ref, wx_ref, lp)
    n_tiles = tc // SUBLANE
    a = a.reshape(n_tiles, SUBLANE, MIX_W)
    b = b.reshape(n_tiles, SUBLANE, MIX_W)
    sub = lax.broadcasted_iota(jnp.int32, (n_tiles, SUBLANE, MIX_W), 1)
    s = 1
    while s < SUBLANE:
        a_sh = pltpu.roll(a, s, 1)
        b_sh = pltpu.roll(b, s, 1)
        m = sub >= s
        b = jnp.where(m, a * b_sh + b, b)
        a = jnp.where(m, a * a_sh, a)
        s *= 2
    carry = hcar[...]
    tiles = []
    for j in range(n_tiles):
        h_j = b[j] + a[j] * carry
        carry = h_j[SUBLANE - 1:SUBLANE, :]
        tiles.append(h_j)
    h = jnp.concatenate(tiles, axis=0)
    hcar[...] = carry
    y_ref[...] = (h * _silu(g_ref[...])).astype(y_ref.dtype)

    @pl.when(ti == n_blocks - 1)
    def _():
        hout_ref[...] = h[tc - 1:tc, :]


def _lru_prompt(u, nb, t, conv_w, lp, wa_d, wx_d, cinit, h0, l, tc=256):
    nt = t // tc
    const = lambda shape: pl.BlockSpec((None,) + shape, lambda b, i: (l,) + tuple(0 for _ in shape))
    return pl.pallas_call(
        functools.partial(_lru_prompt_kernel, n_blocks=nt, tc=tc),
        out_shape=(jax.ShapeDtypeStruct((nb * t, MIX_W), BF16),
                   jax.ShapeDtypeStruct((nb, 1, MIX_W), F32)),
        grid=(nb, nt),
        in_specs=[pl.BlockSpec((tc, MIX_W), lambda b, i: (b * nt + i, OFF_BX // MIX_W)),
                  pl.BlockSpec((tc, MIX_W), lambda b, i: (b * nt + i, OFF_BG // MIX_W)),
                  const((CONV_W, MIX_W)), const((SUBLANE, MIX_W)),
                  const((MIX_W, MIX_W)), const((MIX_W, MIX_W)),
                  pl.BlockSpec((None, SUBLANE, MIX_W), lambda b, i: (b, 0, 0)),
                  pl.BlockSpec((None, 1, MIX_W), lambda b, i: (b, 0, 0))],
        out_specs=(pl.BlockSpec((tc, MIX_W), lambda b, i: (b * nt + i, 0)),
                   pl.BlockSpec((None, 1, MIX_W), lambda b, i: (b, 0, 0))),
        scratch_shapes=[pltpu.VMEM((tc + SUBLANE, MIX_W), F32), pltpu.VMEM((1, MIX_W), F32)],
        compiler_params=_cparams(("parallel", "arbitrary")),
        name="lru_prompt",
    )(u, u, conv_w, lp, wa_d, wx_d, cinit, h0)


def _lru_step_kernel(x_ref, g_ref, buf_ref, h0_ref, cw_ref, lp_ref, wa_ref, wx_ref, y_ref, hout_ref):
    x = x_ref[...]
    cw = cw_ref[...]
    lp = lp_ref[...]
    xc = buf_ref[0] * cw[0:1] + buf_ref[1] * cw[1:2] + buf_ref[2] * cw[2:3] + x * cw[3:4] + lp[LP_CONVB:LP_CONVB + 1]
    a, b = _lru_gates(xc, wa_ref, wx_ref, lp)
    h = a * h0_ref[...] + b
    hout_ref[...] = h
    y_ref[...] = h * _silu(g_ref[...])


def _lru_step(u, buf_t, h0, conv_w, lp, wa_d, wx_d, l):
    nb = u.shape[0]
    const = lambda shape: pl.BlockSpec((None,) + shape, lambda i: (l,) + tuple(0 for _ in shape))
    out = pl.BlockSpec((nb, MIX_W), lambda i: (0, 0))
    return pl.pallas_call(
        _lru_step_kernel,
        out_shape=(jax.ShapeDtypeStruct((nb, MIX_W), F32), jax.ShapeDtypeStruct((nb, MIX_W), F32)),
        grid=(1,),
        in_specs=[pl.BlockSpec((nb, MIX_W), lambda i: (0, OFF_BX // MIX_W)),
                  pl.BlockSpec((nb, MIX_W), lambda i: (0, OFF_BG // MIX_W)),
                  const((CONV_W - 1, nb, MIX_W)), const((nb, MIX_W)),
                  const((CONV_W, MIX_W)), const((SUBLANE, MIX_W)),
                  const((MIX_W, MIX_W)), const((MIX_W, MIX_W))],
        out_specs=(out, out),
        compiler_params=_cparams(("arbitrary",)),
        name="lru_step",
    )(u, u, buf_t, h0, conv_w, lp, wa_d, wx_d)


GP_ALOG, GP_DTB, GP_NORM = range(3)


def _gdn_prompt_kernel(q_ref, k_ref, v_ref, z_ref, ab_ref, cwq_ref, cwk_ref, cwv_ref, gp_ref,
                       ciq_ref, cik_ref, civ_ref, s0_ref, y_ref, sout_ref, xbuf, st, *, n_chunks, heads, nck):
    c = GDN_CHUNK
    tc = nck * c
    hw = heads * LANE
    h_base = pl.program_id(1) * heads
    ti = pl.program_id(2)

    @pl.when(ti == 0)
    def _():
        xbuf[0:SUBLANE, :] = jnp.concatenate([ciq_ref[...], cik_ref[...], civ_ref[...]], axis=1)
        st[...] = s0_ref[...]

    cur = jnp.concatenate([q_ref[...], k_ref[...], v_ref[...]], axis=1)
    xbuf[SUBLANE:SUBLANE + tc, :] = cur
    cw = jnp.concatenate([cwq_ref[...], cwk_ref[...], cwv_ref[...]], axis=1)

    gp = gp_ref[...]
    norm_g = gp[GP_NORM:GP_NORM + 1]
    lane_c = lax.broadcasted_iota(jnp.int32, (c, LANE), 1)
    row, col = _iota2(c, c)
    incl = row >= col
    strict = row > col
    eye = row == col
    incl_01 = incl.astype(BF16)
    mm = functools.partial(_dot, mode=P_CHUNK)
    hs_ = range(heads)
    state = {"s": [st[j] for j in hs_]}
    out_rows = [None] * nck

    def chunk_stages(k):
        base = SUBLANE + k * c
        x0 = cur[k * c:(k + 1) * c]
        conv = (xbuf[pl.ds(base - 3, c), :] * cw[0:1] + xbuf[pl.ds(base - 2, c), :] * cw[1:2]
                + xbuf[pl.ds(base - 1, c), :] * cw[2:3] + x0 * cw[3:4])
        act = _silu(conv)
        yield
        ab = ab_ref[k * c:(k + 1) * c, :]
        g_all = -jnp.exp(gp[GP_ALOG:GP_ALOG + 1]) * _softplus(ab + gp[GP_DTB:GP_DTB + 1])
        beta_all = _sigmoid(ab)
        qh = [act[:, j * DK:(j + 1) * DK] for j in hs_]
        kh = [act[:, hw + j * DK:hw + (j + 1) * DK] for j in hs_]
        vh = [act[:, 2 * hw + j * DV:2 * hw + (j + 1) * DV] for j in hs_]
        qn = [x * lax.rsqrt(jnp.sum(x * x, axis=-1, keepdims=True) + 1e-6) * (DK ** -0.5) for x in qh]
        kn = [x * lax.rsqrt(jnp.sum(x * x, axis=-1, keepdims=True) + 1e-6) for x in kh]
        beta = [jnp.sum(jnp.where(lane_c == h_base + j + H_C, beta_all, 0.0), axis=-1, keepdims=True) for j in hs_]
        yield
        gc_all = _dot_exact_lhs(incl_01, g_all)
        gc = [jnp.broadcast_to(jnp.sum(jnp.where(lane_c == h_base + j, gc_all, 0.0), axis=-1, keepdims=True),
                               (c, LANE)) for j in hs_]
        gc_row = [jnp.sum(jnp.where(eye, x, 0.0), axis=0, keepdims=True) for x in gc]
        decay = [jnp.where(incl, jnp.exp(jnp.where(incl, gc[j] - gc_row[j], 0.0)), 0.0) for j in hs_]
        g_last = [x[c - 1:c, :] for x in gc]
        eg = [jnp.exp(x) for x in gc]
        yield
        kb = [kn[j] * beta[j] for j in hs_]
        p_all = [mm(jnp.concatenate([kb[j], qn[j]], axis=0), kn[j], _NT) for j in hs_]
        yield
        a_mat = [jnp.where(strict, p_all[j][0:c] * decay[j], 0.0) for j in hs_]
        qk = [p_all[j][c:2 * c] * decay[j] for j in hs_]
        k_dec = [kn[j] * jnp.exp(g_last[j] - gc[j]) for j in hs_]
        rhs = [jnp.concatenate([kb[j] * eg[j], vh[j] * beta[j]], axis=1) for j in hs_]
        t_inv = yield from _inv_unit_lower(a_mat, c, levels=3)
        yield
        wu = [mm(t_inv[j], rhs[j]) for j in hs_]
        yield
        w3 = [mm(k_dec[j], wu[j], _TN) for j in hs_]
        w4 = [mm(qk[j], wu[j]) for j in hs_]
        phi = [jnp.where(eye, jnp.exp(g_last[j]), 0.0) - w3[j][:, 0:DK] for j in hs_]
        q1 = [qn[j] * eg[j] - w4[j][:, 0:DK] for j in hs_]
        yield
        s_old = state["s"]
        o = [_dot(q1[j], s_old[j], _NN, P_STATE) + w4[j][:, DK:] for j in hs_]
        state["s"] = [_dot(phi[j], s_old[j], _NN, P_STATE) + w3[j][:, DK:] for j in hs_]
        yield
        zg = _silu(z_ref[k * c:(k + 1) * c, :])
        outs = [o[j] * lax.rsqrt(jnp.mean(o[j] * o[j], axis=-1, keepdims=True) + NORM_EPS) * norm_g
                * zg[:, j * DV:(j + 1) * DV] for j in hs_]
        out_rows[k] = jnp.concatenate(outs, axis=1)

    _interleave([chunk_stages(k) for k in range(nck)], lead=GDN_LEAD)
    xbuf[0:SUBLANE, :] = cur[tc - SUBLANE:tc, :]
    for j in hs_:
        st[j] = state["s"][j]
    y_ref[...] = jnp.concatenate(out_rows, axis=0).astype(y_ref.dtype)

    @pl.when(ti == n_chunks - 1)
    def _():
        sout_ref[...] = st[...]


def _gdn_prompt(u, nb, t, conv_w, gpar, cinit, s0, l, heads=8, nck=2):
    assert GDN_CHUNK == LANE and DK == LANE and DV == LANE
    c = nck * GDN_CHUNK
    nt = t // c
    hw = heads * LANE

    def uspec(off):
        return pl.BlockSpec((c, hw), lambda b, h, i: (b * nt + i, off // hw + h))

    def cwspec(off):
        return pl.BlockSpec((None, CONV_W, hw), lambda b, h, i: (l, 0, off // hw + h))

    def cispec(off):
        return pl.BlockSpec((None, SUBLANE, hw), lambda b, h, i: (b, 0, off // hw + h))

    return pl.pallas_call(
        functools.partial(_gdn_prompt_kernel, n_chunks=nt, heads=heads, nck=nck),
        out_shape=(jax.ShapeDtypeStruct((nb * t, MIX_W), BF16),
                   jax.ShapeDtypeStruct((nb, H_C, DK, DV), F32)),
        grid=(nb, H_C // heads, nt),
        in_specs=[uspec(OFF_Q), uspec(OFF_Q + MIX_W), uspec(OFF_Q + 2 * MIX_W), uspec(OFF_Z),
                  pl.BlockSpec((c, LANE), lambda b, h, i: (b * nt + i, OFF_AB // LANE)),
                  cwspec(0), cwspec(MIX_W), cwspec(2 * MIX_W),
                  pl.BlockSpec((None, SUBLANE, LANE), lambda b, h, i: (l, 0, 0)),
                  cispec(0), cispec(MIX_W), cispec(2 * MIX_W),
                  pl.BlockSpec((None, heads, DK, DV), lambda b, h, i: (b, h, 0, 0))],
        out_specs=(pl.BlockSpec((c, hw), lambda b, h, i: (b * nt + i, h)),
                   pl.BlockSpec((None, heads, DK, DV), lambda b, h, i: (b, h, 0, 0))),
        scratch_shapes=[pltpu.VMEM((c + SUBLANE, 3 * hw), F32), pltpu.VMEM((heads, DK, DV), F32)],
        compiler_params=_cparams(("parallel", "parallel", "arbitrary")),
        name="gdn_prompt",
    )(u, u, u, u, u, conv_w, conv_w, conv_w, gpar, cinit, cinit, cinit, s0)


def _gdn_step_kernel(qkv_ref, z_ref, ab_ref, buf_ref, cw_ref, gp_ref, s_ref, acc_ref, y_ref, sout_ref, hs, *, bb):
    del acc_ref
    cw = cw_ref[...]
    cur = qkv_ref[...]
    act = _silu(buf_ref[0] * cw[0:1] + buf_ref[1] * cw[1:2] + buf_ref[2] * cw[2:3] + cur * cw[3:4])
    gp = gp_ref[...]
    ab = ab_ref[...]
    g_all = -jnp.exp(gp[GP_ALOG:GP_ALOG + 1]) * _softplus(ab + gp[GP_DTB:GP_DTB + 1])
    beta_all = _sigmoid(ab)
    zg = _silu(z_ref[...])
    hs_q, hs_k, hs_v, hs_eg, hs_beta, hs_zg, hs_qk = range(7)
    for h in range(H_C):
        qh = act[:, h * DK:(h + 1) * DK]
        kh = act[:, MIX_W + h * DK:MIX_W + (h + 1) * DK]
        qn = qh * lax.rsqrt(jnp.sum(qh * qh, axis=-1, keepdims=True) + 1e-6) * (DK ** -0.5)
        kn = kh * lax.rsqrt(jnp.sum(kh * kh, axis=-1, keepdims=True) + 1e-6)
        hs[hs_q, h] = qn
        hs[hs_k, h] = kn
        hs[hs_v, h] = act[:, 2 * MIX_W + h * DV:2 * MIX_W + (h + 1) * DV]
        hs[hs_eg, h] = jnp.broadcast_to(jnp.exp(g_all[:, h:h + 1]), (bb, LANE))
        hs[hs_beta, h] = jnp.broadcast_to(beta_all[:, H_C + h:H_C + h + 1], (bb, LANE))
        hs[hs_zg, h] = zg[:, h * DV:(h + 1) * DV]
        hs[hs_qk, h] = jnp.broadcast_to(jnp.sum(qn * kn, axis=-1, keepdims=True), (bb, LANE))

    row, col = _iota2(DK, DV)
    diag = row == col
    rowb = lax.broadcasted_iota(jnp.int32, (bb, DV), 0)
    norm_g = gp[GP_NORM:GP_NORM + 1]
    heads = range(H_C)

    def body(b, outs):
        def rows(q):
            return [hs[q, h, pl.ds(b, 1), :] for h in heads]

        q, k, v, eg, beta = rows(hs_q), rows(hs_k), rows(hs_v), rows(hs_eg), rows(hs_beta)
        s = [s_ref[b, h] for h in heads]
        kcol = [jnp.sum(jnp.where(diag, jnp.broadcast_to(x, (DK, DV)), 0.0), axis=-1, keepdims=True) for x in k]
        qcol = [jnp.sum(jnp.where(diag, jnp.broadcast_to(x, (DK, DV)), 0.0), axis=-1, keepdims=True) for x in q]
        ks = [jnp.sum(kcol[h] * s[h], axis=0, keepdims=True) for h in heads]
        qs = [jnp.sum(qcol[h] * s[h], axis=0, keepdims=True) for h in heads]
        v_new = [beta[h] * (v[h] - eg[h] * ks[h]) for h in heads]
        for h in heads:
            sout_ref[b, h] = s[h] * eg[h] + kcol[h] * v_new[h]
        qk = rows(hs_qk)
        o = [eg[h] * qs[h] + qk[h] * v_new[h] for h in heads]
        ms = [jnp.mean(x * x, axis=-1, keepdims=True) for x in o]
        zg_h = rows(hs_zg)
        res = [o[h] * lax.rsqrt(ms[h] + NORM_EPS) * norm_g * zg_h[h] for h in heads]
        return tuple(jnp.where(rowb == b, jnp.broadcast_to(res[h], (bb, DV)), outs[h]) for h in heads)

    outs = lax.fori_loop(0, bb, body, tuple(jnp.zeros((bb, DV), F32) for _ in heads))
    y_ref[...] = jnp.concatenate(outs, axis=1)


def _gdn_step(u, buf_t, conv_w, gpar, s_all, s_acc, l, bb=SUBLANE):
    nb = u.shape[0]
    lfull = lambda shape: pl.BlockSpec((None,) + shape, lambda i: (l,) + tuple(0 for _ in shape))
    st_spec = pl.BlockSpec((None, bb, H_C, DK, DV), lambda i: (l, i, 0, 0, 0))
    return pl.pallas_call(
        functools.partial(_gdn_step_kernel, bb=bb),
        out_shape=(jax.ShapeDtypeStruct((nb, MIX_W), F32),
                   jax.ShapeDtypeStruct(s_acc.shape, F32)),
        grid=(nb // bb,),
        in_specs=[pl.BlockSpec((bb, QKV_W), lambda i: (i, OFF_Q // QKV_W)),
                  pl.BlockSpec((bb, MIX_W), lambda i: (i, OFF_Z // MIX_W)),
                  pl.BlockSpec((bb, LANE), lambda i: (i, OFF_AB // LANE)),
                  pl.BlockSpec((None, CONV_W - 1, bb, QKV_W), lambda i: (l, 0, i, 0)),
                  lfull((CONV_W, QKV_W)), lfull((SUBLANE, LANE)),
                  st_spec,
                  pl.BlockSpec(memory_space=pl.ANY)],
        out_specs=(pl.BlockSpec((bb, MIX_W), lambda i: (i, 0)), st_spec),
        scratch_shapes=[pltpu.VMEM((7, H_C, bb, LANE), F32)],
        input_output_aliases={7: 1},
        compiler_params=_cparams(("parallel",)),
        name="gdn_step",
    )(u, u, u, buf_t, conv_w, gpar, s_all, s_acc)


def _pad_cols(w, width):
    pad = width - w.shape[-1]
    return jnp.pad(w, [(0, 0)] * (w.ndim - 1) + [(0, pad)])


def _layout_in_cols(w):
    a_sh_end = A_SHIFT_W
    o_ag = a_sh_end
    o_bx = o_ag + MIX_W
    o_bg = o_bx + MIX_W
    o_q = o_bg + MIX_W
    o_a = o_q + QKV_W
    o_z = o_a + 2 * H_C
    o_m = o_z + MIX_W
    parts = [w[..., 0:3 * MIX_W], w[..., o_ag:o_bx], w[..., o_bx:o_bg], w[..., o_bg:o_q], w[..., o_q:o_a],
             w[..., o_z:o_m], w[..., o_m:IN_COLS],
             _pad_cols(w[..., 3 * MIX_W:3 * MIX_W + LORA], LANE),
             _pad_cols(w[..., 3 * MIX_W + LORA:a_sh_end], LANE),
             _pad_cols(w[..., o_a:o_z], LANE)]
    used = sum(part.shape[-1] for part in parts)
    parts.append(jnp.zeros(w.shape[:-1] + (N_PAD - used,), w.dtype))
    return jnp.concatenate(parts, axis=-1)


def _shift_to_padded(s):
    return jnp.concatenate([s[..., 0:3 * MIX_W], _pad_cols(s[..., 3 * MIX_W:3 * MIX_W + LORA], LANE),
                            _pad_cols(s[..., 3 * MIX_W + LORA:], LANE)], axis=-1)


def _shift_from_u(u_rows):
    return jnp.concatenate([u_rows[..., 0:3 * MIX_W], u_rows[..., OFF_WD:OFF_WD + LORA],
                            u_rows[..., OFF_AD:OFF_AD + LORA]], axis=-1)


def _block_diag(w):
    eye = jnp.eye(LRU_BLOCKS, dtype=w.dtype)
    return jnp.einsum('lnij,nm->lnimj', w, eye).reshape(w.shape[0], MIX_W, MIX_W)


def _rows_table(rows, n_rows):
    tab = jnp.stack(rows, axis=1)
    return jnp.pad(tab, ((0, 0), (0, n_rows - len(rows)), (0, 0)))


def _prep_params(p):
    depth = p['rwkv_mix'].shape[0]
    mix = p['rwkv_mix']
    pvec = _rows_table([mix[:, 0:MIX_W], mix[:, MIX_W:2 * MIX_W], mix[:, 2 * MIX_W:3 * MIX_W],
                        p['rwkv_w0'], p['rwkv_a0'], p['rwkv_kk'], p['rwkv_ka'], p['rwkv_rk'].reshape(depth, MIX_W),
                        p['rwkv_lnx_g'], p['rwkv_lnx_b']], 16)
    mix_pad = _shift_to_padded(mix)
    mixl = _rows_table([mix_pad[:, 3 * MIX_W:3 * MIX_W + LANE], mix_pad[:, 3 * MIX_W + LANE:]], SUBLANE)
    lora_pad = ((0, 0), (0, LANE - LORA), (0, 0))
    lp = _rows_table([p['lru_conv_b'], p['lru_ba'], p['lru_bx'], p['lru_lambda']], SUBLANE)
    gpar = _rows_table([_pad_cols(p['gdn_a_log'], LANE), _pad_cols(p['gdn_dt_bias'], LANE), p['gdn_norm_g']], SUBLANE)
    ptab = jnp.stack([p['rwkv_rk'].reshape(depth, MIX_W), p['rwkv_lnx_g'], p['rwkv_lnx_b']], axis=1)
    ptab = jnp.broadcast_to(ptab[..., None], (depth, 3, MIX_W, LANE))
    return dict(pvec=pvec, mix_pad=mix_pad[:, None], mixl=mixl, ptab=ptab,
                w2p=jnp.pad(p['rwkv_w2'], lora_pad), a2p=jnp.pad(p['rwkv_a2'], lora_pad),
                lru_cw=p['lru_conv_w'], lp=lp,
                wa_d=_block_diag(p['lru_wa']).astype(BF16), wx_d=_block_diag(p['lru_wx']).astype(BF16),
                gdn_cw=p['gdn_conv_w'], gpar=gpar)


def _prompt_layer(x2d, nb, t, l, pp, zeros, mod_p, norm_g3, w_in_p, w_branch_b, w_out_b):
    u = _inproj(x2d, mod_p, norm_g3, w_in_p, l, rows_per_mod=t, tm=PROMPT_TM, tn=INPROJ_TN)
    ya, wkv = _rwkv_prompt(u, nb, t, pp['pvec'], pp['mixl'], pp['w2p'], pp['a2p'], zeros['shift'], zeros['wkv'], l)
    yb, h_last = _lru_prompt(u, nb, t, pp['lru_cw'], pp['lp'], pp['wa_d'], pp['wx_d'],
                             zeros['lru_conv'], zeros['lru_h'], l)
    yc, ssm = _gdn_prompt(u, nb, t, pp['gdn_cw'], pp['gpar'], zeros['gdn_conv'], zeros['ssm'], l)
    merged = _merge(ya, yb, yc, u, w_branch_b, l, tm=PROMPT_TM, tn=MERGE_TN)
    x_new = _outproj(merged, w_out_b, x2d, mod_p, l, rows_per_mod=t, tm=PROMPT_TM, tn=OUTPROJ_TN)
    u3 = u.reshape(nb, t, N_PAD)
    states = (_shift_from_u(u3[:, t - 1]), wkv,
              u3[:, t - (CONV_W - 1):, OFF_BX:OFF_BX + MIX_W], h_last.reshape(nb, MIX_W),
              u3[:, t - (CONV_W - 1):, OFF_Q:OFF_Q + QKV_W], ssm)
    return x_new, states


def _sample_layer(x2d, l, pp, st, acc, mod_s, norm_g3, w_in_p, w_branch_b, w_out_b):
    nb = x2d.shape[0]
    u = _inproj(x2d, mod_s, norm_g3, w_in_p, l, rows_per_mod=1, tm=nb, tn=INPROJ_TN)
    ya_t, wkv_acc = _rwkv_step(u, st['shift'], pp['mix_pad'], pp['pvec'], pp['w2p'], pp['a2p'], pp['ptab'],
                               st['wkv_t'], acc[0], l)
    ya = ya_t.T
    yb, h_new = _lru_step(u, st['lru_conv_t'], st['lru_h'], pp['lru_cw'], pp['lp'], pp['wa_d'], pp['wx_d'], l)
    yc, ssm_acc = _gdn_step(u, st['gdn_conv_t'], pp['gdn_cw'], pp['gpar'], st['ssm'], acc[1], l)
    merged = _merge(ya, yb, yc, u, w_branch_b, l, tm=nb, tn=OUTPROJ_TN)
    x_new = _outproj(merged, w_out_b, x2d, mod_s, l, rows_per_mod=1, tm=nb, tn=OUTPROJ_TN)
    rows = (_shift_from_u(u), u[:, OFF_BX:OFF_BX + MIX_W], h_new, u[:, OFF_Q:OFF_Q + QKV_W])
    return x_new, rows, (wkv_acc, ssm_acc)


def kernel(x_prompt, x_sample, state_rwkv_shift, state_rwkv_wkv, state_lru_conv, state_lru_h, state_gdn_conv, state_gdn_ssm, c_prompt, c_sample, ada_w, ada_b, norm_g, w_in, rwkv_mix, rwkv_w0, rwkv_w2, rwkv_a0, rwkv_a2, rwkv_kk, rwkv_ka, rwkv_rk, rwkv_lnx_g, rwkv_lnx_b, lru_conv_w, lru_conv_b, lru_wa, lru_ba, lru_wx, lru_bx, lru_lambda, gdn_conv_w, gdn_a_log, gdn_dt_bias, gdn_norm_g, w_branch, w_out, final_g):
    p = dict(rwkv_mix=rwkv_mix, rwkv_w0=rwkv_w0, rwkv_w2=rwkv_w2, rwkv_a0=rwkv_a0, rwkv_a2=rwkv_a2,
             rwkv_kk=rwkv_kk, rwkv_ka=rwkv_ka, rwkv_rk=rwkv_rk, rwkv_lnx_g=rwkv_lnx_g, rwkv_lnx_b=rwkv_lnx_b,
             lru_conv_w=lru_conv_w, lru_conv_b=lru_conv_b, lru_wa=lru_wa, lru_ba=lru_ba, lru_wx=lru_wx,
             lru_bx=lru_bx, lru_lambda=lru_lambda, gdn_conv_w=gdn_conv_w, gdn_a_log=gdn_a_log,
             gdn_dt_bias=gdn_dt_bias, gdn_norm_g=gdn_norm_g)
    depth = w_in.shape[0]
    bp, t, _ = x_prompt.shape
    bs = x_sample.shape[0]

    w_in_p = jnp.swapaxes(_layout_in_cols(w_in).astype(BF16), 1, 2)
    w_branch_b = w_branch.astype(BF16)
    w_out_b = w_out.astype(BF16)
    norm_g3 = norm_g.reshape(depth, 1, D_MODEL)

    rows = bp + bs
    rows_pad = -(-rows // SUBLANE) * SUBLANE
    c_all = jnp.pad(jnp.concatenate([c_prompt, c_sample], axis=0), ((0, rows_pad - rows), (0, 0)))
    mod = _ada_all(c_all, ada_w, ada_b)
    mod_p = mod[:, 0:bp].reshape(depth, bp, 1, 3 * D_MODEL)
    mod_s = mod[:, bp:bp + bs]

    pp = _prep_params(p)
    zeros = dict(shift=jnp.zeros((bp, SUBLANE, SHIFT_PAD), F32), wkv=jnp.zeros((bp, H_A, HEAD_A, HEAD_A), F32),
                 lru_conv=jnp.zeros((bp, SUBLANE, MIX_W), F32), lru_h=jnp.zeros((bp, 1, MIX_W), F32),
                 gdn_conv=jnp.zeros((bp, SUBLANE, QKV_W), F32), ssm=jnp.zeros((bp, H_C, DK, DV), F32))
    wkv_t = jnp.transpose(state_rwkv_wkv, (0, 2, 3, 4, 1))
    st = dict(shift=_shift_to_padded(state_rwkv_shift), wkv_t=wkv_t,
              lru_conv_t=jnp.swapaxes(state_lru_conv, 1, 2), lru_h=state_lru_h,
              gdn_conv_t=jnp.swapaxes(state_gdn_conv, 1, 2), ssm=state_gdn_ssm)
    acc = (jnp.zeros(wkv_t.shape, F32), jnp.zeros(state_gdn_ssm.shape, F32))

    xp = x_prompt.reshape(bp * t, D_MODEL)
    xs = x_sample.reshape(bs, D_MODEL)
    p_states, s_rows = [], []
    for l in range(depth):
        xp, pst = _prompt_layer(xp, bp, t, l, pp, zeros, mod_p, norm_g3, w_in_p, w_branch_b, w_out_b)
        p_states.append(pst)
        xs, rows_l, acc = _sample_layer(xs, l, pp, st, acc, mod_s, norm_g3, w_in_p, w_branch_b, w_out_b)
        s_rows.append(rows_l)

    y_prompt = _final_norm(xp, final_g, tm=512).reshape(bp, t, D_MODEL)
    y_sample = _final_norm(xs, final_g, tm=bs).reshape(bs, 1, D_MODEL)
    p_out = tuple(jnp.stack([s[i] for s in p_states]) for i in range(6))
    s_shift, s_bx, s_h, s_qkv = (jnp.stack([r[i] for r in s_rows]) for i in range(4))
    s_lru_conv = jnp.concatenate([state_lru_conv[:, :, 1:], s_bx[:, :, None]], axis=2)
    s_gdn_conv = jnp.concatenate([state_gdn_conv[:, :, 1:], s_qkv[:, :, None]], axis=2)
    s_wkv = jnp.transpose(acc[0], (0, 4, 1, 2, 3))
    return (y_prompt, y_sample) + p_out + (s_shift, s_wkv, s_lru_conv, s_h, s_gdn_conv, acc[1])
```

```python
import functools
import math

import jax
import jax.numpy as jnp
import numpy as np
from jax import lax
from jax.experimental import pallas as pl
from jax.experimental.pallas import tpu as pltpu

F32 = jnp.float32
BF16 = jnp.bfloat16
HIGHEST = lax.Precision.HIGHEST

D_MODEL = 2048
DEPTH = 4
MIX_W = 1024
HEAD_A = 64
H_A = 16
LORA = 96
GN_EPS_A = 64e-5
A_SHIFT_W = 3 * MIX_W + 2 * LORA
LRU_BLOCKS = 16
LRU_BS = 64
LRU_C = 8.0
CONV_W = 4
DK = 128
DV = 128
H_C = 8
QKV_W = 3 * MIX_W
NORM_EPS = 1e-6
IN_COLS = 16592

LANE = 128
SUBLANE = 8

OFF_R, OFF_K, OFF_V = 0, 1024, 2048
OFF_AG = 3072
OFF_BX, OFF_BG = 4096, 5120
OFF_Q = 6144
OFF_Z = 9216
OFF_MERGE = 10240
OFF_WD, OFF_AD, OFF_AB = 16384, 16512, 16640
N_PAD = 16896
SHIFT_PAD = 3 * MIX_W + 2 * LANE

RWKV_CHUNK = 64
GDN_CHUNK = 128
RWKV_LEAD = 4
GDN_LEAD = 4
VMEM_LIMIT = 56 * 1024 * 1024

PROMPT_TM = 1024
INPROJ_TN = 1536
MERGEOUT_TM = 256


def _cparams(sem):
    return pltpu.CompilerParams(dimension_semantics=sem, vmem_limit_bytes=VMEM_LIMIT)


def _sigmoid(x):
    return 1.0 / (1.0 + jnp.exp(-x))


def _silu(x):
    return x * _sigmoid(x)


def _softplus(x):
    return jnp.maximum(x, 0.0) + jnp.log1p(jnp.exp(-jnp.abs(x)))


_NN = (((1,), (0,)), ((), ()))
_NT = (((1,), (1,)), ((), ()))
_TN = (((0,), (0,)), ((), ()))

P_LORA = "bf16"
P_CHUNK = "bf16"
P_INV = "bf16"
P_STATE = "bf16"
P_STEP = "f32"


def _dg(a, b, dims):
    return lax.dot_general(a, b, dims, preferred_element_type=F32)


def _dot(a, b, dims=_NN, mode="bf16"):
    if mode == "f32":
        return lax.dot_general(a, b, dims, preferred_element_type=F32, precision=HIGHEST)
    ah, bh = a.astype(BF16), b.astype(BF16)
    if mode == "bf16":
        return _dg(ah, bh, dims)
    al = (a - ah.astype(F32)).astype(BF16)
    bl = (b - bh.astype(F32)).astype(BF16)
    return _dg(ah, bh, dims) + (_dg(al, bh, dims) + _dg(ah, bl, dims))


def _dot_exact_lhs(a01, x):
    a = a01
    x1 = x.astype(BF16)
    r1 = x - x1.astype(F32)
    x2 = r1.astype(BF16)
    x3 = (r1 - x2.astype(F32)).astype(BF16)
    return _dg(a, x1, _NN) + (_dg(a, x2, _NN) + _dg(a, x3, _NN))


def _mm_bf16(a, b):
    return jnp.dot(a.astype(BF16), b.astype(BF16), preferred_element_type=F32)


def _iota2(n, m):
    return (lax.broadcasted_iota(jnp.int32, (n, m), 0), lax.broadcasted_iota(jnp.int32, (n, m), 1))


def _inv_unit_lower(mats, n, levels):
    mm = functools.partial(_dot, mode=P_INV)
    row, col = _iota2(n, n)
    eye = (row == col).astype(F32)
    same = (row >> 4) == (col >> 4)
    d = [jnp.where(same, a, 0.0) for a in mats]
    o = [a - di for a, di in zip(mats, d)]
    x1 = [-di for di in d]
    x2 = [mm(x, x) for x in x1]
    yield
    td = [mm(eye + a, eye + b) for a, b in zip(x1, x2)]
    x4 = [mm(x, x) for x in x2]
    yield
    td = [mm(t, eye + x) for t, x in zip(td, x4)]
    x8 = [mm(x, x) for x in x4]
    yield
    td = [mm(t, eye + x) for t, x in zip(td, x8)]
    yield
    p = [-mm(t, oi) for t, oi in zip(td, o)]
    t = [eye + pi for pi in p]
    for _ in range(levels - 1):
        yield
        p = [mm(pi, pi) for pi in p]
        yield
        t = [mm(ti, eye + pi) for ti, pi in zip(t, p)]
    yield
    return [mm(ti, tdi) for ti, tdi in zip(t, td)]


def _interleave(gens, lead):
    done = [False] * len(gens)
    step = 0
    while not all(done):
        for i, g in enumerate(gens):
            if not done[i] and step >= i * lead:
                try:
                    next(g)
                except StopIteration:
                    done[i] = True
        step += 1


def _ada_kernel(c_ref, w_ref, b_ref, o_ref):
    h = _silu(c_ref[...]).astype(BF16)
    o_ref[...] = jnp.dot(h, w_ref[...].astype(BF16), preferred_element_type=F32) + b_ref[...]


def _ada_all(c_all, ada_w, ada_b, tn=512):
    rows = c_all.shape[0]
    depth = ada_w.shape[0]
    n = ada_w.shape[2]
    return pl.pallas_call(
        _ada_kernel,
        out_shape=jax.ShapeDtypeStruct((depth, rows, n), F32),
        grid=(depth, n // tn),
        in_specs=[pl.BlockSpec((rows, D_MODEL), lambda l, j: (0, 0)),
                  pl.BlockSpec((None, D_MODEL, tn), lambda l, j: (l, 0, j)),
                  pl.BlockSpec((None, 1, tn), lambda l, j: (l, 0, j))],
        out_specs=pl.BlockSpec((None, rows, tn), lambda l, j: (l, 0, j)),
        compiler_params=_cparams(("parallel", "parallel")),
        name="ada_mod",
    )(c_all, ada_w, ada_b.reshape(depth, 1, n))


def _inproj_kernel(x_ref, shift_ref, scale_ref, g_ref, w_ref, o_ref, h_ref):
    @pl.when(pl.program_id(1) == 0)
    def _():
        x = x_ref[...]
        y = x * lax.rsqrt(jnp.mean(x * x, axis=-1, keepdims=True) + NORM_EPS) * g_ref[...]
        h_ref[...] = (y * (1.0 + scale_ref[...]) + shift_ref[...]).astype(BF16)

    o_ref[...] = _dg(h_ref[...], w_ref[...], _NT)


def _inproj(x2d, mod, norm_g3, w_in_p, l, rows_per_mod, tm, tn):
    m = x2d.shape[0]
    n = w_in_p.shape[1]
    if rows_per_mod > 1:
        def mspec(part):
            return pl.BlockSpec((None, None, 1, D_MODEL), lambda i, j: (l, (i * tm) // rows_per_mod, 0, part))
    else:
        def mspec(part):
            return pl.BlockSpec((None, tm, D_MODEL), lambda i, j: (l, i, part))
    return pl.pallas_call(
        _inproj_kernel,
        out_shape=jax.ShapeDtypeStruct((m, n), F32),
        grid=(m // tm, n // tn),
        in_specs=[pl.BlockSpec((tm, D_MODEL), lambda i, j: (i, 0)),
                  mspec(0), mspec(1),
                  pl.BlockSpec((None, 1, D_MODEL), lambda i, j: (l, 0, 0)),
                  pl.BlockSpec((None, tn, D_MODEL), lambda i, j: (l, j, 0))],
        out_specs=pl.BlockSpec((tm, tn), lambda i, j: (i, j)),
        scratch_shapes=[pltpu.VMEM((tm, D_MODEL), BF16)],
        compiler_params=_cparams(("parallel", "arbitrary")),
        name="inproj",
    )(x2d, mod, mod, norm_g3, w_in_p)


def _mergeout_kernel(ya_ref, yb_ref, yc_ref, ga_ref, gb_ref, gc_ref, wa_ref, wb_ref, wc_ref, wo_ref,
                     x_ref, gate_ref, o_ref):
    acc = _sigmoid(ga_ref[...]) * jnp.dot(ya_ref[...].astype(BF16), wa_ref[...], preferred_element_type=F32)
    acc += _sigmoid(gb_ref[...]) * jnp.dot(yb_ref[...].astype(BF16), wb_ref[...], preferred_element_type=F32)
    acc += _sigmoid(gc_ref[...]) * jnp.dot(yc_ref[...].astype(BF16), wc_ref[...], preferred_element_type=F32)
    proj = jnp.dot(acc.astype(BF16), wo_ref[...], preferred_element_type=F32)
    o_ref[...] = x_ref[...] + gate_ref[...] * proj


def _mergeout(ya, yb, yc, u, w_branch_b, w_out_b, x2d, mod, l, rows_per_mod, tm):
    m = x2d.shape[0]
    yspec = pl.BlockSpec((tm, MIX_W), lambda i: (i, 0))
    resident = pl.Buffered(1)

    def gspec(br):
        return pl.BlockSpec((tm, D_MODEL), lambda i: (i, OFF_MERGE // D_MODEL + br))

    def wspec(br):
        return pl.BlockSpec((None, None, MIX_W, D_MODEL), lambda i: (l, br, 0, 0), pipeline_mode=resident)

    if rows_per_mod > 1:
        mspec = pl.BlockSpec((None, None, 1, D_MODEL), lambda i: (l, (i * tm) // rows_per_mod, 0, 2))
    else:
        mspec = pl.BlockSpec((None, tm, D_MODEL), lambda i: (l, i, 2))
    return pl.pallas_call(
        _mergeout_kernel,
        out_shape=jax.ShapeDtypeStruct((m, D_MODEL), F32),
        grid=(m // tm,),
        in_specs=[yspec, yspec, yspec, gspec(0), gspec(1), gspec(2), wspec(0), wspec(1), wspec(2),
                  pl.BlockSpec((None, D_MODEL, D_MODEL), lambda i: (l, 0, 0), pipeline_mode=resident),
                  pl.BlockSpec((tm, D_MODEL), lambda i: (i, 0)),
                  mspec],
        out_specs=pl.BlockSpec((tm, D_MODEL), lambda i: (i, 0)),
        compiler_params=_cparams(("parallel",)),
        name="merge_outproj",
    )(ya, yb, yc, u, u, u, w_branch_b, w_branch_b, w_branch_b, w_out_b, x2d, mod)


def _final_norm_kernel(x_ref, g_ref, o_ref):
    x = x_ref[...]
    o_ref[...] = x * lax.rsqrt(jnp.mean(x * x, axis=-1, keepdims=True) + NORM_EPS) * g_ref[...]


def _final_norm(x2d, g, tm):
    m = x2d.shape[0]
    return pl.pallas_call(
        _final_norm_kernel,
        out_shape=jax.ShapeDtypeStruct((m, D_MODEL), F32),
        grid=(m // tm,),
        in_specs=[pl.BlockSpec((tm, D_MODEL), lambda i: (i, 0)), pl.BlockSpec((1, D_MODEL), lambda i: (0, 0))],
        out_specs=pl.BlockSpec((tm, D_MODEL), lambda i: (i, 0)),
        compiler_params=_cparams(("parallel",)),
        name="final_norm",
    )(x2d, g.reshape(1, D_MODEL))


PV_MIX_R, PV_MIX_K, PV_MIX_V, PV_W0, PV_A0, PV_KK, PV_KA, PV_RK, PV_LNG, PV_LNB = range(10)


def _half_sum(x, lo_mask):
    s_lo = jnp.sum(jnp.where(lo_mask, x, 0.0), axis=-1, keepdims=True)
    s_hi = jnp.sum(jnp.where(lo_mask, 0.0, x), axis=-1, keepdims=True)
    return jnp.where(lo_mask, s_lo, s_hi)


def _rwkv_prompt_kernel(r_ref, k_ref, v_ref, g_ref, wd_ref, ad_ref, pv_ref, mixl_ref, w2_ref, a2_ref,
                        shr_ref, shk_ref, shv_ref, shwd_ref, shad_ref, s0_ref,
                        y_ref, sout_ref, xbuf, sb, *, n_chunks, pairs, nck):
    c = RWKV_CHUNK
    gw = pairs * LANE
    ti = pl.program_id(2)
    lo = lax.broadcasted_iota(jnp.int32, (1, LANE), 1) < HEAD_A

    @pl.when(ti == 0)
    def _():
        xbuf[0:SUBLANE, :] = jnp.concatenate(
            [shr_ref[...], shk_ref[...], shv_ref[...], shwd_ref[...], shad_ref[...]], axis=1)
        zero = jnp.zeros((HEAD_A, HEAD_A), F32)
        for g in range(pairs):
            top = jnp.concatenate([s0_ref[2 * g], zero], axis=1)
            bot = jnp.concatenate([zero, s0_ref[2 * g + 1]], axis=1)
            sb[g] = jnp.concatenate([top, bot], axis=0)

    tc = nck * c
    cur = jnp.concatenate([r_ref[...], k_ref[...], v_ref[...], wd_ref[...], ad_ref[...]], axis=1)
    xbuf[SUBLANE:SUBLANE + tc, :] = cur
    prev = xbuf[pl.ds(SUBLANE - 1, tc), :]
    pv = pv_ref[...]
    mixl = mixl_ref[...]
    mix = jnp.concatenate([pv[PV_MIX_R:PV_MIX_R + 1], pv[PV_MIX_K:PV_MIX_K + 1], pv[PV_MIX_V:PV_MIX_V + 1],
                           mixl[0:1], mixl[1:2]], axis=1)
    xbuf[0:SUBLANE, :] = cur[tc - SUBLANE:tc, :]

    w0, a0 = pv[PV_W0:PV_W0 + 1], pv[PV_A0:PV_A0 + 1]
    k_k, k_a, r_k = pv[PV_KK:PV_KK + 1], pv[PV_KA:PV_KA + 1], pv[PV_RK:PV_RK + 1]
    ln_g, ln_b = pv[PV_LNG:PV_LNG + 1], pv[PV_LNB:PV_LNB + 1]
    rowc, colc = _iota2(c, c)
    lower_01 = (rowc >= colc).astype(BF16)

    n = 2 * c
    row, col = _iota2(n, n)
    same_head = (row >> 6) == (col >> 6)
    strict = same_head & (row > col)
    incl = same_head & (row >= col)
    eye = row == col
    mm = functools.partial(_dot, mode=P_CHUNK)

    def stack(x):
        return jnp.concatenate([jnp.where(lo, x, 0.0), jnp.where(lo, 0.0, x)], axis=0)

    gs = range(pairs)
    sls = [slice(g * LANE, (g + 1) * LANE) for g in gs]
    state = {"s": [sb[g] for g in gs]}
    out_rows = [None] * nck

    def chunk_stages(k):
        rows = slice(k * c, (k + 1) * c)
        cur_k = cur[rows]
        z = cur_k + (prev[rows] - cur_k) * mix
        zr, zk, zv = z[:, 0:gw], z[:, gw:2 * gw], z[:, 2 * gw:3 * gw]
        zwd, zad = z[:, 3 * gw:3 * gw + LANE], z[:, 3 * gw + LANE:]
        logw = -_softplus(-(w0 + _dot(jnp.tanh(zwd), w2_ref[...], mode=P_LORA))) - 0.5
        lw = -jnp.exp(logw)
        a = _sigmoid(a0 + _dot(zad, a2_ref[...], mode=P_LORA))
        kkp = zk * k_k
        kmod = zk * (1.0 + (a - 1.0) * k_a)
        yield
        gcum = _dot_exact_lhs(lower_01, lw)
        g_end = gcum[c - 1:c, :]
        e_pos, e_neg = jnp.exp(gcum), jnp.exp(-gcum)
        e_prev, e_end = jnp.exp(gcum - lw), jnp.exp(g_end - gcum)
        eg_end = jnp.exp(g_end)
        rt, kb, kend = zr * e_pos, kmod * e_neg, kmod * e_end
        yield
        kk = [kkp[:, sl] * lax.rsqrt(_half_sum(kkp[:, sl] * kkp[:, sl], lo) + 1e-6) for sl in sls]
        bt = [kk[g] * a[:, sls[g]] for g in gs]
        rt_s = [stack(rt[:, sl]) for sl in sls]
        kb_s = [stack(kb[:, sl]) for sl in sls]
        bb_s = [stack(bt[g] * e_neg[:, sls[g]]) for g in gs]
        kt_s = [stack(kk[g] * e_prev[:, sls[g]]) for g in gs]
        yield
        kend_s = [stack(kend[:, sl]) for sl in sls]
        bend_s = [stack(bt[g] * e_end[:, sls[g]]) for g in gs]
        v_s = [stack(zv[:, sl]) for sl in sls]
        p_all = [mm(jnp.concatenate([kt_s[g], rt_s[g]], axis=0), jnp.concatenate([bb_s[g], kb_s[g]], axis=0), _NT)
                 for g in gs]
        yield
        a_ub = [jnp.where(strict, x[0:n, 0:n], 0.0) for x in p_all]
        a_vk = [jnp.where(strict, x[0:n, n:2 * n], 0.0) for x in p_all]
        p_rb = [jnp.where(incl, x[n:2 * n, 0:n], 0.0) for x in p_all]
        p_rk = [jnp.where(incl, x[n:2 * n, n:2 * n], 0.0) for x in p_all]
        av = [mm(a_vk[g], v_s[g]) for g in gs]
        pv_ = [mm(p_rk[g], v_s[g]) for g in gs]
        t_inv = yield from _inv_unit_lower(a_ub, n, levels=2)
        yield
        m12 = [mm(t_inv[g], jnp.concatenate([kt_s[g], av[g]], axis=1)) for g in gs]
        yield
        w2m = [mm(p_rb[g], m12[g]) for g in gs]
        q1 = [rt_s[g] - w2m[g][:, 0:LANE] for g in gs]
        q2 = [pv_[g] - w2m[g][:, LANE:2 * LANE] for g in gs]
        phi = [jnp.where(eye, eg_end[:, sls[g]], 0.0) - mm(bend_s[g], m12[g][:, 0:LANE], _TN) for g in gs]
        psi_t = [mm(jnp.concatenate([v_s[g], m12[g][:, LANE:2 * LANE]], axis=0),
                    jnp.concatenate([kend_s[g], -bend_s[g]], axis=0), _TN) for g in gs]
        yield
        s_old = state["s"]
        y_s = [_dot(q1[g], s_old[g], _NT, P_STATE) + q2[g] for g in gs]
        state["s"] = [_dot(s_old[g], phi[g], _NT, P_STATE) + psi_t[g] for g in gs]
        yield
        gate = _silu(g_ref[rows, :])
        rk_prod = zr * kmod * r_k
        outs = []
        for g in gs:
            sl = sls[g]
            y2 = y_s[g][0:c, :] + y_s[g][c:n, :]
            mu = _half_sum(y2, lo) * (1.0 / HEAD_A)
            dlt = y2 - mu
            var = _half_sum(dlt * dlt, lo) * (1.0 / HEAD_A)
            yn = dlt * lax.rsqrt(var + GN_EPS_A) * ln_g[:, sl] + ln_b[:, sl]
            bonus = _half_sum(rk_prod[:, sl], lo) * zv[:, sl]
            outs.append((yn + bonus) * gate[:, sl])
        out_rows[k] = jnp.concatenate(outs, axis=1)

    _interleave([chunk_stages(k) for k in range(nck)], lead=RWKV_LEAD)
    for g in gs:
        sb[g] = state["s"][g]
    y_ref[...] = jnp.concatenate(out_rows, axis=0).astype(y_ref.dtype)

    @pl.when(ti == n_chunks - 1)
    def _():
        for g in range(pairs):
            s_fin = sb[g]
            sout_ref[2 * g] = s_fin[0:HEAD_A, 0:HEAD_A]
            sout_ref[2 * g + 1] = s_fin[HEAD_A:2 * HEAD_A, HEAD_A:2 * HEAD_A]


def _rwkv_prompt(u, nb, t, pvec, mixl, w2p, a2p, sh_init, s0, l, pairs=8, nck=2):
    c = nck * RWKV_CHUNK
    nt = t // c
    gw = pairs * LANE
    groups = (H_A // 2) // pairs

    def uspec(off):
        return pl.BlockSpec((c, gw), lambda b, p, i: (b * nt + i, off // gw + p))

    def lspec(off):
        return pl.BlockSpec((c, LANE), lambda b, p, i: (b * nt + i, off // LANE))

    def shspec(off):
        return pl.BlockSpec((None, SUBLANE, gw), lambda b, p, i: (b, 0, off // gw + p))

    def shlspec(off):
        return pl.BlockSpec((None, SUBLANE, LANE), lambda b, p, i: (b, 0, off // LANE))

    group_cols = lambda rows: pl.BlockSpec((None, rows, gw), lambda b, p, i: (l, 0, p))
    return pl.pallas_call(
        functools.partial(_rwkv_prompt_kernel, n_chunks=nt, pairs=pairs, nck=nck),
        out_shape=(jax.ShapeDtypeStruct((nb * t, MIX_W), BF16),
                   jax.ShapeDtypeStruct((nb, H_A, HEAD_A, HEAD_A), F32)),
        grid=(nb, groups, nt),
        in_specs=[uspec(OFF_R), uspec(OFF_K), uspec(OFF_V), uspec(OFF_AG), lspec(OFF_WD), lspec(OFF_AD),
                  group_cols(16),
                  pl.BlockSpec((None, SUBLANE, LANE), lambda b, p, i: (l, 0, 0)),
                  group_cols(LANE), group_cols(LANE),
                  shspec(0), shspec(MIX_W), shspec(2 * MIX_W), shlspec(3 * MIX_W), shlspec(3 * MIX_W + LANE),
                  pl.BlockSpec((None, 2 * pairs, HEAD_A, HEAD_A), lambda b, p, i: (b, p, 0, 0))],
        out_specs=(pl.BlockSpec((c, gw), lambda b, p, i: (b * nt + i, p)),
                   pl.BlockSpec((None, 2 * pairs, HEAD_A, HEAD_A), lambda b, p, i: (b, p, 0, 0))),
        scratch_shapes=[pltpu.VMEM((c + SUBLANE, 3 * gw + 2 * LANE), F32),
                        pltpu.VMEM((pairs, 2 * HEAD_A, 2 * HEAD_A), F32)],
        compiler_params=_cparams(("parallel", "parallel", "arbitrary")),
        name="rwkv_prompt",
    )(u, u, u, u, u, u, pvec, mixl, w2p, a2p, sh_init, sh_init, sh_init, sh_init, sh_init, s0)


TB_R, TB_W, TB_KKP, TB_A, TB_KM, TB_V, TB_GATE = range(7)
PT_RK, PT_LNG, PT_LNB = range(3)


def _rwkv_step_kernel(rkv_ref, g_ref, wd_ref, ad_ref, prev_ref, mix_ref, pv_ref, w2_ref, a2_ref, pt_ref,
                      s_ref, acc_ref, y_ref, sout_ref, tab, ybuf):
    del acc_ref
    h = pl.program_id(0)

    @pl.when(h == 0)
    def _():
        cur = jnp.concatenate([rkv_ref[...], wd_ref[...], ad_ref[...]], axis=1)
        z = cur + (prev_ref[...] - cur) * mix_ref[...]
        zr, zk, zv = z[:, 0:MIX_W], z[:, MIX_W:2 * MIX_W], z[:, 2 * MIX_W:3 * MIX_W]
        zwd, zad = z[:, 3 * MIX_W:3 * MIX_W + LANE], z[:, 3 * MIX_W + LANE:]
        pv = pv_ref[...]
        logw = -_softplus(-(pv[PV_W0:PV_W0 + 1] + _dot(jnp.tanh(zwd), w2_ref[...], mode=P_STEP))) - 0.5
        a = _sigmoid(pv[PV_A0:PV_A0 + 1] + _dot(zad, a2_ref[...], mode=P_STEP))
        tab[TB_R] = zr.T
        tab[TB_W] = jnp.exp(-jnp.exp(logw)).T
        tab[TB_KKP] = (zk * pv[PV_KK:PV_KK + 1]).T
        tab[TB_A] = a.T
        tab[TB_KM] = (zk * (1.0 + (a - 1.0) * pv[PV_KA:PV_KA + 1])).T
        tab[TB_V] = zv.T
        tab[TB_GATE] = _silu(g_ref[...]).T

    rows = pl.ds(pl.multiple_of(h * HEAD_A, HEAD_A), HEAD_A)
    r_t, w_t, km_t = tab[TB_R, rows, :], tab[TB_W, rows, :], tab[TB_KM, rows, :]
    kkp_t = tab[TB_KKP, rows, :]
    kk_t = kkp_t * lax.rsqrt(jnp.sum(kkp_t * kkp_t, axis=0, keepdims=True) + 1e-6)
    bt_t = kk_t * tab[TB_A, rows, :]
    v_t = tab[TB_V, rows, :]
    for v in range(HEAD_A):
        s = s_ref[v]
        sa = jnp.sum(s * kk_t, axis=0, keepdims=True)
        s_new = s * w_t - sa * bt_t + v_t[v:v + 1, :] * km_t
        sout_ref[v] = s_new
        ybuf[v:v + 1, :] = jnp.sum(s_new * r_t, axis=0, keepdims=True)
    y = ybuf[...]
    mu = jnp.mean(y, axis=0, keepdims=True)
    dlt = y - mu
    var = jnp.mean(dlt * dlt, axis=0, keepdims=True)
    yn = dlt * lax.rsqrt(var + GN_EPS_A) * pt_ref[PT_LNG] + pt_ref[PT_LNB]
    bonus = jnp.sum(r_t * km_t * pt_ref[PT_RK], axis=0, keepdims=True) * v_t
    y_ref[...] = (yn + bonus) * tab[TB_GATE, rows, :]


def _rwkv_step(u, prev_pad, mix_pad, pvec, w2p, a2p, ptab, s_all, s_acc, l):
    nb = u.shape[0]
    lfull = lambda shape: pl.BlockSpec((None,) + shape, lambda h: (l,) + tuple(0 for _ in shape))
    st_spec = pl.BlockSpec((None, None, HEAD_A, HEAD_A, nb), lambda h: (l, h, 0, 0, 0))
    return pl.pallas_call(
        _rwkv_step_kernel,
        out_shape=(jax.ShapeDtypeStruct((MIX_W, nb), F32),
                   jax.ShapeDtypeStruct(s_acc.shape, F32)),
        grid=(H_A,),
        in_specs=[pl.BlockSpec((nb, 3 * MIX_W), lambda h: (0, 0)),
                  pl.BlockSpec((nb, MIX_W), lambda h: (0, OFF_AG // MIX_W)),
                  pl.BlockSpec((nb, LANE), lambda h: (0, OFF_WD // LANE)),
                  pl.BlockSpec((nb, LANE), lambda h: (0, OFF_AD // LANE)),
                  lfull((nb, SHIFT_PAD)),
                  lfull((1, SHIFT_PAD)), lfull((16, MIX_W)), lfull((LANE, MIX_W)), lfull((LANE, MIX_W)),
                  pl.BlockSpec((None, 3, HEAD_A, LANE), lambda h: (l, 0, h, 0)),
                  st_spec,
                  pl.BlockSpec(memory_space=pl.ANY)],
        out_specs=(pl.BlockSpec((HEAD_A, nb), lambda h: (h, 0)), st_spec),
        scratch_shapes=[pltpu.VMEM((7, MIX_W, nb), F32), pltpu.VMEM((HEAD_A, nb), F32)],
        input_output_aliases={11: 1},
        compiler_params=_cparams(("arbitrary",)),
        name="rwkv_step",
    )(u, u, u, u, prev_pad, mix_pad, pvec, w2p, a2p, ptab, s_all, s_acc)


LP_CONVB, LP_BA, LP_BX, LP_LAM = range(4)


def _lru_gates(xc, wa_ref, wx_ref, lp):
    r = _sigmoid(_mm_bf16(xc, wa_ref[...]) + lp[LP_BA:LP_BA + 1])
    i = _sigmoid(_mm_bf16(xc, wx_ref[...]) + lp[LP_BX:LP_BX + 1])
    log_a = -LRU_C * r * _softplus(-lp[LP_LAM:LP_LAM + 1])
    a = jnp.exp(log_a)
    b = jnp.sqrt(jnp.tanh(-log_a) * (a * a + 1.0)) * (i * xc)
    return a, b


def _lru_prompt_kernel(x_ref, g_ref, cw_ref, lp_ref, wa_ref, wx_ref, cinit_ref, h0_ref,
                       y_ref, hout_ref, xbuf, hcar, *, n_blocks, tc):
    ti = pl.program_id(1)

    @pl.when(ti == 0)
    def _():
        xbuf[0:SUBLANE, :] = cinit_ref[...]
        hcar[...] = h0_ref[...]

    x = x_ref[...]
    xbuf[SUBLANE:SUBLANE + tc, :] = x
    cw = cw_ref[...]
    lp = lp_ref[...]
    xc = (xbuf[pl.ds(SUBLANE - 3, tc), :] * cw[0:1] + xbuf[pl.ds(SUBLANE - 2, tc), :] * cw[1:2]
          + xbuf[pl.ds(SUBLANE - 1, tc), :] * cw[2:3] + x * cw[3:4] + lp[LP_CONVB:LP_CONVB + 1])
    xbuf[0:SUBLANE, :] = x[tc - SUBLANE:tc, :]
    a, b = _lru_gates(xc, wa_ref, wx_ref, lp)
    n_tiles = tc // SUBLANE
    a = a.reshape(n_tiles, SUBLANE, MIX_W)
    b = b.reshape(n_tiles, SUBLANE, MIX_W)
    sub = lax.broadcasted_iota(jnp.int32, (n_tiles, SUBLANE, MIX_W), 1)
    s = 1
    while s < SUBLANE:
        a_sh = pltpu.roll(a, s, 1)
        b_sh = pltpu.roll(b, s, 1)
        m = sub >= s
        b = jnp.where(m, a * b_sh + b, b)
        a = jnp.where(m, a * a_sh, a)
        s *= 2
    carry = hcar[...]
    tiles = []
    for j in range(n_tiles):
        h_j = b[j] + a[j] * carry
        carry = h_j[SUBLANE - 1:SUBLANE, :]
        tiles.append(h_j)
    h = jnp.concatenate(tiles, axis=0)
    hcar[...] = carry
    y_ref[...] = (h * _silu(g_ref[...])).astype(y_ref.dtype)

    @pl.when(ti == n_blocks - 1)
    def _():
        hout_ref[...] = h[tc - 1:tc, :]


def _lru_prompt(u, nb, t, conv_w, lp, wa_d, wx_d, cinit, h0, l, tc=256):
    nt = t // tc
    const = lambda shape: pl.BlockSpec((None,) + shape, lambda b, i: (l,) + tuple(0 for _ in shape))
    return pl.pallas_call(
        functools.partial(_lru_prompt_kernel, n_blocks=nt, tc=tc),
        out_shape=(jax.ShapeDtypeStruct((nb * t, MIX_W), BF16),
                   jax.ShapeDtypeStruct((nb, 1, MIX_W), F32)),
        grid=(nb, nt),
        in_specs=[pl.BlockSpec((tc, MIX_W), lambda b, i: (b * nt + i, OFF_BX // MIX_W)),
                  pl.BlockSpec((tc, MIX_W), lambda b, i: (b * nt + i, OFF_BG // MIX_W)),
                  const((CONV_W, MIX_W)), const((SUBLANE, MIX_W)),
                  const((MIX_W, MIX_W)), const((MIX_W, MIX_W)),
                  pl.BlockSpec((None, SUBLANE, MIX_W), lambda b, i: (b, 0, 0)),
                  pl.BlockSpec((None, 1, MIX_W), lambda b, i: (b, 0, 0))],
        out_specs=(pl.BlockSpec((tc, MIX_W), lambda b, i: (b * nt + i, 0)),
                   pl.BlockSpec((None, 1, MIX_W), lambda b, i: (b, 0, 0))),
        scratch_shapes=[pltpu.VMEM((tc + SUBLANE, MIX_W), F32), pltpu.VMEM((1, MIX_W), F32)],
        compiler_params=_cparams(("parallel", "arbitrary")),
        name="lru_prompt",
    )(u, u, conv_w, lp, wa_d, wx_d, cinit, h0)


def _lru_step_kernel(x_ref, g_ref, buf_ref, h0_ref, cw_ref, lp_ref, wa_ref, wx_ref, y_ref, hout_ref):
    x = x_ref[...]
    cw = cw_ref[...]
    lp = lp_ref[...]
    xc = buf_ref[0] * cw[0:1] + buf_ref[1] * cw[1:2] + buf_ref[2] * cw[2:3] + x * cw[3:4] + lp[LP_CONVB:LP_CONVB + 1]
    a, b = _lru_gates(xc, wa_ref, wx_ref, lp)
    h = a * h0_ref[...] + b
    hout_ref[...] = h
    y_ref[...] = h * _silu(g_ref[...])


def _lru_step(u, buf_t, h0, conv_w, lp, wa_d, wx_d, l):
    nb = u.shape[0]
    const = lambda shape: pl.BlockSpec((None,) + shape, lambda i: (l,) + tuple(0 for _ in shape))
    out = pl.BlockSpec((nb, MIX_W), lambda i: (0, 0))
    return pl.pallas_call(
        _lru_step_kernel,
        out_shape=(jax.ShapeDtypeStruct((nb, MIX_W), F32), jax.ShapeDtypeStruct((nb, MIX_W), F32)),
        grid=(1,),
        in_specs=[pl.BlockSpec((nb, MIX_W), lambda i: (0, OFF_BX // MIX_W)),
                  pl.BlockSpec((nb, MIX_W), lambda i: (0, OFF_BG // MIX_W)),
                  const((CONV_W - 1, nb, MIX_W)), const((nb, MIX_W)),
                  const((CONV_W, MIX_W)), const((SUBLANE, MIX_W)),
                  const((MIX_W, MIX_W)), const((MIX_W, MIX_W))],
        out_specs=(out, out),
        compiler_params=_cparams(("arbitrary",)),
        name="lru_step",
    )(u, u, buf_t, h0, conv_w, lp, wa_d, wx_d)


GP_ALOG, GP_DTB, GP_NORM = range(3)


def _gdn_prompt_kernel(q_ref, k_ref, v_ref, z_ref, ab_ref, cwq_ref, cwk_ref, cwv_ref, gp_ref,
                       ciq_ref, cik_ref, civ_ref, s0_ref, y_ref, sout_ref, xbuf, st, *, n_chunks, heads, nck):
    c = GDN_CHUNK
    tc = nck * c
    hw = heads * LANE
    h_base = pl.program_id(1) * heads
    ti = pl.program_id(2)

    @pl.when(ti == 0)
    def _():
        xbuf[0:SUBLANE, :] = jnp.concatenate([ciq_ref[...], cik_ref[...], civ_ref[...]], axis=1)
        st[...] = s0_ref[...]

    cur = jnp.concatenate([q_ref[...], k_ref[...], v_ref[...]], axis=1)
    xbuf[SUBLANE:SUBLANE + tc, :] = cur
    cw = jnp.concatenate([cwq_ref[...], cwk_ref[...], cwv_ref[...]], axis=1)

    gp = gp_ref[...]
    norm_g = gp[GP_NORM:GP_NORM + 1]
    lane_c = lax.broadcasted_iota(jnp.int32, (c, LANE), 1)
    row, col = _iota2(c, c)
    incl = row >= col
    strict = row > col
    eye = row == col
    incl_01 = incl.astype(BF16)
    mm = functools.partial(_dot, mode=P_CHUNK)
    hs_ = range(heads)
    state = {"s": [st[j] for j in hs_]}
    out_rows = [None] * nck

    def chunk_stages(k):
        base = SUBLANE + k * c
        x0 = cur[k * c:(k + 1) * c]
        conv = (xbuf[pl.ds(base - 3, c), :] * cw[0:1] + xbuf[pl.ds(base - 2, c), :] * cw[1:2]
                + xbuf[pl.ds(base - 1, c), :] * cw[2:3] + x0 * cw[3:4])
        act = _silu(conv)
        yield
        ab = ab_ref[k * c:(k + 1) * c, :]
        g_all = -jnp.exp(gp[GP_ALOG:GP_ALOG + 1]) * _softplus(ab + gp[GP_DTB:GP_DTB + 1])
        beta_all = _sigmoid(ab)
        qh = [act[:, j * DK:(j + 1) * DK] for j in hs_]
        kh = [act[:, hw + j * DK:hw + (j + 1) * DK] for j in hs_]
        vh = [act[:, 2 * hw + j * DV:2 * hw + (j + 1) * DV] for j in hs_]
        qn = [x * lax.rsqrt(jnp.sum(x * x, axis=-1, keepdims=True) + 1e-6) * (DK ** -0.5) for x in qh]
        kn = [x * lax.rsqrt(jnp.sum(x * x, axis=-1, keepdims=True) + 1e-6) for x in kh]
        beta = [jnp.sum(jnp.where(lane_c == h_base + j + H_C, beta_all, 0.0), axis=-1, keepdims=True) for j in hs_]
        yield
        gc_all = _dot_exact_lhs(incl_01, g_all)
        gc = [jnp.broadcast_to(jnp.sum(jnp.where(lane_c == h_base + j, gc_all, 0.0), axis=-1, keepdims=True),
                               (c, LANE)) for j in hs_]
        gc_row = [jnp.sum(jnp.where(eye, x, 0.0), axis=0, keepdims=True) for x in gc]
        decay = [jnp.where(incl, jnp.exp(jnp.where(incl, gc[j] - gc_row[j], 0.0)), 0.0) for j in hs_]
        g_last = [x[c - 1:c, :] for x in gc]
        eg = [jnp.exp(x) for x in gc]
        yield
        kb = [kn[j] * beta[j] for j in hs_]
        p_all = [mm(jnp.concatenate([kb[j], qn[j]], axis=0), kn[j], _NT) for j in hs_]
        yield
        a_mat = [jnp.where(strict, p_all[j][0:c] * decay[j], 0.0) for j in hs_]
        qk = [p_all[j][c:2 * c] * decay[j] for j in hs_]
        k_dec = [kn[j] * jnp.exp(g_last[j] - gc[j]) for j in hs_]
        rhs = [jnp.concatenate([kb[j] * eg[j], vh[j] * beta[j]], axis=1) for j in hs_]
        t_inv = yield from _inv_unit_lower(a_mat, c, levels=3)
        yield
        wu = [mm(t_inv[j], rhs[j]) for j in hs_]
        yield
        w3 = [mm(k_dec[j], wu[j], _TN) for j in hs_]
        w4 = [mm(qk[j], wu[j]) for j in hs_]
        phi = [jnp.where(eye, jnp.exp(g_last[j]), 0.0) - w3[j][:, 0:DK] for j in hs_]
        q1 = [qn[j] * eg[j] - w4[j][:, 0:DK] for j in hs_]
        yield
        s_old = state["s"]
        o = [_dot(q1[j], s_old[j], _NN, P_STATE) + w4[j][:, DK:] for j in hs_]
        state["s"] = [_dot(phi[j], s_old[j], _NN, P_STATE) + w3[j][:, DK:] for j in hs_]
        yield
        zg = _silu(z_ref[k * c:(k + 1) * c, :])
        outs = [o[j] * lax.rsqrt(jnp.mean(o[j] * o[j], axis=-1, keepdims=True) + NORM_EPS) * norm_g
                * zg[:, j * DV:(j + 1) * DV] for j in hs_]
        out_rows[k] = jnp.concatenate(outs, axis=1)

    _interleave([chunk_stages(k) for k in range(nck)], lead=GDN_LEAD)
    xbuf[0:SUBLANE, :] = cur[tc - SUBLANE:tc, :]
    for j in hs_:
        st[j] = state["s"][j]
    y_ref[...] = jnp.concatenate(out_rows, axis=0).astype(y_ref.dtype)

    @pl.when(ti == n_chunks - 1)
    def _():
        sout_ref[...] = st[...]


def _gdn_prompt(u, nb, t, conv_w, gpar, cinit, s0, l, heads=8, nck=2):
    assert GDN_CHUNK == LANE and DK == LANE and DV == LANE
    c = nck * GDN_CHUNK
    nt = t // c
    hw = heads * LANE

    def uspec(off):
        return pl.BlockSpec((c, hw), lambda b, h, i: (b * nt + i, off // hw + h))

    def cwspec(off):
        return pl.BlockSpec((None, CONV_W, hw), lambda b, h, i: (l, 0, off // hw + h))

    def cispec(off):
        return pl.BlockSpec((None, SUBLANE, hw), lambda b, h, i: (b, 0, off // hw + h))

    return pl.pallas_call(
        functools.partial(_gdn_prompt_kernel, n_chunks=nt, heads=heads, nck=nck),
        out_shape=(jax.ShapeDtypeStruct((nb * t, MIX_W), BF16),
                   jax.ShapeDtypeStruct((nb, H_C, DK, DV), F32)),
        grid=(nb, H_C // heads, nt),
        in_specs=[uspec(OFF_Q), uspec(OFF_Q + MIX_W), uspec(OFF_Q + 2 * MIX_W), uspec(OFF_Z),
                  pl.BlockSpec((c, LANE), lambda b, h, i: (b * nt + i, OFF_AB // LANE)),
                  cwspec(0), cwspec(MIX_W), cwspec(2 * MIX_W),
                  pl.BlockSpec((None, SUBLANE, LANE), lambda b, h, i: (l, 0, 0)),
                  cispec(0), cispec(MIX_W), cispec(2 * MIX_W),
                  pl.BlockSpec((None, heads, DK, DV), lambda b, h, i: (b, h, 0, 0))],
        out_specs=(pl.BlockSpec((c, hw), lambda b, h, i: (b * nt + i, h)),
                   pl.BlockSpec((None, heads, DK, DV), lambda b, h, i: (b, h, 0, 0))),
        scratch_shapes=[pltpu.VMEM((c + SUBLANE, 3 * hw), F32), pltpu.VMEM((heads, DK, DV), F32)],
        compiler_params=_cparams(("parallel", "parallel", "arbitrary")),
        name="gdn_prompt",
    )(u, u, u, u, u, conv_w, conv_w, conv_w, gpar, cinit, cinit, cinit, s0)


def _gdn_step_kernel(qkv_ref, z_ref, ab_ref, buf_ref, cw_ref, gp_ref, s_ref, acc_ref, y_ref, sout_ref, hs, *, bb):
    del acc_ref
    cw = cw_ref[...]
    cur = qkv_ref[...]
    act = _silu(buf_ref[0] * cw[0:1] + buf_ref[1] * cw[1:2] + buf_ref[2] * cw[2:3] + cur * cw[3:4])
    gp = gp_ref[...]
    ab = ab_ref[...]
    g_all = -jnp.exp(gp[GP_ALOG:GP_ALOG + 1]) * _softplus(ab + gp[GP_DTB:GP_DTB + 1])
    beta_all = _sigmoid(ab)
    zg = _silu(z_ref[...])
    hs_q, hs_k, hs_v, hs_eg, hs_beta, hs_zg, hs_qk = range(7)
    for h in range(H_C):
        qh = act[:, h * DK:(h + 1) * DK]
        kh = act[:, MIX_W + h * DK:MIX_W + (h + 1) * DK]
        qn = qh * lax.rsqrt(jnp.sum(qh * qh, axis=-1, keepdims=True) + 1e-6) * (DK ** -0.5)
        kn = kh * lax.rsqrt(jnp.sum(kh * kh, axis=-1, keepdims=True) + 1e-6)
        hs[hs_q, h] = qn
        hs[hs_k, h] = kn
        hs[hs_v, h] = act[:, 2 * MIX_W + h * DV:2 * MIX_W + (h + 1) * DV]
        hs[hs_eg, h] = jnp.broadcast_to(jnp.exp(g_all[:, h:h + 1]), (bb, LANE))
        hs[hs_beta, h] = jnp.broadcast_to(beta_all[:, H_C + h:H_C + h + 1], (bb, LANE))
        hs[hs_zg, h] = zg[:, h * DV:(h + 1) * DV]
        hs[hs_qk, h] = jnp.broadcast_to(jnp.sum(qn * kn, axis=-1, keepdims=True), (bb, LANE))

    row, col = _iota2(DK, DV)
    diag = row == col
    rowb = lax.broadcasted_iota(jnp.int32, (bb, DV), 0)
    norm_g = gp[GP_NORM:GP_NORM + 1]
    heads = range(H_C)

    def body(b, outs):
        def rows(q):
            return [hs[q, h, pl.ds(b, 1), :] for h in heads]

        q, k, v, eg, beta = rows(hs_q), rows(hs_k), rows(hs_v), rows(hs_eg), rows(hs_beta)
        s = [s_ref[b, h] for h in heads]
        kcol = [jnp.sum(jnp.where(diag, jnp.broadcast_to(x, (DK, DV)), 0.0), axis=-1, keepdims=True) for x in k]
        qcol = [jnp.sum(jnp.where(diag, jnp.broadcast_to(x, (DK, DV)), 0.0), axis=-1, keepdims=True) for x in q]
        ks = [jnp.sum(kcol[h] * s[h], axis=0, keepdims=True) for h in heads]
        qs = [jnp.sum(qcol[h] * s[h], axis=0, keepdims=True) for h in heads]
        v_new = [beta[h] * (v[h] - eg[h] * ks[h]) for h in heads]
        for h in heads:
            sout_ref[b, h] = s[h] * eg[h] + kcol[h] * v_new[h]
        qk = rows(hs_qk)
        o = [eg[h] * qs[h] + qk[h] * v_new[h] for h in heads]
        ms = [jnp.mean(x * x, axis=-1, keepdims=True) for x in o]
        zg_h = rows(hs_zg)
        res = [o[h] * lax.rsqrt(ms[h] + NORM_EPS) * norm_g * zg_h[h] for h in heads]
        return tuple(jnp.where(rowb == b, jnp.broadcast_to(res[h], (bb, DV)), outs[h]) for h in heads)

    outs = lax.fori_loop(0, bb, body, tuple(jnp.zeros((bb, DV), F32) for _ in heads))
    y_ref[...] = jnp.concatenate(outs, axis=1)


def _gdn_step(u, buf_t, conv_w, gpar, s_all, s_acc, l, bb=SUBLANE):
    nb = u.shape[0]
    lfull = lambda shape: pl.BlockSpec((None,) + shape, lambda i: (l,) + tuple(0 for _ in shape))
    st_spec = pl.BlockSpec((None, bb, H_C, DK, DV), lambda i: (l, i, 0, 0, 0))
    return pl.pallas_call(
        functools.partial(_gdn_step_kernel, bb=bb),
        out_shape=(jax.ShapeDtypeStruct((nb, MIX_W), F32),
                   jax.ShapeDtypeStruct(s_acc.shape, F32)),
        grid=(nb // bb,),
        in_specs=[pl.BlockSpec((bb, QKV_W), lambda i: (i, OFF_Q // QKV_W)),
                  pl.BlockSpec((bb, MIX_W), lambda i: (i, OFF_Z // MIX_W)),
                  pl.BlockSpec((bb, LANE), lambda i: (i, OFF_AB // LANE)),
                  pl.BlockSpec((None, CONV_W - 1, bb, QKV_W), lambda i: (l, 0, i, 0)),
                  lfull((CONV_W, QKV_W)), lfull((SUBLANE, LANE)),
                  st_spec,
                  pl.BlockSpec(memory_space=pl.ANY)],
        out_specs=(pl.BlockSpec((bb, MIX_W), lambda i: (i, 0)), st_spec),
        scratch_shapes=[pltpu.VMEM((7, H_C, bb, LANE), F32)],
        input_output_aliases={7: 1},
        compiler_params=_cparams(("parallel",)),
        name="gdn_step",
    )(u, u, u, buf_t, conv_w, gpar, s_all, s_acc)


def _pad_cols(w, width):
    pad = width - w.shape[-1]
    return jnp.pad(w, [(0, 0)] * (w.ndim - 1) + [(0, pad)])


def _layout_in_cols(w):
    a_sh_end = A_SHIFT_W
    o_ag = a_sh_end
    o_bx = o_ag + MIX_W
    o_bg = o_bx + MIX_W
    o_q = o_bg + MIX_W
    o_a = o_q + QKV_W
    o_z = o_a + 2 * H_C
    o_m = o_z + MIX_W
    parts = [w[..., 0:3 * MIX_W], w[..., o_ag:o_bx], w[..., o_bx:o_bg], w[..., o_bg:o_q], w[..., o_q:o_a],
             w[..., o_z:o_m], w[..., o_m:IN_COLS],
             _pad_cols(w[..., 3 * MIX_W:3 * MIX_W + LORA], LANE),
             _pad_cols(w[..., 3 * MIX_W + LORA:a_sh_end], LANE),
             _pad_cols(w[..., o_a:o_z], LANE)]
    used = sum(part.shape[-1] for part in parts)
    parts.append(jnp.zeros(w.shape[:-1] + (N_PAD - used,), w.dtype))
    return jnp.concatenate(parts, axis=-1)


def _shift_to_padded(s):
    return jnp.concatenate([s[..., 0:3 * MIX_W], _pad_cols(s[..., 3 * MIX_W:3 * MIX_W + LORA], LANE),
                            _pad_cols(s[..., 3 * MIX_W + LORA:], LANE)], axis=-1)


def _shift_from_u(u_rows):
    return jnp.concatenate([u_rows[..., 0:3 * MIX_W], u_rows[..., OFF_WD:OFF_WD + LORA],
                            u_rows[..., OFF_AD:OFF_AD + LORA]], axis=-1)


def _block_diag(w):
    eye = jnp.eye(LRU_BLOCKS, dtype=w.dtype)
    return jnp.einsum('lnij,nm->lnimj', w, eye).reshape(w.shape[0], MIX_W, MIX_W)


def _rows_table(rows, n_rows):
    tab = jnp.stack(rows, axis=1)
    return jnp.pad(tab, ((0, 0), (0, n_rows - len(rows)), (0, 0)))


def _prep_params(p):
    depth = p['rwkv_mix'].shape[0]
    mix = p['rwkv_mix']
    pvec = _rows_table([mix[:, 0:MIX_W], mix[:, MIX_W:2 * MIX_W], mix[:, 2 * MIX_W:3 * MIX_W],
                        p['rwkv_w0'], p['rwkv_a0'], p['rwkv_kk'], p['rwkv_ka'], p['rwkv_rk'].reshape(depth, MIX_W),
                        p['rwkv_lnx_g'], p['rwkv_lnx_b']], 16)
    mix_pad = _shift_to_padded(mix)
    mixl = _rows_table([mix_pad[:, 3 * MIX_W:3 * MIX_W + LANE], mix_pad[:, 3 * MIX_W + LANE:]], SUBLANE)
    lora_pad = ((0, 0), (0, LANE - LORA), (0, 0))
    lp = _rows_table([p['lru_conv_b'], p['lru_ba'], p['lru_bx'], p['lru_lambda']], SUBLANE)
    gpar = _rows_table([_pad_cols(p['gdn_a_log'], LANE), _pad_cols(p['gdn_dt_bias'], LANE), p['gdn_norm_g']], SUBLANE)
    ptab = jnp.stack([p['rwkv_rk'].reshape(depth, MIX_W), p['rwkv_lnx_g'], p['rwkv_lnx_b']], axis=1)
    ptab = jnp.broadcast_to(ptab[..., None], (depth, 3, MIX_W, LANE))
    return dict(pvec=pvec, mix_pad=mix_pad[:, None], mixl=mixl, ptab=ptab,
                w2p=jnp.pad(p['rwkv_w2'], lora_pad), a2p=jnp.pad(p['rwkv_a2'], lora_pad),
                lru_cw=p['lru_conv_w'], lp=lp,
                wa_d=_block_diag(p['lru_wa']).astype(BF16), wx_d=_block_diag(p['lru_wx']).astype(BF16),
                gdn_cw=p['gdn_conv_w'], gpar=gpar)


def _prompt_layer(x2d, nb, t, l, pp, zeros, mod_p, norm_g3, w_in_p, w_branch_b, w_out_b):
    u = _inproj(x2d, mod_p, norm_g3, w_in_p, l, rows_per_mod=t, tm=PROMPT_TM, tn=INPROJ_TN)
    ya, wkv = _rwkv_prompt(u, nb, t, pp['pvec'], pp['mixl'], pp['w2p'], pp['a2p'], zeros['shift'], zeros['wkv'], l)
    yb, h_last = _lru_prompt(u, nb, t, pp['lru_cw'], pp['lp'], pp['wa_d'], pp['wx_d'],
                             zeros['lru_conv'], zeros['lru_h'], l)
    yc, ssm = _gdn_prompt(u, nb, t, pp['gdn_cw'], pp['gpar'], zeros['gdn_conv'], zeros['ssm'], l)
    x_new = _mergeout(ya, yb, yc, u, w_branch_b, w_out_b, x2d, mod_p, l, rows_per_mod=t, tm=MERGEOUT_TM)
    u3 = u.reshape(nb, t, N_PAD)
    states = (_shift_from_u(u3[:, t - 1]), wkv,
              u3[:, t - (CONV_W - 1):, OFF_BX:OFF_BX + MIX_W], h_last.reshape(nb, MIX_W),
              u3[:, t - (CONV_W - 1):, OFF_Q:OFF_Q + QKV_W], ssm)
    return x_new, states


def _sample_layer(x2d, l, pp, st, acc, mod_s, norm_g3, w_in_p, w_branch_b, w_out_b):
    nb = x2d.shape[0]
    u = _inproj(x2d, mod_s, norm_g3, w_in_p, l, rows_per_mod=1, tm=nb, tn=INPROJ_TN)
    ya_t, wkv_acc = _rwkv_step(u, st['shift'], pp['mix_pad'], pp['pvec'], pp['w2p'], pp['a2p'], pp['ptab'],
                               st['wkv_t'], acc[0], l)
    ya = ya_t.T
    yb, h_new = _lru_step(u, st['lru_conv_t'], st['lru_h'], pp['lru_cw'], pp['lp'], pp['wa_d'], pp['wx_d'], l)
    yc, ssm_acc = _gdn_step(u, st['gdn_conv_t'], pp['gdn_cw'], pp['gpar'], st['ssm'], acc[1], l)
    x_new = _mergeout(ya, yb, yc, u, w_branch_b, w_out_b, x2d, mod_s, l, rows_per_mod=1, tm=nb)
    rows = (_shift_from_u(u), u[:, OFF_BX:OFF_BX + MIX_W], h_new, u[:, OFF_Q:OFF_Q + QKV_W])
    return x_new, rows, (wkv_acc, ssm_acc)


def kernel(x_prompt, x_sample, state_rwkv_shift, state_rwkv_wkv, state_lru_conv, state_lru_h, state_gdn_conv, state_gdn_ssm, c_prompt, c_sample, ada_w, ada_b, norm_g, w_in, rwkv_mix, rwkv_w0, rwkv_w2, rwkv_a0, rwkv_a2, rwkv_kk, rwkv_ka, rwkv_rk, rwkv_lnx_g, rwkv_lnx_b, lru_conv_w, lru_conv_b, lru_wa, lru_ba, lru_wx, lru_bx, lru_lambda, gdn_conv_w, gdn_a_log, gdn_dt_bias, gdn_norm_g, w_branch, w_out, final_g):
    p = dict(rwkv_mix=rwkv_mix, rwkv_w0=rwkv_w0, rwkv_w2=rwkv_w2, rwkv_a0=rwkv_a0, rwkv_a2=rwkv_a2,
             rwkv_kk=rwkv_kk, rwkv_ka=rwkv_ka, rwkv_rk=rwkv_rk, rwkv_lnx_g=rwkv_lnx_g, rwkv_lnx_b=rwkv_lnx_b,
             lru_conv_w=lru_conv_w, lru_conv_b=lru_conv_b, lru_wa=lru_wa, lru_ba=lru_ba, lru_wx=lru_wx,
             lru_bx=lru_bx, lru_lambda=lru_lambda, gdn_conv_w=gdn_conv_w, gdn_a_log=gdn_a_log,
             gdn_dt_bias=gdn_dt_bias, gdn_norm_g=gdn_norm_g)
    depth = w_in.shape[0]
    bp, t, _ = x_prompt.shape
    bs = x_sample.shape[0]

    w_in_p = jnp.swapaxes(_layout_in_cols(w_in).astype(BF16), 1, 2)
    w_branch_b = w_branch.astype(BF16)
    w_out_b = w_out.astype(BF16)
    norm_g3 = norm_g.reshape(depth, 1, D_MODEL)

    rows = bp + bs
    rows_pad = -(-rows // SUBLANE) * SUBLANE
    c_all = jnp.pad(jnp.concatenate([c_prompt, c_sample], axis=0), ((0, rows_pad - rows), (0, 0)))
    mod = _ada_all(c_all, ada_w, ada_b)
    mod_p = mod[:, 0:bp].reshape(depth, bp, 1, 3 * D_MODEL)
    mod_s = mod[:, bp:bp + bs]

    pp = _prep_params(p)
    zeros = dict(shift=jnp.zeros((bp, SUBLANE, SHIFT_PAD), F32), wkv=jnp.zeros((bp, H_A, HEAD_A, HEAD_A), F32),
                 lru_conv=jnp.zeros((bp, SUBLANE, MIX_W), F32), lru_h=jnp.zeros((bp, 1, MIX_W), F32),
                 gdn_conv=jnp.zeros((bp, SUBLANE, QKV_W), F32), ssm=jnp.zeros((bp, H_C, DK, DV), F32))
    wkv_t = jnp.transpose(state_rwkv_wkv, (0, 2, 3, 4, 1))
    st = dict(shift=_shift_to_padded(state_rwkv_shift), wkv_t=wkv_t,
              lru_conv_t=jnp.swapaxes(state_lru_conv, 1, 2), lru_h=state_lru_h,
              gdn_conv_t=jnp.swapaxes(state_gdn_conv, 1, 2), ssm=state_gdn_ssm)
    acc = (jnp.zeros(wkv_t.shape, F32), jnp.zeros(state_gdn_ssm.shape, F32))

    xp = x_prompt.reshape(bp * t, D_MODEL)
    xs = x_sample.reshape(bs, D_MODEL)
    p_states, s_rows = [], []
    for l in range(depth):
        xp, pst = _prompt_layer(xp, bp, t, l, pp, zeros, mod_p, norm_g3, w_in_p, w_branch_b, w_out_b)
        p_states.append(pst)
        xs, rows_l, acc = _sample_layer(xs, l, pp, st, acc, mod_s, norm_g3, w_in_p, w_branch_b, w_out_b)
        s_rows.append(rows_l)

    y_prompt = _final_norm(xp, final_g, tm=512).reshape(bp, t, D_MODEL)
    y_sample = _final_norm(xs, final_g, tm=bs).reshape(bs, 1, D_MODEL)
    p_out = tuple(jnp.stack([s[i] for s in p_states]) for i in range(6))
    s_shift, s_bx, s_h, s_qkv = (jnp.stack([r[i] for r in s_rows]) for i in range(4))
    s_lru_conv = jnp.concatenate([state_lru_conv[:, :, 1:], s_bx[:, :, None]], axis=2)
    s_gdn_conv = jnp.concatenate([state_gdn_conv[:, :, 1:], s_qkv[:, :, None]], axis=2)
    s_wkv = jnp.transpose(acc[0], (0, 4, 1, 2, 3))
    return (y_prompt, y_sample) + p_out + (s_shift, s_wkv, s_lru_conv, s_h, s_gdn_conv, acc[1])
```

```python
import functools
import math

import jax
import jax.numpy as jnp
import numpy as np
from jax import lax
from jax.experimental import pallas as pl
from jax.experimental.pallas import tpu as pltpu

F32 = jnp.float32
BF16 = jnp.bfloat16
HIGHEST = lax.Precision.HIGHEST

D_MODEL = 2048
DEPTH = 4
MIX_W = 1024
HEAD_A = 64
H_A = 16
LORA = 96
GN_EPS_A = 64e-5
A_SHIFT_W = 3 * MIX_W + 2 * LORA
LRU_BLOCKS = 16
LRU_BS = 64
LRU_C = 8.0
CONV_W = 4
DK = 128
DV = 128
H_C = 8
QKV_W = 3 * MIX_W
NORM_EPS = 1e-6
IN_COLS = 16592

LANE = 128
SUBLANE = 8

OFF_R, OFF_K, OFF_V = 0, 1024, 2048
OFF_AG = 3072
OFF_BX, OFF_BG = 4096, 5120
OFF_Q = 6144
OFF_Z = 9216
OFF_MERGE = 10240
OFF_WD, OFF_AD, OFF_AB = 16384, 16512, 16640
N_PAD = 16896
SHIFT_PAD = 3 * MIX_W + 2 * LANE

RWKV_CHUNK = 64
GDN_CHUNK = 128
RWKV_LEAD = 4
GDN_LEAD = 4
VMEM_LIMIT = 56 * 1024 * 1024

PROMPT_TM = 1024
INPROJ_TN = 1536
MERGEOUT_TM = 256


def _cparams(sem):
    return pltpu.CompilerParams(dimension_semantics=sem, vmem_limit_bytes=VMEM_LIMIT)


def _sigmoid(x):
    return 1.0 / (1.0 + jnp.exp(-x))


def _silu(x):
    return x * _sigmoid(x)


def _softplus(x):
    return jnp.maximum(x, 0.0) + jnp.log1p(jnp.exp(-jnp.abs(x)))


_NN = (((1,), (0,)), ((), ()))
_NT = (((1,), (1,)), ((), ()))
_TN = (((0,), (0,)), ((), ()))

P_LORA = "bf16"
P_CHUNK = "bf16"
P_INV = "bf16"
P_STATE = "bf16"
P_STEP = "f32"


def _dg(a, b, dims):
    return lax.dot_general(a, b, dims, preferred_element_type=F32)


def _dot(a, b, dims=_NN, mode="bf16"):
    if mode == "f32":
        return lax.dot_general(a, b, dims, preferred_element_type=F32, precision=HIGHEST)
    ah, bh = a.astype(BF16), b.astype(BF16)
    if mode == "bf16":
        return _dg(ah, bh, dims)
    al = (a - ah.astype(F32)).astype(BF16)
    bl = (b - bh.astype(F32)).astype(BF16)
    return _dg(ah, bh, dims) + (_dg(al, bh, dims) + _dg(ah, bl, dims))


def _dot_exact_lhs(a01, x):
    a = a01
    x1 = x.astype(BF16)
    r1 = x - x1.astype(F32)
    x2 = r1.astype(BF16)
    x3 = (r1 - x2.astype(F32)).astype(BF16)
    return _dg(a, x1, _NN) + (_dg(a, x2, _NN) + _dg(a, x3, _NN))


def _mm_bf16(a, b):
    return jnp.dot(a.astype(BF16), b.astype(BF16), preferred_element_type=F32)


def _iota2(n, m):
    return (lax.broadcasted_iota(jnp.int32, (n, m), 0), lax.broadcasted_iota(jnp.int32, (n, m), 1))


def _inv_unit_lower(mats, n, levels):
    mm = functools.partial(_dot, mode=P_INV)
    row, col = _iota2(n, n)
    eye = (row == col).astype(F32)
    same = (row >> 4) == (col >> 4)
    d = [jnp.where(same, a, 0.0) for a in mats]
    o = [a - di for a, di in zip(mats, d)]
    x1 = [-di for di in d]
    x2 = [mm(x, x) for x in x1]
    yield
    td = [mm(eye + a, eye + b) for a, b in zip(x1, x2)]
    x4 = [mm(x, x) for x in x2]
    yield
    td = [mm(t, eye + x) for t, x in zip(td, x4)]
    x8 = [mm(x, x) for x in x4]
    yield
    td = [mm(t, eye + x) for t, x in zip(td, x8)]
    yield
    p = [-mm(t, oi) for t, oi in zip(td, o)]
    t = [eye + pi for pi in p]
    for _ in range(levels - 1):
        yield
        p = [mm(pi, pi) for pi in p]
        yield
        t = [mm(ti, eye + pi) for ti, pi in zip(t, p)]
    yield
    return [mm(ti, tdi) for ti, tdi in zip(t, td)]


def _interleave(gens, lead):
    done = [False] * len(gens)
    step = 0
    while not all(done):
        for i, g in enumerate(gens):
            if not done[i] and step >= i * lead:
                try:
                    next(g)
                except StopIteration:
                    done[i] = True
        step += 1


def _ada_kernel(c_ref, w_ref, b_ref, o_ref):
    h = _silu(c_ref[...]).astype(BF16)
    o_ref[...] = jnp.dot(h, w_ref[...].astype(BF16), preferred_element_type=F32) + b_ref[...]


def _ada_all(c_all, ada_w, ada_b, tn=512):
    rows = c_all.shape[0]
    depth = ada_w.shape[0]
    n = ada_w.shape[2]
    return pl.pallas_call(
        _ada_kernel,
        out_shape=jax.ShapeDtypeStruct((depth, rows, n), F32),
        grid=(depth, n // tn),
        in_specs=[pl.BlockSpec((rows, D_MODEL), lambda l, j: (0, 0)),
                  pl.BlockSpec((None, D_MODEL, tn), lambda l, j: (l, 0, j)),
                  pl.BlockSpec((None, 1, tn), lambda l, j: (l, 0, j))],
        out_specs=pl.BlockSpec((None, rows, tn), lambda l, j: (l, 0, j)),
        compiler_params=_cparams(("parallel", "parallel")),
        name="ada_mod",
    )(c_all, ada_w, ada_b.reshape(depth, 1, n))


def _inproj_kernel(x_ref, shift_ref, scale_ref, g_ref, w_ref, o_ref, h_ref):
    @pl.when(pl.program_id(1) == 0)
    def _():
        x = x_ref[...]
        y = x * lax.rsqrt(jnp.mean(x * x, axis=-1, keepdims=True) + NORM_EPS) * g_ref[...]
        h_ref[...] = (y * (1.0 + scale_ref[...]) + shift_ref[...]).astype(BF16)

    o_ref[...] = _dg(h_ref[...], w_ref[...], _NT)


def _inproj(x2d, mod, norm_g3, w_in_p, l, rows_per_mod, tm, tn):
    m = x2d.shape[0]
    n = w_in_p.shape[1]
    if rows_per_mod > 1:
        def mspec(part):
            return pl.BlockSpec((None, None, 1, D_MODEL), lambda i, j: (l, (i * tm) // rows_per_mod, 0, part))
    else:
        def mspec(part):
            return pl.BlockSpec((None, tm, D_MODEL), lambda i, j: (l, i, part))
    return pl.pallas_call(
        _inproj_kernel,
        out_shape=jax.ShapeDtypeStruct((m, n), F32),
        grid=(m // tm, n // tn),
        in_specs=[pl.BlockSpec((tm, D_MODEL), lambda i, j: (i, 0)),
                  mspec(0), mspec(1),
                  pl.BlockSpec((None, 1, D_MODEL), lambda i, j: (l, 0, 0)),
                  pl.BlockSpec((None, tn, D_MODEL), lambda i, j: (l, j, 0))],
        out_specs=pl.BlockSpec((tm, tn), lambda i, j: (i, j)),
        scratch_shapes=[pltpu.VMEM((tm, D_MODEL), BF16)],
        compiler_params=_cparams(("parallel", "arbitrary")),
        name="inproj",
    )(x2d, mod, mod, norm_g3, w_in_p)


def _mergeout_kernel(ya_ref, yb_ref, yc_ref, ga_ref, gb_ref, gc_ref, wa_ref, wb_ref, wc_ref, wo_ref,
                     x_ref, gate_ref, fg_ref, o_ref, *, final_norm):
    acc = _sigmoid(ga_ref[...]) * jnp.dot(ya_ref[...].astype(BF16), wa_ref[...], preferred_element_type=F32)
    acc += _sigmoid(gb_ref[...]) * jnp.dot(yb_ref[...].astype(BF16), wb_ref[...], preferred_element_type=F32)
    acc += _sigmoid(gc_ref[...]) * jnp.dot(yc_ref[...].astype(BF16), wc_ref[...], preferred_element_type=F32)
    proj = jnp.dot(acc.astype(BF16), wo_ref[...], preferred_element_type=F32)
    x = x_ref[...] + gate_ref[...] * proj
    if final_norm:
        x = x * lax.rsqrt(jnp.mean(x * x, axis=-1, keepdims=True) + NORM_EPS) * fg_ref[...]
    o_ref[...] = x


def _mergeout(ya, yb, yc, u, w_branch_b, w_out_b, x2d, mod, final_g, l, rows_per_mod, tm, final_norm):
    m = x2d.shape[0]
    yspec = pl.BlockSpec((tm, MIX_W), lambda i: (i, 0))
    resident = pl.Buffered(1)

    def gspec(br):
        return pl.BlockSpec((tm, D_MODEL), lambda i: (i, OFF_MERGE // D_MODEL + br))

    def wspec(br):
        return pl.BlockSpec((None, None, MIX_W, D_MODEL), lambda i: (l, br, 0, 0), pipeline_mode=resident)

    if rows_per_mod > 1:
        mspec = pl.BlockSpec((None, None, 1, D_MODEL), lambda i: (l, (i * tm) // rows_per_mod, 0, 2))
    else:
        mspec = pl.BlockSpec((None, tm, D_MODEL), lambda i: (l, i, 2))
    return pl.pallas_call(
        functools.partial(_mergeout_kernel, final_norm=final_norm),
        out_shape=jax.ShapeDtypeStruct((m, D_MODEL), F32),
        grid=(m // tm,),
        in_specs=[yspec, yspec, yspec, gspec(0), gspec(1), gspec(2), wspec(0), wspec(1), wspec(2),
                  pl.BlockSpec((None, D_MODEL, D_MODEL), lambda i: (l, 0, 0), pipeline_mode=resident),
                  pl.BlockSpec((tm, D_MODEL), lambda i: (i, 0)),
                  mspec,
                  pl.BlockSpec((1, D_MODEL), lambda i: (0, 0))],
        out_specs=pl.BlockSpec((tm, D_MODEL), lambda i: (i, 0)),
        compiler_params=_cparams(("parallel",)),
        name="merge_outproj",
    )(ya, yb, yc, u, u, u, w_branch_b, w_branch_b, w_branch_b, w_out_b, x2d, mod, final_g)


PV_MIX_R, PV_MIX_K, PV_MIX_V, PV_W0, PV_A0, PV_KK, PV_KA, PV_RK, PV_LNG, PV_LNB = range(10)


def _half_sum(x, lo_mask):
    s_lo = jnp.sum(jnp.where(lo_mask, x, 0.0), axis=-1, keepdims=True)
    s_hi = jnp.sum(jnp.where(lo_mask, 0.0, x), axis=-1, keepdims=True)
    return jnp.where(lo_mask, s_lo, s_hi)


def _rwkv_prompt_kernel(r_ref, k_ref, v_ref, g_ref, wd_ref, ad_ref, pv_ref, mixl_ref, w2_ref, a2_ref,
                        shr_ref, shk_ref, shv_ref, shwd_ref, shad_ref, s0_ref,
                        y_ref, sout_ref, xbuf, sb, *, n_chunks, pairs, nck):
    c = RWKV_CHUNK
    gw = pairs * LANE
    ti = pl.program_id(2)
    lo = lax.broadcasted_iota(jnp.int32, (1, LANE), 1) < HEAD_A

    @pl.when(ti == 0)
    def _():
        xbuf[0:SUBLANE, :] = jnp.concatenate(
            [shr_ref[...], shk_ref[...], shv_ref[...], shwd_ref[...], shad_ref[...]], axis=1)
        zero = jnp.zeros((HEAD_A, HEAD_A), F32)
        for g in range(pairs):
            top = jnp.concatenate([s0_ref[2 * g], zero], axis=1)
            bot = jnp.concatenate([zero, s0_ref[2 * g + 1]], axis=1)
            sb[g] = jnp.concatenate([top, bot], axis=0)

    tc = nck * c
    cur = jnp.concatenate([r_ref[...], k_ref[...], v_ref[...], wd_ref[...], ad_ref[...]], axis=1)
    xbuf[SUBLANE:SUBLANE + tc, :] = cur
    prev = xbuf[pl.ds(SUBLANE - 1, tc), :]
    pv = pv_ref[...]
    mixl = mixl_ref[...]
    mix = jnp.concatenate([pv[PV_MIX_R:PV_MIX_R + 1], pv[PV_MIX_K:PV_MIX_K + 1], pv[PV_MIX_V:PV_MIX_V + 1],
                           mixl[0:1], mixl[1:2]], axis=1)
    xbuf[0:SUBLANE, :] = cur[tc - SUBLANE:tc, :]

    w0, a0 = pv[PV_W0:PV_W0 + 1], pv[PV_A0:PV_A0 + 1]
    k_k, k_a, r_k = pv[PV_KK:PV_KK + 1], pv[PV_KA:PV_KA + 1], pv[PV_RK:PV_RK + 1]
    ln_g, ln_b = pv[PV_LNG:PV_LNG + 1], pv[PV_LNB:PV_LNB + 1]
    rowc, colc = _iota2(c, c)
    lower_01 = (rowc >= colc).astype(BF16)

    n = 2 * c
    row, col = _iota2(n, n)
    same_head = (row >> 6) == (col >> 6)
    strict = same_head & (row > col)
    incl = same_head & (row >= col)
    eye = row == col
    mm = functools.partial(_dot, mode=P_CHUNK)

    def stack(x):
        return jnp.concatenate([jnp.where(lo, x, 0.0), jnp.where(lo, 0.0, x)], axis=0)

    gs = range(pairs)
    sls = [slice(g * LANE, (g + 1) * LANE) for g in gs]
    state = {"s": [sb[g] for g in gs]}
    out_rows = [None] * nck

    def chunk_stages(k):
        rows = slice(k * c, (k + 1) * c)
        cur_k = cur[rows]
        z = cur_k + (prev[rows] - cur_k) * mix
        zr, zk, zv = z[:, 0:gw], z[:, gw:2 * gw], z[:, 2 * gw:3 * gw]
        zwd, zad = z[:, 3 * gw:3 * gw + LANE], z[:, 3 * gw + LANE:]
        logw = -_softplus(-(w0 + _dot(jnp.tanh(zwd), w2_ref[...], mode=P_LORA))) - 0.5
        lw = -jnp.exp(logw)
        a = _sigmoid(a0 + _dot(zad, a2_ref[...], mode=P_LORA))
        kkp = zk * k_k
        kmod = zk * (1.0 + (a - 1.0) * k_a)
        yield
        gcum = _dot_exact_lhs(lower_01, lw)
        g_end = gcum[c - 1:c, :]
        e_pos, e_neg = jnp.exp(gcum), jnp.exp(-gcum)
        e_prev, e_end = jnp.exp(gcum - lw), jnp.exp(g_end - gcum)
        eg_end = jnp.exp(g_end)
        rt, kb, kend = zr * e_pos, kmod * e_neg, kmod * e_end
        yield
        kk = [kkp[:, sl] * lax.rsqrt(_half_sum(kkp[:, sl] * kkp[:, sl], lo) + 1e-6) for sl in sls]
        bt = [kk[g] * a[:, sls[g]] for g in gs]
        rt_s = [stack(rt[:, sl]) for sl in sls]
        kb_s = [stack(kb[:, sl]) for sl in sls]
        bb_s = [stack(bt[g] * e_neg[:, sls[g]]) for g in gs]
        kt_s = [stack(kk[g] * e_prev[:, sls[g]]) for g in gs]
        yield
        kend_s = [stack(kend[:, sl]) for sl in sls]
        bend_s = [stack(bt[g] * e_end[:, sls[g]]) for g in gs]
        v_s = [stack(zv[:, sl]) for sl in sls]
        p_all = [mm(jnp.concatenate([kt_s[g], rt_s[g]], axis=0), jnp.concatenate([bb_s[g], kb_s[g]], axis=0), _NT)
                 for g in gs]
        yield
        a_ub = [jnp.where(strict, x[0:n, 0:n], 0.0) for x in p_all]
        a_vk = [jnp.where(strict, x[0:n, n:2 * n], 0.0) for x in p_all]
        p_rb = [jnp.where(incl, x[n:2 * n, 0:n], 0.0) for x in p_all]
        p_rk = [jnp.where(incl, x[n:2 * n, n:2 * n], 0.0) for x in p_all]
        av = [mm(a_vk[g], v_s[g]) for g in gs]
        pv_ = [mm(p_rk[g], v_s[g]) for g in gs]
        t_inv = yield from _inv_unit_lower(a_ub, n, levels=2)
        yield
        m12 = [mm(t_inv[g], jnp.concatenate([kt_s[g], av[g]], axis=1)) for g in gs]
        yield
        w2m = [mm(p_rb[g], m12[g]) for g in gs]
        q1 = [rt_s[g] - w2m[g][:, 0:LANE] for g in gs]
        q2 = [pv_[g] - w2m[g][:, LANE:2 * LANE] for g in gs]
        phi = [jnp.where(eye, eg_end[:, sls[g]], 0.0) - mm(bend_s[g], m12[g][:, 0:LANE], _TN) for g in gs]
        psi_t = [mm(jnp.concatenate([v_s[g], m12[g][:, LANE:2 * LANE]], axis=0),
                    jnp.concatenate([kend_s[g], -bend_s[g]], axis=0), _TN) for g in gs]
        yield
        s_old = state["s"]
        y_s = [_dot(q1[g], s_old[g], _NT, P_STATE) + q2[g] for g in gs]
        state["s"] = [_dot(s_old[g], phi[g], _NT, P_STATE) + psi_t[g] for g in gs]
        yield
        gate = _silu(g_ref[rows, :])
        rk_prod = zr * kmod * r_k
        outs = []
        for g in gs:
            sl = sls[g]
            y2 = y_s[g][0:c, :] + y_s[g][c:n, :]
            mu = _half_sum(y2, lo) * (1.0 / HEAD_A)
            dlt = y2 - mu
            var = _half_sum(dlt * dlt, lo) * (1.0 / HEAD_A)
            yn = dlt * lax.rsqrt(var + GN_EPS_A) * ln_g[:, sl] + ln_b[:, sl]
            bonus = _half_sum(rk_prod[:, sl], lo) * zv[:, sl]
            outs.append((yn + bonus) * gate[:, sl])
        out_rows[k] = jnp.concatenate(outs, axis=1)

    _interleave([chunk_stages(k) for k in range(nck)], lead=RWKV_LEAD)
    for g in gs:
        sb[g] = state["s"][g]
    y_ref[...] = jnp.concatenate(out_rows, axis=0).astype(y_ref.dtype)

    @pl.when(ti == n_chunks - 1)
    def _():
        for g in range(pairs):
            s_fin = sb[g]
            sout_ref[2 * g] = s_fin[0:HEAD_A, 0:HEAD_A]
            sout_ref[2 * g + 1] = s_fin[HEAD_A:2 * HEAD_A, HEAD_A:2 * HEAD_A]


def _rwkv_prompt(u, nb, t, pvec, mixl, w2p, a2p, sh_init, s0, l, pairs=8, nck=2):
    c = nck * RWKV_CHUNK
    nt = t // c
    gw = pairs * LANE
    groups = (H_A // 2) // pairs

    def uspec(off):
        return pl.BlockSpec((c, gw), lambda b, p, i: (b * nt + i, off // gw + p))

    def lspec(off):
        return pl.BlockSpec((c, LANE), lambda b, p, i: (b * nt + i, off // LANE))

    def shspec(off):
        return pl.BlockSpec((None, SUBLANE, gw), lambda b, p, i: (b, 0, off // gw + p))

    def shlspec(off):
        return pl.BlockSpec((None, SUBLANE, LANE), lambda b, p, i: (b, 0, off // LANE))

    group_cols = lambda rows: pl.BlockSpec((None, rows, gw), lambda b, p, i: (l, 0, p))
    return pl.pallas_call(
        functools.partial(_rwkv_prompt_kernel, n_chunks=nt, pairs=pairs, nck=nck),
        out_shape=(jax.ShapeDtypeStruct((nb * t, MIX_W), BF16),
                   jax.ShapeDtypeStruct((nb, H_A, HEAD_A, HEAD_A), F32)),
        grid=(nb, groups, nt),
        in_specs=[uspec(OFF_R), uspec(OFF_K), uspec(OFF_V), uspec(OFF_AG), lspec(OFF_WD), lspec(OFF_AD),
                  group_cols(16),
                  pl.BlockSpec((None, SUBLANE, LANE), lambda b, p, i: (l, 0, 0)),
                  group_cols(LANE), group_cols(LANE),
                  shspec(0), shspec(MIX_W), shspec(2 * MIX_W), shlspec(3 * MIX_W), shlspec(3 * MIX_W + LANE),
                  pl.BlockSpec((None, 2 * pairs, HEAD_A, HEAD_A), lambda b, p, i: (b, p, 0, 0))],
        out_specs=(pl.BlockSpec((c, gw), lambda b, p, i: (b * nt + i, p)),
                   pl.BlockSpec((None, 2 * pairs, HEAD_A, HEAD_A), lambda b, p, i: (b, p, 0, 0))),
        scratch_shapes=[pltpu.VMEM((c + SUBLANE, 3 * gw + 2 * LANE), F32),
                        pltpu.VMEM((pairs, 2 * HEAD_A, 2 * HEAD_A), F32)],
        compiler_params=_cparams(("parallel", "parallel", "arbitrary")),
        name="rwkv_prompt",
    )(u, u, u, u, u, u, pvec, mixl, w2p, a2p, sh_init, sh_init, sh_init, sh_init, sh_init, s0)


TB_R, TB_W, TB_KKP, TB_A, TB_KM, TB_V, TB_GATE = range(7)
PT_RK, PT_LNG, PT_LNB = range(3)


def _rwkv_step_kernel(rkv_ref, g_ref, wd_ref, ad_ref, prev_ref, mix_ref, pv_ref, w2_ref, a2_ref, pt_ref,
                      s_ref, acc_ref, y_ref, sout_ref, tab, ybuf):
    del acc_ref
    h = pl.program_id(0)

    @pl.when(h == 0)
    def _():
        cur = jnp.concatenate([rkv_ref[...], wd_ref[...], ad_ref[...]], axis=1)
        z = cur + (prev_ref[...] - cur) * mix_ref[...]
        zr, zk, zv = z[:, 0:MIX_W], z[:, MIX_W:2 * MIX_W], z[:, 2 * MIX_W:3 * MIX_W]
        zwd, zad = z[:, 3 * MIX_W:3 * MIX_W + LANE], z[:, 3 * MIX_W + LANE:]
        pv = pv_ref[...]
        logw = -_softplus(-(pv[PV_W0:PV_W0 + 1] + _dot(jnp.tanh(zwd), w2_ref[...], mode=P_STEP))) - 0.5
        a = _sigmoid(pv[PV_A0:PV_A0 + 1] + _dot(zad, a2_ref[...], mode=P_STEP))
        tab[TB_R] = zr.T
        tab[TB_W] = jnp.exp(-jnp.exp(logw)).T
        tab[TB_KKP] = (zk * pv[PV_KK:PV_KK + 1]).T
        tab[TB_A] = a.T
        tab[TB_KM] = (zk * (1.0 + (a - 1.0) * pv[PV_KA:PV_KA + 1])).T
        tab[TB_V] = zv.T
        tab[TB_GATE] = _silu(g_ref[...]).T

    rows = pl.ds(pl.multiple_of(h * HEAD_A, HEAD_A), HEAD_A)
    r_t, w_t, km_t = tab[TB_R, rows, :], tab[TB_W, rows, :], tab[TB_KM, rows, :]
    kkp_t = tab[TB_KKP, rows, :]
    kk_t = kkp_t * lax.rsqrt(jnp.sum(kkp_t * kkp_t, axis=0, keepdims=True) + 1e-6)
    bt_t = kk_t * tab[TB_A, rows, :]
    v_t = tab[TB_V, rows, :]
    for v in range(HEAD_A):
        s = s_ref[v]
        sa = jnp.sum(s * kk_t, axis=0, keepdims=True)
        s_new = s * w_t - sa * bt_t + v_t[v:v + 1, :] * km_t
        sout_ref[v] = s_new
        ybuf[v:v + 1, :] = jnp.sum(s_new * r_t, axis=0, keepdims=True)
    y = ybuf[...]
    mu = jnp.mean(y, axis=0, keepdims=True)
    dlt = y - mu
    var = jnp.mean(dlt * dlt, axis=0, keepdims=True)
    yn = dlt * lax.rsqrt(var + GN_EPS_A) * pt_ref[PT_LNG] + pt_ref[PT_LNB]
    bonus = jnp.sum(r_t * km_t * pt_ref[PT_RK], axis=0, keepdims=True) * v_t
    y_ref[...] = (yn + bonus) * tab[TB_GATE, rows, :]


def _rwkv_step(u, prev_pad, mix_pad, pvec, w2p, a2p, ptab, s_all, s_acc, l):
    nb = u.shape[0]
    lfull = lambda shape: pl.BlockSpec((None,) + shape, lambda h: (l,) + tuple(0 for _ in shape))
    st_spec = pl.BlockSpec((None, None, HEAD_A, HEAD_A, nb), lambda h: (l, h, 0, 0, 0))
    return pl.pallas_call(
        _rwkv_step_kernel,
        out_shape=(jax.ShapeDtypeStruct((MIX_W, nb), F32),
                   jax.ShapeDtypeStruct(s_acc.shape, F32)),
        grid=(H_A,),
        in_specs=[pl.BlockSpec((nb, 3 * MIX_W), lambda h: (0, 0)),
                  pl.BlockSpec((nb, MIX_W), lambda h: (0, OFF_AG // MIX_W)),
                  pl.BlockSpec((nb, LANE), lambda h: (0, OFF_WD // LANE)),
                  pl.BlockSpec((nb, LANE), lambda h: (0, OFF_AD // LANE)),
                  lfull((nb, SHIFT_PAD)),
                  lfull((1, SHIFT_PAD)), lfull((16, MIX_W)), lfull((LANE, MIX_W)), lfull((LANE, MIX_W)),
                  pl.BlockSpec((None, 3, HEAD_A, LANE), lambda h: (l, 0, h, 0)),
                  st_spec,
                  pl.BlockSpec(memory_space=pl.ANY)],
        out_specs=(pl.BlockSpec((HEAD_A, nb), lambda h: (h, 0)), st_spec),
        scratch_shapes=[pltpu.VMEM((7, MIX_W, nb), F32), pltpu.VMEM((HEAD_A, nb), F32)],
        input_output_aliases={11: 1},
        compiler_params=_cparams(("arbitrary",)),
        name="rwkv_step",
    )(u, u, u, u, prev_pad, mix_pad, pvec, w2p, a2p, ptab, s_all, s_acc)


LP_CONVB, LP_BA, LP_BX, LP_LAM = range(4)


def _lru_gates(xc, wa_ref, wx_ref, lp):
    r = _sigmoid(_mm_bf16(xc, wa_ref[...]) + lp[LP_BA:LP_BA + 1])
    i = _sigmoid(_mm_bf16(xc, wx_ref[...]) + lp[LP_BX:LP_BX + 1])
    log_a = -LRU_C * r * _softplus(-lp[LP_LAM:LP_LAM + 1])
    a = jnp.exp(log_a)
    t = jnp.tanh(-log_a) * (a * a + 1.0)
    b = jnp.where(t > 0.0, t * lax.rsqrt(t), 0.0) * (i * xc)
    return a, b


def _lru_prompt_kernel(x_ref, g_ref, cw_ref, lp_ref, wa_ref, wx_ref, cinit_ref, h0_ref,
                       y_ref, hout_ref, xbuf, hcar, *, n_blocks, tc):
    ti = pl.program_id(1)

    @pl.when(ti == 0)
    def _():
        xbuf[0:SUBLANE, :] = cinit_ref[...]
        hcar[...] = h0_ref[...]

    x = x_ref[...]
    xbuf[SUBLANE:SUBLANE + tc, :] = x
    cw = cw_ref[...]
    lp = lp_ref[...]
    xc = (xbuf[pl.ds(SUBLANE - 3, tc), :] * cw[0:1] + xbuf[pl.ds(SUBLANE - 2, tc), :] * cw[1:2]
          + xbuf[pl.ds(SUBLANE - 1, tc), :] * cw[2:3] + x * cw[3:4] + lp[LP_CONVB:LP_CONVB + 1])
    xbuf[0:SUBLANE, :] = x[tc - SUBLANE:tc, :]
    a, b = _lru_gates(xc, wa_ref, wx_ref, lp)
    n_tiles = tc // SUBLANE
    a = a.reshape(n_tiles, SUBLANE, MIX_W)
    b = b.reshape(n_tiles, SUBLANE, MIX_W)
    sub = lax.broadcasted_iota(jnp.int32, (n_tiles, SUBLANE, MIX_W), 1)
    s = 1
    while s < SUBLANE:
        a_sh = pltpu.roll(a, s, 1)
        b_sh = pltpu.roll(b, s, 1)
        m = sub >= s
        b = jnp.where(m, a * b_sh + b, b)
        a = jnp.where(m, a * a_sh, a)
        s *= 2
    carry = hcar[...]
    tiles = []
    for j in range(n_tiles):
        h_j = b[j] + a[j] * carry
        carry = h_j[SUBLANE - 1:SUBLANE, :]
        tiles.append(h_j)
    h = jnp.concatenate(tiles, axis=0)
    hcar[...] = carry
    y_ref[...] = (h * _silu(g_ref[...])).astype(y_ref.dtype)

    @pl.when(ti == n_blocks - 1)
    def _():
        hout_ref[...] = h[tc - 1:tc, :]


def _lru_prompt(u, nb, t, conv_w, lp, wa_d, wx_d, cinit, h0, l, tc=256):
    nt = t // tc
    const = lambda shape: pl.BlockSpec((None,) + shape, lambda b, i: (l,) + tuple(0 for _ in shape))
    return pl.pallas_call(
        functools.partial(_lru_prompt_kernel, n_blocks=nt, tc=tc),
        out_shape=(jax.ShapeDtypeStruct((nb * t, MIX_W), BF16),
                   jax.ShapeDtypeStruct((nb, 1, MIX_W), F32)),
        grid=(nb, nt),
        in_specs=[pl.BlockSpec((tc, MIX_W), lambda b, i: (b * nt + i, OFF_BX // MIX_W)),
                  pl.BlockSpec((tc, MIX_W), lambda b, i: (b * nt + i, OFF_BG // MIX_W)),
                  const((CONV_W, MIX_W)), const((SUBLANE, MIX_W)),
                  const((MIX_W, MIX_W)), const((MIX_W, MIX_W)),
                  pl.BlockSpec((None, SUBLANE, MIX_W), lambda b, i: (b, 0, 0)),
                  pl.BlockSpec((None, 1, MIX_W), lambda b, i: (b, 0, 0))],
        out_specs=(pl.BlockSpec((tc, MIX_W), lambda b, i: (b * nt + i, 0)),
                   pl.BlockSpec((None, 1, MIX_W), lambda b, i: (b, 0, 0))),
        scratch_shapes=[pltpu.VMEM((tc + SUBLANE, MIX_W), F32), pltpu.VMEM((1, MIX_W), F32)],
        compiler_params=_cparams(("parallel", "arbitrary")),
        name="lru_prompt",
    )(u, u, conv_w, lp, wa_d, wx_d, cinit, h0)


def _lru_step_kernel(x_ref, g_ref, buf_ref, h0_ref, cw_ref, lp_ref, wa_ref, wx_ref, y_ref, hout_ref):
    x = x_ref[...]
    cw = cw_ref[...]
    lp = lp_ref[...]
    xc = buf_ref[0] * cw[0:1] + buf_ref[1] * cw[1:2] + buf_ref[2] * cw[2:3] + x * cw[3:4] + lp[LP_CONVB:LP_CONVB + 1]
    a, b = _lru_gates(xc, wa_ref, wx_ref, lp)
    h = a * h0_ref[...] + b
    hout_ref[...] = h
    y_ref[...] = h * _silu(g_ref[...])


def _lru_step(u, buf_t, h0, conv_w, lp, wa_d, wx_d, l):
    nb = u.shape[0]
    const = lambda shape: pl.BlockSpec((None,) + shape, lambda i: (l,) + tuple(0 for _ in shape))
    out = pl.BlockSpec((nb, MIX_W), lambda i: (0, 0))
    return pl.pallas_call(
        _lru_step_kernel,
        out_shape=(jax.ShapeDtypeStruct((nb, MIX_W), F32), jax.ShapeDtypeStruct((nb, MIX_W), F32)),
        grid=(1,),
        in_specs=[pl.BlockSpec((nb, MIX_W), lambda i: (0, OFF_BX // MIX_W)),
                  pl.BlockSpec((nb, MIX_W), lambda i: (0, OFF_BG // MIX_W)),
                  const((CONV_W - 1, nb, MIX_W)), const((nb, MIX_W)),
                  const((CONV_W, MIX_W)), const((SUBLANE, MIX_W)),
                  const((MIX_W, MIX_W)), const((MIX_W, MIX_W))],
        out_specs=(out, out),
        compiler_params=_cparams(("arbitrary",)),
        name="lru_step",
    )(u, u, buf_t, h0, conv_w, lp, wa_d, wx_d)


GP_ALOG, GP_DTB, GP_NORM = range(3)


def _gdn_prompt_kernel(q_ref, k_ref, v_ref, z_ref, ab_ref, cwq_ref, cwk_ref, cwv_ref, gp_ref,
                       ciq_ref, cik_ref, civ_ref, s0_ref, y_ref, sout_ref, xbuf, st, *, n_chunks, heads, nck):
    c = GDN_CHUNK
    tc = nck * c
    hw = heads * LANE
    h_base = pl.program_id(1) * heads
    ti = pl.program_id(2)

    @pl.when(ti == 0)
    def _():
        xbuf[0:SUBLANE, :] = jnp.concatenate([ciq_ref[...], cik_ref[...], civ_ref[...]], axis=1)
        st[...] = s0_ref[...]

    cur = jnp.concatenate([q_ref[...], k_ref[...], v_ref[...]], axis=1)
    xbuf[SUBLANE:SUBLANE + tc, :] = cur
    cw = jnp.concatenate([cwq_ref[...], cwk_ref[...], cwv_ref[...]], axis=1)

    gp = gp_ref[...]
    norm_g = gp[GP_NORM:GP_NORM + 1]
    lane_c = lax.broadcasted_iota(jnp.int32, (c, LANE), 1)
    row, col = _iota2(c, c)
    incl = row >= col
    strict = row > col
    eye = row == col
    incl_01 = incl.astype(BF16)
    mm = functools.partial(_dot, mode=P_CHUNK)
    hs_ = range(heads)
    state = {"s": [st[j] for j in hs_]}
    out_rows = [None] * nck

    def chunk_stages(k):
        base = SUBLANE + k * c
        x0 = cur[k * c:(k + 1) * c]
        conv = (xbuf[pl.ds(base - 3, c), :] * cw[0:1] + xbuf[pl.ds(base - 2, c), :] * cw[1:2]
                + xbuf[pl.ds(base - 1, c), :] * cw[2:3] + x0 * cw[3:4])
        act = _silu(conv)
        yield
        ab = ab_ref[k * c:(k + 1) * c, :]
        g_all = -jnp.exp(gp[GP_ALOG:GP_ALOG + 1]) * _softplus(ab + gp[GP_DTB:GP_DTB + 1])
        beta_all = _sigmoid(ab)
        qh = [act[:, j * DK:(j + 1) * DK] for j in hs_]
        kh = [act[:, hw + j * DK:hw + (j + 1) * DK] for j in hs_]
        vh = [act[:, 2 * hw + j * DV:2 * hw + (j + 1) * DV] for j in hs_]
        qn = [x * lax.rsqrt(jnp.sum(x * x, axis=-1, keepdims=True) + 1e-6) * (DK ** -0.5) for x in qh]
        kn = [x * lax.rsqrt(jnp.sum(x * x, axis=-1, keepdims=True) + 1e-6) for x in kh]
        beta = [jnp.sum(jnp.where(lane_c == h_base + j + H_C, beta_all, 0.0), axis=-1, keepdims=True) for j in hs_]
        yield
        gc_all = _dot_exact_lhs(incl_01, g_all)
        gc = [jnp.broadcast_to(jnp.sum(jnp.where(lane_c == h_base + j, gc_all, 0.0), axis=-1, keepdims=True),
                               (c, LANE)) for j in hs_]
        gc_row = [jnp.sum(jnp.where(eye, x, 0.0), axis=0, keepdims=True) for x in gc]
        decay = [jnp.where(incl, jnp.exp(jnp.where(incl, gc[j] - gc_row[j], 0.0)), 0.0) for j in hs_]
        g_last = [x[c - 1:c, :] for x in gc]
        eg = [jnp.exp(x) for x in gc]
        yield
        kb = [kn[j] * beta[j] for j in hs_]
        p_all = [mm(jnp.concatenate([kb[j], qn[j]], axis=0), kn[j], _NT) for j in hs_]
        yield
        a_mat = [jnp.where(strict, p_all[j][0:c] * decay[j], 0.0) for j in hs_]
        qk = [p_all[j][c:2 * c] * decay[j] for j in hs_]
        k_dec = [kn[j] * jnp.exp(g_last[j] - gc[j]) for j in hs_]
        rhs = [jnp.concatenate([kb[j] * eg[j], vh[j] * beta[j]], axis=1) for j in hs_]
        t_inv = yield from _inv_unit_lower(a_mat, c, levels=3)
        yield
        wu = [mm(t_inv[j], rhs[j]) for j in hs_]
        yield
        w3 = [mm(k_dec[j], wu[j], _TN) for j in hs_]
        w4 = [mm(qk[j], wu[j]) for j in hs_]
        phi = [jnp.where(eye, jnp.exp(g_last[j]), 0.0) - w3[j][:, 0:DK] for j in hs_]
        q1 = [qn[j] * eg[j] - w4[j][:, 0:DK] for j in hs_]
        yield
        s_old = state["s"]
        o = [_dot(q1[j], s_old[j], _NN, P_STATE) + w4[j][:, DK:] for j in hs_]
        state["s"] = [_dot(phi[j], s_old[j], _NN, P_STATE) + w3[j][:, DK:] for j in hs_]
        yield
        zg = _silu(z_ref[k * c:(k + 1) * c, :])
        outs = [o[j] * lax.rsqrt(jnp.mean(o[j] * o[j], axis=-1, keepdims=True) + NORM_EPS) * norm_g
                * zg[:, j * DV:(j + 1) * DV] for j in hs_]
        out_rows[k] = jnp.concatenate(outs, axis=1)

    _interleave([chunk_stages(k) for k in range(nck)], lead=GDN_LEAD)
    xbuf[0:SUBLANE, :] = cur[tc - SUBLANE:tc, :]
    for j in hs_:
        st[j] = state["s"][j]
    y_ref[...] = jnp.concatenate(out_rows, axis=0).astype(y_ref.dtype)

    @pl.when(ti == n_chunks - 1)
    def _():
        sout_ref[...] = st[...]


def _gdn_prompt(u, nb, t, conv_w, gpar, cinit, s0, l, heads=8, nck=2):
    assert GDN_CHUNK == LANE and DK == LANE and DV == LANE
    c = nck * GDN_CHUNK
    nt = t // c
    hw = heads * LANE

    def uspec(off):
        return pl.BlockSpec((c, hw), lambda b, h, i: (b * nt + i, off // hw + h))

    def cwspec(off):
        return pl.BlockSpec((None, CONV_W, hw), lambda b, h, i: (l, 0, off // hw + h))

    def cispec(off):
        return pl.BlockSpec((None, SUBLANE, hw), lambda b, h, i: (b, 0, off // hw + h))

    return pl.pallas_call(
        functools.partial(_gdn_prompt_kernel, n_chunks=nt, heads=heads, nck=nck),
        out_shape=(jax.ShapeDtypeStruct((nb * t, MIX_W), BF16),
                   jax.ShapeDtypeStruct((nb, H_C, DK, DV), F32)),
        grid=(nb, H_C // heads, nt),
        in_specs=[uspec(OFF_Q), uspec(OFF_Q + MIX_W), uspec(OFF_Q + 2 * MIX_W), uspec(OFF_Z),
                  pl.BlockSpec((c, LANE), lambda b, h, i: (b * nt + i, OFF_AB // LANE)),
                  cwspec(0), cwspec(MIX_W), cwspec(2 * MIX_W),
                  pl.BlockSpec((None, SUBLANE, LANE), lambda b, h, i: (l, 0, 0)),
                  cispec(0), cispec(MIX_W), cispec(2 * MIX_W),
                  pl.BlockSpec((None, heads, DK, DV), lambda b, h, i: (b, h, 0, 0))],
        out_specs=(pl.BlockSpec((c, hw), lambda b, h, i: (b * nt + i, h)),
                   pl.BlockSpec((None, heads, DK, DV), lambda b, h, i: (b, h, 0, 0))),
        scratch_shapes=[pltpu.VMEM((c + SUBLANE, 3 * hw), F32), pltpu.VMEM((heads, DK, DV), F32)],
        compiler_params=_cparams(("parallel", "parallel", "arbitrary")),
        name="gdn_prompt",
    )(u, u, u, u, u, conv_w, conv_w, conv_w, gpar, cinit, cinit, cinit, s0)


def _gdn_step_kernel(qkv_ref, z_ref, ab_ref, buf_ref, cw_ref, gp_ref, s_ref, acc_ref, y_ref, sout_ref, hs, *, bb):
    del acc_ref
    cw = cw_ref[...]
    cur = qkv_ref[...]
    act = _silu(buf_ref[0] * cw[0:1] + buf_ref[1] * cw[1:2] + buf_ref[2] * cw[2:3] + cur * cw[3:4])
    gp = gp_ref[...]
    ab = ab_ref[...]
    g_all = -jnp.exp(gp[GP_ALOG:GP_ALOG + 1]) * _softplus(ab + gp[GP_DTB:GP_DTB + 1])
    beta_all = _sigmoid(ab)
    zg = _silu(z_ref[...])
    hs_q, hs_k, hs_v, hs_eg, hs_beta, hs_zg, hs_qk = range(7)
    for h in range(H_C):
        qh = act[:, h * DK:(h + 1) * DK]
        kh = act[:, MIX_W + h * DK:MIX_W + (h + 1) * DK]
        qn = qh * lax.rsqrt(jnp.sum(qh * qh, axis=-1, keepdims=True) + 1e-6) * (DK ** -0.5)
        kn = kh * lax.rsqrt(jnp.sum(kh * kh, axis=-1, keepdims=True) + 1e-6)
        hs[hs_q, h] = qn
        hs[hs_k, h] = kn
        hs[hs_v, h] = act[:, 2 * MIX_W + h * DV:2 * MIX_W + (h + 1) * DV]
        hs[hs_eg, h] = jnp.broadcast_to(jnp.exp(g_all[:, h:h + 1]), (bb, LANE))
        hs[hs_beta, h] = jnp.broadcast_to(beta_all[:, H_C + h:H_C + h + 1], (bb, LANE))
        hs[hs_zg, h] = zg[:, h * DV:(h + 1) * DV]
        hs[hs_qk, h] = jnp.broadcast_to(jnp.sum(qn * kn, axis=-1, keepdims=True), (bb, LANE))

    row, col = _iota2(DK, DV)
    diag = row == col
    rowb = lax.broadcasted_iota(jnp.int32, (bb, DV), 0)
    norm_g = gp[GP_NORM:GP_NORM + 1]
    heads = range(H_C)

    def body(b, outs):
        def rows(q):
            return [hs[q, h, pl.ds(b, 1), :] for h in heads]

        q, k, v, eg, beta = rows(hs_q), rows(hs_k), rows(hs_v), rows(hs_eg), rows(hs_beta)
        s = [s_ref[b, h] for h in heads]
        kcol = [jnp.sum(jnp.where(diag, jnp.broadcast_to(x, (DK, DV)), 0.0), axis=-1, keepdims=True) for x in k]
        qcol = [jnp.sum(jnp.where(diag, jnp.broadcast_to(x, (DK, DV)), 0.0), axis=-1, keepdims=True) for x in q]
        ks = [jnp.sum(kcol[h] * s[h], axis=0, keepdims=True) for h in heads]
        qs = [jnp.sum(qcol[h] * s[h], axis=0, keepdims=True) for h in heads]
        v_new = [beta[h] * (v[h] - eg[h] * ks[h]) for h in heads]
        for h in heads:
            sout_ref[b, h] = s[h] * eg[h] + kcol[h] * v_new[h]
        qk = rows(hs_qk)
        o = [eg[h] * qs[h] + qk[h] * v_new[h] for h in heads]
        ms = [jnp.mean(x * x, axis=-1, keepdims=True) for x in o]
        zg_h = rows(hs_zg)
        res = [o[h] * lax.rsqrt(ms[h] + NORM_EPS) * norm_g * zg_h[h] for h in heads]
        return tuple(jnp.where(rowb == b, jnp.broadcast_to(res[h], (bb, DV)), outs[h]) for h in heads)

    outs = lax.fori_loop(0, bb, body, tuple(jnp.zeros((bb, DV), F32) for _ in heads))
    y_ref[...] = jnp.concatenate(outs, axis=1)


def _gdn_step(u, buf_t, conv_w, gpar, s_all, s_acc, l, bb=SUBLANE):
    nb = u.shape[0]
    lfull = lambda shape: pl.BlockSpec((None,) + shape, lambda i: (l,) + tuple(0 for _ in shape))
    st_spec = pl.BlockSpec((None, bb, H_C, DK, DV), lambda i: (l, i, 0, 0, 0))
    return pl.pallas_call(
        functools.partial(_gdn_step_kernel, bb=bb),
        out_shape=(jax.ShapeDtypeStruct((nb, MIX_W), F32),
                   jax.ShapeDtypeStruct(s_acc.shape, F32)),
        grid=(nb // bb,),
        in_specs=[pl.BlockSpec((bb, QKV_W), lambda i: (i, OFF_Q // QKV_W)),
                  pl.BlockSpec((bb, MIX_W), lambda i: (i, OFF_Z // MIX_W)),
                  pl.BlockSpec((bb, LANE), lambda i: (i, OFF_AB // LANE)),
                  pl.BlockSpec((None, CONV_W - 1, bb, QKV_W), lambda i: (l, 0, i, 0)),
                  lfull((CONV_W, QKV_W)), lfull((SUBLANE, LANE)),
                  st_spec,
                  pl.BlockSpec(memory_space=pl.ANY)],
        out_specs=(pl.BlockSpec((bb, MIX_W), lambda i: (i, 0)), st_spec),
        scratch_shapes=[pltpu.VMEM((7, H_C, bb, LANE), F32)],
        input_output_aliases={7: 1},
        compiler_params=_cparams(("parallel",)),
        name="gdn_step",
    )(u, u, u, buf_t, conv_w, gpar, s_all, s_acc)


def _pad_cols(w, width):
    pad = width - w.shape[-1]
    return jnp.pad(w, [(0, 0)] * (w.ndim - 1) + [(0, pad)])


def _layout_in_cols(w):
    a_sh_end = A_SHIFT_W
    o_ag = a_sh_end
    o_bx = o_ag + MIX_W
    o_bg = o_bx + MIX_W
    o_q = o_bg + MIX_W
    o_a = o_q + QKV_W
    o_z = o_a + 2 * H_C
    o_m = o_z + MIX_W
    parts = [w[..., 0:3 * MIX_W], w[..., o_ag:o_bx], w[..., o_bx:o_bg], w[..., o_bg:o_q], w[..., o_q:o_a],
             w[..., o_z:o_m], w[..., o_m:IN_COLS],
             _pad_cols(w[..., 3 * MIX_W:3 * MIX_W + LORA], LANE),
             _pad_cols(w[..., 3 * MIX_W + LORA:a_sh_end], LANE),
             _pad_cols(w[..., o_a:o_z], LANE)]
    used = sum(part.shape[-1] for part in parts)
    parts.append(jnp.zeros(w.shape[:-1] + (N_PAD - used,), w.dtype))
    return jnp.concatenate(parts, axis=-1)


def _shift_to_padded(s):
    return jnp.concatenate([s[..., 0:3 * MIX_W], _pad_cols(s[..., 3 * MIX_W:3 * MIX_W + LORA], LANE),
                            _pad_cols(s[..., 3 * MIX_W + LORA:], LANE)], axis=-1)


def _shift_from_u(u_rows):
    return jnp.concatenate([u_rows[..., 0:3 * MIX_W], u_rows[..., OFF_WD:OFF_WD + LORA],
                            u_rows[..., OFF_AD:OFF_AD + LORA]], axis=-1)


def _block_diag(w):
    eye = jnp.eye(LRU_BLOCKS, dtype=w.dtype)
    return jnp.einsum('lnij,nm->lnimj', w, eye).reshape(w.shape[0], MIX_W, MIX_W)


def _rows_table(rows, n_rows):
    tab = jnp.stack(rows, axis=1)
    return jnp.pad(tab, ((0, 0), (0, n_rows - len(rows)), (0, 0)))


def _prep_params(p):
    depth = p['rwkv_mix'].shape[0]
    mix = p['rwkv_mix']
    pvec = _rows_table([mix[:, 0:MIX_W], mix[:, MIX_W:2 * MIX_W], mix[:, 2 * MIX_W:3 * MIX_W],
                        p['rwkv_w0'], p['rwkv_a0'], p['rwkv_kk'], p['rwkv_ka'], p['rwkv_rk'].reshape(depth, MIX_W),
                        p['rwkv_lnx_g'], p['rwkv_lnx_b']], 16)
    mix_pad = _shift_to_padded(mix)
    mixl = _rows_table([mix_pad[:, 3 * MIX_W:3 * MIX_W + LANE], mix_pad[:, 3 * MIX_W + LANE:]], SUBLANE)
    lora_pad = ((0, 0), (0, LANE - LORA), (0, 0))
    lp = _rows_table([p['lru_conv_b'], p['lru_ba'], p['lru_bx'], p['lru_lambda']], SUBLANE)
    gpar = _rows_table([_pad_cols(p['gdn_a_log'], LANE), _pad_cols(p['gdn_dt_bias'], LANE), p['gdn_norm_g']], SUBLANE)
    ptab = jnp.stack([p['rwkv_rk'].reshape(depth, MIX_W), p['rwkv_lnx_g'], p['rwkv_lnx_b']], axis=1)
    ptab = jnp.broadcast_to(ptab[..., None], (depth, 3, MIX_W, LANE))
    return dict(pvec=pvec, mix_pad=mix_pad[:, None], mixl=mixl, ptab=ptab,
                w2p=jnp.pad(p['rwkv_w2'], lora_pad), a2p=jnp.pad(p['rwkv_a2'], lora_pad),
                lru_cw=p['lru_conv_w'], lp=lp,
                wa_d=_block_diag(p['lru_wa']).astype(BF16), wx_d=_block_diag(p['lru_wx']).astype(BF16),
                gdn_cw=p['gdn_conv_w'], gpar=gpar)


def _prompt_layer(x2d, nb, t, l, pp, zeros, mod_p, norm_g3, w_in_p, w_branch_b, w_out_b, final_g, last):
    u = _inproj(x2d, mod_p, norm_g3, w_in_p, l, rows_per_mod=t, tm=PROMPT_TM, tn=INPROJ_TN)
    ya, wkv = _rwkv_prompt(u, nb, t, pp['pvec'], pp['mixl'], pp['w2p'], pp['a2p'], zeros['shift'], zeros['wkv'], l)
    yb, h_last = _lru_prompt(u, nb, t, pp['lru_cw'], pp['lp'], pp['wa_d'], pp['wx_d'],
                             zeros['lru_conv'], zeros['lru_h'], l)
    yc, ssm = _gdn_prompt(u, nb, t, pp['gdn_cw'], pp['gpar'], zeros['gdn_conv'], zeros['ssm'], l)
    x_new = _mergeout(ya, yb, yc, u, w_branch_b, w_out_b, x2d, mod_p, final_g, l, rows_per_mod=t, tm=MERGEOUT_TM,
                      final_norm=last)
    u3 = u.reshape(nb, t, N_PAD)
    states = (_shift_from_u(u3[:, t - 1]), wkv,
              u3[:, t - (CONV_W - 1):, OFF_BX:OFF_BX + MIX_W], h_last.reshape(nb, MIX_W),
              u3[:, t - (CONV_W - 1):, OFF_Q:OFF_Q + QKV_W], ssm)
    return x_new, states


def _sample_layer(x2d, l, pp, st, acc, mod_s, norm_g3, w_in_p, w_branch_b, w_out_b, final_g, last):
    nb = x2d.shape[0]
    u = _inproj(x2d, mod_s, norm_g3, w_in_p, l, rows_per_mod=1, tm=nb, tn=INPROJ_TN)
    ya_t, wkv_acc = _rwkv_step(u, st['shift'], pp['mix_pad'], pp['pvec'], pp['w2p'], pp['a2p'], pp['ptab'],
                               st['wkv_t'], acc[0], l)
    ya = ya_t.T
    yb, h_new = _lru_step(u, st['lru_conv_t'], st['lru_h'], pp['lru_cw'], pp['lp'], pp['wa_d'], pp['wx_d'], l)
    yc, ssm_acc = _gdn_step(u, st['gdn_conv_t'], pp['gdn_cw'], pp['gpar'], st['ssm'], acc[1], l)
    x_new = _mergeout(ya, yb, yc, u, w_branch_b, w_out_b, x2d, mod_s, final_g, l, rows_per_mod=1, tm=nb,
                      final_norm=last)
    rows = (_shift_from_u(u), u[:, OFF_BX:OFF_BX + MIX_W], h_new, u[:, OFF_Q:OFF_Q + QKV_W])
    return x_new, rows, (wkv_acc, ssm_acc)


def kernel(x_prompt, x_sample, state_rwkv_shift, state_rwkv_wkv, state_lru_conv, state_lru_h, state_gdn_conv, state_gdn_ssm, c_prompt, c_sample, ada_w, ada_b, norm_g, w_in, rwkv_mix, rwkv_w0, rwkv_w2, rwkv_a0, rwkv_a2, rwkv_kk, rwkv_ka, rwkv_rk, rwkv_lnx_g, rwkv_lnx_b, lru_conv_w, lru_conv_b, lru_wa, lru_ba, lru_wx, lru_bx, lru_lambda, gdn_conv_w, gdn_a_log, gdn_dt_bias, gdn_norm_g, w_branch, w_out, final_g):
    p = dict(rwkv_mix=rwkv_mix, rwkv_w0=rwkv_w0, rwkv_w2=rwkv_w2, rwkv_a0=rwkv_a0, rwkv_a2=rwkv_a2,
             rwkv_kk=rwkv_kk, rwkv_ka=rwkv_ka, rwkv_rk=rwkv_rk, rwkv_lnx_g=rwkv_lnx_g, rwkv_lnx_b=rwkv_lnx_b,
             lru_conv_w=lru_conv_w, lru_conv_b=lru_conv_b, lru_wa=lru_wa, lru_ba=lru_ba, lru_wx=lru_wx,
             lru_bx=lru_bx, lru_lambda=lru_lambda, gdn_conv_w=gdn_conv_w, gdn_a_log=gdn_a_log,
             gdn_dt_bias=gdn_dt_bias, gdn_norm_g=gdn_norm_g)
    depth = w_in.shape[0]
    bp, t, _ = x_prompt.shape
    bs = x_sample.shape[0]

    w_in_p = jnp.swapaxes(_layout_in_cols(w_in).astype(BF16), 1, 2)
    w_branch_b = w_branch.astype(BF16)
    w_out_b = w_out.astype(BF16)
    norm_g3 = norm_g.reshape(depth, 1, D_MODEL)

    rows = bp + bs
    rows_pad = -(-rows // SUBLANE) * SUBLANE
    c_all = jnp.pad(jnp.concatenate([c_prompt, c_sample], axis=0), ((0, rows_pad - rows), (0, 0)))
    mod = _ada_all(c_all, ada_w, ada_b)
    mod_p = mod[:, 0:bp].reshape(depth, bp, 1, 3 * D_MODEL)
    mod_s = mod[:, bp:bp + bs]

    pp = _prep_params(p)
    zeros = dict(shift=jnp.zeros((bp, SUBLANE, SHIFT_PAD), F32), wkv=jnp.zeros((bp, H_A, HEAD_A, HEAD_A), F32),
                 lru_conv=jnp.zeros((bp, SUBLANE, MIX_W), F32), lru_h=jnp.zeros((bp, 1, MIX_W), F32),
                 gdn_conv=jnp.zeros((bp, SUBLANE, QKV_W), F32), ssm=jnp.zeros((bp, H_C, DK, DV), F32))
    wkv_t = jnp.transpose(state_rwkv_wkv, (0, 2, 3, 4, 1))
    st = dict(shift=_shift_to_padded(state_rwkv_shift), wkv_t=wkv_t,
              lru_conv_t=jnp.swapaxes(state_lru_conv, 1, 2), lru_h=state_lru_h,
              gdn_conv_t=jnp.swapaxes(state_gdn_conv, 1, 2), ssm=state_gdn_ssm)
    acc = (jnp.zeros(wkv_t.shape, F32), jnp.zeros(state_gdn_ssm.shape, F32))

    xp = x_prompt.reshape(bp * t, D_MODEL)
    xs = x_sample.reshape(bs, D_MODEL)
    p_states, s_rows = [], []
    final_g2 = final_g.reshape(1, D_MODEL)
    for l in range(depth):
        last = l == depth - 1
        xp, pst = _prompt_layer(xp, bp, t, l, pp, zeros, mod_p, norm_g3, w_in_p, w_branch_b, w_out_b, final_g2, last)
        p_states.append(pst)
        xs, rows_l, acc = _sample_layer(xs, l, pp, st, acc, mod_s, norm_g3, w_in_p, w_branch_b, w_out_b,
                                        final_g2, last)
        s_rows.append(rows_l)

    y_prompt = xp.reshape(bp, t, D_MODEL)
    y_sample = xs.reshape(bs, 1, D_MODEL)
    p_out = tuple(jnp.stack([s[i] for s in p_states]) for i in range(6))
    s_shift, s_bx, s_h, s_qkv = (jnp.stack([r[i] for r in s_rows]) for i in range(4))
    s_lru_conv = jnp.concatenate([state_lru_conv[:, :, 1:], s_bx[:, :, None]], axis=2)
    s_gdn_conv = jnp.concatenate([state_gdn_conv[:, :, 1:], s_qkv[:, :, None]], axis=2)
    s_wkv = jnp.transpose(acc[0], (0, 4, 1, 2, 3))
    return (y_prompt, y_sample) + p_out + (s_shift, s_wkv, s_lru_conv, s_h, s_gdn_conv, acc[1])
```

```python
import functools
import math

import jax
import jax.numpy as jnp
import numpy as np
from jax import lax
from jax.experimental import pallas as pl
from jax.experimental.pallas import tpu as pltpu

F32 = jnp.float32
BF16 = jnp.bfloat16
HIGHEST = lax.Precision.HIGHEST

D_MODEL = 2048
DEPTH = 4
MIX_W = 1024
HEAD_A = 64
H_A = 16
LORA = 96
GN_EPS_A = 64e-5
A_SHIFT_W = 3 * MIX_W + 2 * LORA
LRU_BLOCKS = 16
LRU_BS = 64
LRU_C = 8.0
CONV_W = 4
DK = 128
DV = 128
H_C = 8
QKV_W = 3 * MIX_W
NORM_EPS = 1e-6
IN_COLS = 16592

LANE = 128
SUBLANE = 8

OFF_R, OFF_K, OFF_V = 0, 1024, 2048
OFF_AG = 3072
OFF_BX, OFF_BG = 4096, 5120
OFF_Q = 6144
OFF_Z = 9216
OFF_MERGE = 10240
OFF_WD, OFF_AD, OFF_AB = 16384, 16512, 16640
N_PAD = 16896
SHIFT_PAD = 3 * MIX_W + 2 * LANE

RWKV_CHUNK = 64
GDN_CHUNK = 128
RWKV_LEAD = 4
GDN_LEAD = 4
VMEM_LIMIT = 56 * 1024 * 1024

PROMPT_TM = 1024
INPROJ_TN = 1536
MERGEOUT_TM = 256


def _cparams(sem):
    return pltpu.CompilerParams(dimension_semantics=sem, vmem_limit_bytes=VMEM_LIMIT)


def _sigmoid(x):
    return 1.0 / (1.0 + jnp.exp(-x))


def _silu(x):
    return x * _sigmoid(x)


def _softplus(x):
    return jnp.maximum(x, 0.0) + jnp.log1p(jnp.exp(-jnp.abs(x)))


_NN = (((1,), (0,)), ((), ()))
_NT = (((1,), (1,)), ((), ()))
_TN = (((0,), (0,)), ((), ()))

P_LORA = "bf16"
P_CHUNK = "bf16"
P_INV = "bf16"
P_STATE = "bf16"
P_STEP = "f32"


def _dg(a, b, dims):
    return lax.dot_general(a, b, dims, preferred_element_type=F32)


def _dot(a, b, dims=_NN, mode="bf16"):
    if mode == "f32":
        return lax.dot_general(a, b, dims, preferred_element_type=F32, precision=HIGHEST)
    ah, bh = a.astype(BF16), b.astype(BF16)
    if mode == "bf16":
        return _dg(ah, bh, dims)
    al = (a - ah.astype(F32)).astype(BF16)
    bl = (b - bh.astype(F32)).astype(BF16)
    return _dg(ah, bh, dims) + (_dg(al, bh, dims) + _dg(ah, bl, dims))


def _dot_exact_lhs(a01, x):
    a = a01
    x1 = x.astype(BF16)
    r1 = x - x1.astype(F32)
    x2 = r1.astype(BF16)
    x3 = (r1 - x2.astype(F32)).astype(BF16)
    return _dg(a, x1, _NN) + (_dg(a, x2, _NN) + _dg(a, x3, _NN))


def _mm_bf16(a, b):
    return jnp.dot(a.astype(BF16), b.astype(BF16), preferred_element_type=F32)


def _iota2(n, m):
    return (lax.broadcasted_iota(jnp.int32, (n, m), 0), lax.broadcasted_iota(jnp.int32, (n, m), 1))


def _inv_unit_lower(mats, n, levels):
    mm = functools.partial(_dot, mode=P_INV)
    row, col = _iota2(n, n)
    eye = (row == col).astype(F32)
    same = (row >> 4) == (col >> 4)
    d = [jnp.where(same, a, 0.0) for a in mats]
    o = [a - di for a, di in zip(mats, d)]
    x1 = [-di for di in d]
    x2 = [mm(x, x) for x in x1]
    yield
    td = [mm(eye + a, eye + b) for a, b in zip(x1, x2)]
    x4 = [mm(x, x) for x in x2]
    yield
    td = [mm(t, eye + x) for t, x in zip(td, x4)]
    x8 = [mm(x, x) for x in x4]
    yield
    td = [mm(t, eye + x) for t, x in zip(td, x8)]
    yield
    p = [-mm(t, oi) for t, oi in zip(td, o)]
    t = [eye + pi for pi in p]
    for _ in range(levels - 1):
        yield
        p = [mm(pi, pi) for pi in p]
        yield
        t = [mm(ti, eye + pi) for ti, pi in zip(t, p)]
    yield
    return [mm(ti, tdi) for ti, tdi in zip(t, td)]


def _interleave(gens, lead):
    done = [False] * len(gens)
    step = 0
    while not all(done):
        for i, g in enumerate(gens):
            if not done[i] and step >= i * lead:
                try:
                    next(g)
                except StopIteration:
                    done[i] = True
        step += 1


def _ada_kernel(c_ref, w_ref, b_ref, o_ref):
    h = _silu(c_ref[...]).astype(BF16)
    o_ref[...] = jnp.dot(h, w_ref[...].astype(BF16), preferred_element_type=F32) + b_ref[...]


def _ada_all(c_all, ada_w, ada_b, tn=512):
    rows = c_all.shape[0]
    depth = ada_w.shape[0]
    n = ada_w.shape[2]
    return pl.pallas_call(
        _ada_kernel,
        out_shape=jax.ShapeDtypeStruct((depth, rows, n), F32),
        grid=(depth, n // tn),
        in_specs=[pl.BlockSpec((rows, D_MODEL), lambda l, j: (0, 0)),
                  pl.BlockSpec((None, D_MODEL, tn), lambda l, j: (l, 0, j)),
                  pl.BlockSpec((None, 1, tn), lambda l, j: (l, 0, j))],
        out_specs=pl.BlockSpec((None, rows, tn), lambda l, j: (l, 0, j)),
        compiler_params=_cparams(("parallel", "parallel")),
        name="ada_mod",
    )(c_all, ada_w, ada_b.reshape(depth, 1, n))


def _inproj_kernel(x_ref, shift_ref, scale_ref, g_ref, w_ref, o_ref, h_ref):
    @pl.when(pl.program_id(1) == 0)
    def _():
        x = x_ref[...]
        y = x * lax.rsqrt(jnp.mean(x * x, axis=-1, keepdims=True) + NORM_EPS) * g_ref[...]
        h_ref[...] = (y * (1.0 + scale_ref[...]) + shift_ref[...]).astype(BF16)

    o_ref[...] = _dg(h_ref[...], w_ref[...], _NT)


def _inproj(x2d, mod, norm_g3, w_in_p, l, rows_per_mod, tm, tn):
    m = x2d.shape[0]
    n = w_in_p.shape[1]
    if rows_per_mod > 1:
        def mspec(part):
            return pl.BlockSpec((None, None, 1, D_MODEL), lambda i, j: (l, (i * tm) // rows_per_mod, 0, part))
    else:
        def mspec(part):
            return pl.BlockSpec((None, tm, D_MODEL), lambda i, j: (l, i, part))
    return pl.pallas_call(
        _inproj_kernel,
        out_shape=jax.ShapeDtypeStruct((m, n), F32),
        grid=(m // tm, n // tn),
        in_specs=[pl.BlockSpec((tm, D_MODEL), lambda i, j: (i, 0)),
                  mspec(0), mspec(1),
                  pl.BlockSpec((None, 1, D_MODEL), lambda i, j: (l, 0, 0)),
                  pl.BlockSpec((None, tn, D_MODEL), lambda i, j: (l, j, 0))],
        out_specs=pl.BlockSpec((tm, tn), lambda i, j: (i, j)),
        scratch_shapes=[pltpu.VMEM((tm, D_MODEL), BF16)],
        compiler_params=_cparams(("parallel", "arbitrary")),
        name="inproj",
    )(x2d, mod, mod, norm_g3, w_in_p)


def _mergeout_kernel(ya_ref, yb_ref, yc_ref, ga_ref, gb_ref, gc_ref, wa_ref, wb_ref, wc_ref, wo_ref,
                     x_ref, gate_ref, fg_ref, o_ref, *, final_norm):
    acc = _sigmoid(ga_ref[...]) * jnp.dot(ya_ref[...].astype(BF16), wa_ref[...], preferred_element_type=F32)
    acc += _sigmoid(gb_ref[...]) * jnp.dot(yb_ref[...].astype(BF16), wb_ref[...], preferred_element_type=F32)
    acc += _sigmoid(gc_ref[...]) * jnp.dot(yc_ref[...].astype(BF16), wc_ref[...], preferred_element_type=F32)
    proj = jnp.dot(acc.astype(BF16), wo_ref[...], preferred_element_type=F32)
    x = x_ref[...] + gate_ref[...] * proj
    if final_norm:
        x = x * lax.rsqrt(jnp.mean(x * x, axis=-1, keepdims=True) + NORM_EPS) * fg_ref[...]
    o_ref[...] = x


def _mergeout(ya, yb, yc, u, w_branch_b, w_out_b, x2d, mod, final_g, l, rows_per_mod, tm, final_norm):
    m = x2d.shape[0]
    yspec = pl.BlockSpec((tm, MIX_W), lambda i: (i, 0))
    resident = pl.Buffered(1)

    def gspec(br):
        return pl.BlockSpec((tm, D_MODEL), lambda i: (i, OFF_MERGE // D_MODEL + br))

    def wspec(br):
        return pl.BlockSpec((None, None, MIX_W, D_MODEL), lambda i: (l, br, 0, 0), pipeline_mode=resident)

    if rows_per_mod > 1:
        mspec = pl.BlockSpec((None, None, 1, D_MODEL), lambda i: (l, (i * tm) // rows_per_mod, 0, 2))
    else:
        mspec = pl.BlockSpec((None, tm, D_MODEL), lambda i: (l, i, 2))
    return pl.pallas_call(
        functools.partial(_mergeout_kernel, final_norm=final_norm),
        out_shape=jax.ShapeDtypeStruct((m, D_MODEL), F32),
        grid=(m // tm,),
        in_specs=[yspec, yspec, yspec, gspec(0), gspec(1), gspec(2), wspec(0), wspec(1), wspec(2),
                  pl.BlockSpec((None, D_MODEL, D_MODEL), lambda i: (l, 0, 0), pipeline_mode=resident),
                  pl.BlockSpec((tm, D_MODEL), lambda i: (i, 0)),
                  mspec,
                  pl.BlockSpec((1, D_MODEL), lambda i: (0, 0))],
        out_specs=pl.BlockSpec((tm, D_MODEL), lambda i: (i, 0)),
        compiler_params=_cparams(("parallel",)),
        name="merge_outproj",
    )(ya, yb, yc, u, u, u, w_branch_b, w_branch_b, w_branch_b, w_out_b, x2d, mod, final_g)


PV_MIX_R, PV_MIX_K, PV_MIX_V, PV_W0, PV_A0, PV_KK, PV_KA, PV_RK, PV_LNG, PV_LNB = range(10)


def _half_sum(x, lo_mask):
    s_lo = jnp.sum(jnp.where(lo_mask, x, 0.0), axis=-1, keepdims=True)
    s_hi = jnp.sum(jnp.where(lo_mask, 0.0, x), axis=-1, keepdims=True)
    return jnp.where(lo_mask, s_lo, s_hi)


def _rwkv_prompt_kernel(r_ref, k_ref, v_ref, g_ref, wd_ref, ad_ref, pv_ref, mixl_ref, w2_ref, a2_ref,
                        shr_ref, shk_ref, shv_ref, shwd_ref, shad_ref, s0_ref,
                        y_ref, sout_ref, xbuf, sb, *, n_chunks, pairs, nck):
    c = RWKV_CHUNK
    gw = pairs * LANE
    ti = pl.program_id(2)
    lo = lax.broadcasted_iota(jnp.int32, (1, LANE), 1) < HEAD_A

    @pl.when(ti == 0)
    def _():
        xbuf[0:SUBLANE, :] = jnp.concatenate(
            [shr_ref[...], shk_ref[...], shv_ref[...], shwd_ref[...], shad_ref[...]], axis=1)
        zero = jnp.zeros((HEAD_A, HEAD_A), F32)
        for g in range(pairs):
            top = jnp.concatenate([s0_ref[2 * g], zero], axis=1)
            bot = jnp.concatenate([zero, s0_ref[2 * g + 1]], axis=1)
            sb[g] = jnp.concatenate([top, bot], axis=0)

    tc = nck * c
    cur = jnp.concatenate([r_ref[...], k_ref[...], v_ref[...], wd_ref[...], ad_ref[...]], axis=1)
    xbuf[SUBLANE:SUBLANE + tc, :] = cur
    prev = xbuf[pl.ds(SUBLANE - 1, tc), :]
    pv = pv_ref[...]
    mixl = mixl_ref[...]
    mix = jnp.concatenate([pv[PV_MIX_R:PV_MIX_R + 1], pv[PV_MIX_K:PV_MIX_K + 1], pv[PV_MIX_V:PV_MIX_V + 1],
                           mixl[0:1], mixl[1:2]], axis=1)
    xbuf[0:SUBLANE, :] = cur[tc - SUBLANE:tc, :]

    w0, a0 = pv[PV_W0:PV_W0 + 1], pv[PV_A0:PV_A0 + 1]
    k_k, k_a, r_k = pv[PV_KK:PV_KK + 1], pv[PV_KA:PV_KA + 1], pv[PV_RK:PV_RK + 1]
    ln_g, ln_b = pv[PV_LNG:PV_LNG + 1], pv[PV_LNB:PV_LNB + 1]
    rowc, colc = _iota2(c, c)
    lower_01 = (rowc >= colc).astype(BF16)

    n = 2 * c
    row, col = _iota2(n, n)
    same_head = (row >> 6) == (col >> 6)
    strict = same_head & (row > col)
    incl = same_head & (row >= col)
    eye = row == col
    mm = functools.partial(_dot, mode=P_CHUNK)

    def stack(x):
        return jnp.concatenate([jnp.where(lo, x, 0.0), jnp.where(lo, 0.0, x)], axis=0)

    gs = range(pairs)
    sls = [slice(g * LANE, (g + 1) * LANE) for g in gs]
    state = {"s": [sb[g] for g in gs]}
    out_rows = [None] * nck

    def chunk_stages(k):
        rows = slice(k * c, (k + 1) * c)
        cur_k = cur[rows]
        z = cur_k + (prev[rows] - cur_k) * mix
        zr, zk, zv = z[:, 0:gw], z[:, gw:2 * gw], z[:, 2 * gw:3 * gw]
        zwd, zad = z[:, 3 * gw:3 * gw + LANE], z[:, 3 * gw + LANE:]
        logw = -_softplus(-(w0 + _dot(jnp.tanh(zwd), w2_ref[...], mode=P_LORA))) - 0.5
        lw = -jnp.exp(logw)
        a = _sigmoid(a0 + _dot(zad, a2_ref[...], mode=P_LORA))
        kkp = zk * k_k
        kmod = zk * (1.0 + (a - 1.0) * k_a)
        yield
        gcum = _dot_exact_lhs(lower_01, lw)
        g_end = gcum[c - 1:c, :]
        e_pos, e_neg = jnp.exp(gcum), jnp.exp(-gcum)
        e_prev, e_end = jnp.exp(gcum - lw), jnp.exp(g_end - gcum)
        eg_end = jnp.exp(g_end)
        rt, kb, kend = zr * e_pos, kmod * e_neg, kmod * e_end
        yield
        kk = [kkp[:, sl] * lax.rsqrt(_half_sum(kkp[:, sl] * kkp[:, sl], lo) + 1e-6) for sl in sls]
        bt = [kk[g] * a[:, sls[g]] for g in gs]
        rt_s = [stack(rt[:, sl]) for sl in sls]
        kt_s = [stack(kk[g] * e_prev[:, sls[g]]) for g in gs]
        kb_s = [jnp.concatenate([kb[:, sl], kb[:, sl]], axis=0) for sl in sls]
        bb_s = [jnp.concatenate([bt[g] * e_neg[:, sls[g]]] * 2, axis=0) for g in gs]
        yield
        kend_s = [stack(kend[:, sl]) for sl in sls]
        bend_s = [stack(bt[g] * e_end[:, sls[g]]) for g in gs]
        v_s = [stack(zv[:, sl]) for sl in sls]
        p_all = [mm(jnp.concatenate([kt_s[g], rt_s[g]], axis=0), jnp.concatenate([bb_s[g], kb_s[g]], axis=0), _NT)
                 for g in gs]
        yield
        a_ub = [jnp.where(strict, x[0:n, 0:n], 0.0) for x in p_all]
        a_vk = [jnp.where(strict, x[0:n, n:2 * n], 0.0) for x in p_all]
        p_rb = [jnp.where(incl, x[n:2 * n, 0:n], 0.0) for x in p_all]
        p_rk = [jnp.where(incl, x[n:2 * n, n:2 * n], 0.0) for x in p_all]
        av = [mm(a_vk[g], v_s[g]) for g in gs]
        pv_ = [mm(p_rk[g], v_s[g]) for g in gs]
        t_inv = yield from _inv_unit_lower(a_ub, n, levels=2)
        yield
        m12 = [mm(t_inv[g], jnp.concatenate([kt_s[g], av[g]], axis=1)) for g in gs]
        yield
        w2m = [mm(p_rb[g], m12[g]) for g in gs]
        q1 = [rt_s[g] - w2m[g][:, 0:LANE] for g in gs]
        q2 = [pv_[g] - w2m[g][:, LANE:2 * LANE] for g in gs]
        phi = [jnp.where(eye, eg_end[:, sls[g]], 0.0) - mm(bend_s[g], m12[g][:, 0:LANE], _TN) for g in gs]
        psi_t = [mm(jnp.concatenate([v_s[g], m12[g][:, LANE:2 * LANE]], axis=0),
                    jnp.concatenate([kend_s[g], -bend_s[g]], axis=0), _TN) for g in gs]
        yield
        s_old = state["s"]
        y_s = [_dot(q1[g], s_old[g], _NT, P_STATE) + q2[g] for g in gs]
        state["s"] = [_dot(s_old[g], phi[g], _NT, P_STATE) + psi_t[g] for g in gs]
        yield
        gate = _silu(g_ref[rows, :])
        rk_prod = zr * kmod * r_k
        outs = []
        for g in gs:
            sl = sls[g]
            y2 = y_s[g][0:c, :] + y_s[g][c:n, :]
            mu = _half_sum(y2, lo) * (1.0 / HEAD_A)
            dlt = y2 - mu
            var = _half_sum(dlt * dlt, lo) * (1.0 / HEAD_A)
            yn = dlt * lax.rsqrt(var + GN_EPS_A) * ln_g[:, sl] + ln_b[:, sl]
            bonus = _half_sum(rk_prod[:, sl], lo) * zv[:, sl]
            outs.append((yn + bonus) * gate[:, sl])
        out_rows[k] = jnp.concatenate(outs, axis=1)

    _interleave([chunk_stages(k) for k in range(nck)], lead=RWKV_LEAD)
    for g in gs:
        sb[g] = state["s"][g]
    y_ref[...] = jnp.concatenate(out_rows, axis=0).astype(y_ref.dtype)

    @pl.when(ti == n_chunks - 1)
    def _():
        for g in range(pairs):
            s_fin = sb[g]
            sout_ref[2 * g] = s_fin[0:HEAD_A, 0:HEAD_A]
            sout_ref[2 * g + 1] = s_fin[HEAD_A:2 * HEAD_A, HEAD_A:2 * HEAD_A]


def _rwkv_prompt(u, nb, t, pvec, mixl, w2p, a2p, sh_init, s0, l, pairs=8, nck=2):
    c = nck * RWKV_CHUNK
    nt = t // c
    gw = pairs * LANE
    groups = (H_A // 2) // pairs

    def uspec(off):
        return pl.BlockSpec((c, gw), lambda b, p, i: (b * nt + i, off // gw + p))

    def lspec(off):
        return pl.BlockSpec((c, LANE), lambda b, p, i: (b * nt + i, off // LANE))

    def shspec(off):
        return pl.BlockSpec((None, SUBLANE, gw), lambda b, p, i: (b, 0, off // gw + p))

    def shlspec(off):
        return pl.BlockSpec((None, SUBLANE, LANE), lambda b, p, i: (b, 0, off // LANE))

    group_cols = lambda rows: pl.BlockSpec((None, rows, gw), lambda b, p, i: (l, 0, p))
    return pl.pallas_call(
        functools.partial(_rwkv_prompt_kernel, n_chunks=nt, pairs=pairs, nck=nck),
        out_shape=(jax.ShapeDtypeStruct((nb * t, MIX_W), BF16),
                   jax.ShapeDtypeStruct((nb, H_A, HEAD_A, HEAD_A), F32)),
        grid=(nb, groups, nt),
        in_specs=[uspec(OFF_R), uspec(OFF_K), uspec(OFF_V), uspec(OFF_AG), lspec(OFF_WD), lspec(OFF_AD),
                  group_cols(16),
                  pl.BlockSpec((None, SUBLANE, LANE), lambda b, p, i: (l, 0, 0)),
                  group_cols(LANE), group_cols(LANE),
                  shspec(0), shspec(MIX_W), shspec(2 * MIX_W), shlspec(3 * MIX_W), shlspec(3 * MIX_W + LANE),
                  pl.BlockSpec((None, 2 * pairs, HEAD_A, HEAD_A), lambda b, p, i: (b, p, 0, 0))],
        out_specs=(pl.BlockSpec((c, gw), lambda b, p, i: (b * nt + i, p)),
                   pl.BlockSpec((None, 2 * pairs, HEAD_A, HEAD_A), lambda b, p, i: (b, p, 0, 0))),
        scratch_shapes=[pltpu.VMEM((c + SUBLANE, 3 * gw + 2 * LANE), F32),
                        pltpu.VMEM((pairs, 2 * HEAD_A, 2 * HEAD_A), F32)],
        compiler_params=_cparams(("parallel", "parallel", "arbitrary")),
        name="rwkv_prompt",
    )(u, u, u, u, u, u, pvec, mixl, w2p, a2p, sh_init, sh_init, sh_init, sh_init, sh_init, s0)


TB_R, TB_W, TB_KKP, TB_A, TB_KM, TB_V, TB_GATE = range(7)
PT_RK, PT_LNG, PT_LNB = range(3)


def _rwkv_step_kernel(rkv_ref, g_ref, wd_ref, ad_ref, prev_ref, mix_ref, pv_ref, w2_ref, a2_ref, pt_ref,
                      s_ref, acc_ref, y_ref, sout_ref, tab, ybuf):
    del acc_ref
    h = pl.program_id(0)

    @pl.when(h == 0)
    def _():
        cur = jnp.concatenate([rkv_ref[...], wd_ref[...], ad_ref[...]], axis=1)
        z = cur + (prev_ref[...] - cur) * mix_ref[...]
        zr, zk, zv = z[:, 0:MIX_W], z[:, MIX_W:2 * MIX_W], z[:, 2 * MIX_W:3 * MIX_W]
        zwd, zad = z[:, 3 * MIX_W:3 * MIX_W + LANE], z[:, 3 * MIX_W + LANE:]
        pv = pv_ref[...]
        logw = -_softplus(-(pv[PV_W0:PV_W0 + 1] + _dot(jnp.tanh(zwd), w2_ref[...], mode=P_STEP))) - 0.5
        a = _sigmoid(pv[PV_A0:PV_A0 + 1] + _dot(zad, a2_ref[...], mode=P_STEP))
        tab[TB_R] = zr.T
        tab[TB_W] = jnp.exp(-jnp.exp(logw)).T
        tab[TB_KKP] = (zk * pv[PV_KK:PV_KK + 1]).T
        tab[TB_A] = a.T
        tab[TB_KM] = (zk * (1.0 + (a - 1.0) * pv[PV_KA:PV_KA + 1])).T
        tab[TB_V] = zv.T
        tab[TB_GATE] = _silu(g_ref[...]).T

    rows = pl.ds(pl.multiple_of(h * HEAD_A, HEAD_A), HEAD_A)
    r_t, w_t, km_t = tab[TB_R, rows, :], tab[TB_W, rows, :], tab[TB_KM, rows, :]
    kkp_t = tab[TB_KKP, rows, :]
    kk_t = kkp_t * lax.rsqrt(jnp.sum(kkp_t * kkp_t, axis=0, keepdims=True) + 1e-6)
    bt_t = kk_t * tab[TB_A, rows, :]
    v_t = tab[TB_V, rows, :]
    for v in range(HEAD_A):
        s = s_ref[v]
        sa = jnp.sum(s * kk_t, axis=0, keepdims=True)
        s_new = s * w_t - sa * bt_t + v_t[v:v + 1, :] * km_t
        sout_ref[v] = s_new
        ybuf[v:v + 1, :] = jnp.sum(s_new * r_t, axis=0, keepdims=True)
    y = ybuf[...]
    mu = jnp.mean(y, axis=0, keepdims=True)
    dlt = y - mu
    var = jnp.mean(dlt * dlt, axis=0, keepdims=True)
    yn = dlt * lax.rsqrt(var + GN_EPS_A) * pt_ref[PT_LNG] + pt_ref[PT_LNB]
    bonus = jnp.sum(r_t * km_t * pt_ref[PT_RK], axis=0, keepdims=True) * v_t
    y_ref[...] = (yn + bonus) * tab[TB_GATE, rows, :]


def _rwkv_step(u, prev_pad, mix_pad, pvec, w2p, a2p, ptab, s_all, s_acc, l):
    nb = u.shape[0]
    lfull = lambda shape: pl.BlockSpec((None,) + shape, lambda h: (l,) + tuple(0 for _ in shape))
    st_spec = pl.BlockSpec((None, None, HEAD_A, HEAD_A, nb), lambda h: (l, h, 0, 0, 0))
    return pl.pallas_call(
        _rwkv_step_kernel,
        out_shape=(jax.ShapeDtypeStruct((MIX_W, nb), F32),
                   jax.ShapeDtypeStruct(s_acc.shape, F32)),
        grid=(H_A,),
        in_specs=[pl.BlockSpec((nb, 3 * MIX_W), lambda h: (0, 0)),
                  pl.BlockSpec((nb, MIX_W), lambda h: (0, OFF_AG // MIX_W)),
                  pl.BlockSpec((nb, LANE), lambda h: (0, OFF_WD // LANE)),
                  pl.BlockSpec((nb, LANE), lambda h: (0, OFF_AD // LANE)),
                  lfull((nb, SHIFT_PAD)),
                  lfull((1, SHIFT_PAD)), lfull((16, MIX_W)), lfull((LANE, MIX_W)), lfull((LANE, MIX_W)),
                  pl.BlockSpec((None, 3, HEAD_A, LANE), lambda h: (l, 0, h, 0)),
                  st_spec,
                  pl.BlockSpec(memory_space=pl.ANY)],
        out_specs=(pl.BlockSpec((HEAD_A, nb), lambda h: (h, 0)), st_spec),
        scratch_shapes=[pltpu.VMEM((7, MIX_W, nb), F32), pltpu.VMEM((HEAD_A, nb), F32)],
        input_output_aliases={11: 1},
        compiler_params=_cparams(("arbitrary",)),
        name="rwkv_step",
    )(u, u, u, u, prev_pad, mix_pad, pvec, w2p, a2p, ptab, s_all, s_acc)


LP_CONVB, LP_BA, LP_BX, LP_LAM = range(4)


def _lru_gates(xc, wa_ref, wx_ref, lp):
    r = _sigmoid(_mm_bf16(xc, wa_ref[...]) + lp[LP_BA:LP_BA + 1])
    i = _sigmoid(_mm_bf16(xc, wx_ref[...]) + lp[LP_BX:LP_BX + 1])
    log_a = -LRU_C * r * _softplus(-lp[LP_LAM:LP_LAM + 1])
    a = jnp.exp(log_a)
    t = jnp.tanh(-log_a) * (a * a + 1.0)
    b = jnp.where(t > 0.0, t * lax.rsqrt(t), 0.0) * (i * xc)
    return a, b


def _lru_prompt_kernel(x_ref, g_ref, cw_ref, lp_ref, wa_ref, wx_ref, cinit_ref, h0_ref,
                       y_ref, hout_ref, xbuf, hcar, *, n_blocks, tc):
    ti = pl.program_id(1)

    @pl.when(ti == 0)
    def _():
        xbuf[0:SUBLANE, :] = cinit_ref[...]
        hcar[...] = h0_ref[...]

    x = x_ref[...]
    xbuf[SUBLANE:SUBLANE + tc, :] = x
    cw = cw_ref[...]
    lp = lp_ref[...]
    xc = (xbuf[pl.ds(SUBLANE - 3, tc), :] * cw[0:1] + xbuf[pl.ds(SUBLANE - 2, tc), :] * cw[1:2]
          + xbuf[pl.ds(SUBLANE - 1, tc), :] * cw[2:3] + x * cw[3:4] + lp[LP_CONVB:LP_CONVB + 1])
    xbuf[0:SUBLANE, :] = x[tc - SUBLANE:tc, :]
    a, b = _lru_gates(xc, wa_ref, wx_ref, lp)
    n_tiles = tc // SUBLANE
    a = a.reshape(n_tiles, SUBLANE, MIX_W)
    b = b.reshape(n_tiles, SUBLANE, MIX_W)
    sub = lax.broadcasted_iota(jnp.int32, (n_tiles, SUBLANE, MIX_W), 1)
    s = 1
    while s < SUBLANE:
        a_sh = pltpu.roll(a, s, 1)
        b_sh = pltpu.roll(b, s, 1)
        m = sub >= s
        b = jnp.where(m, a * b_sh + b, b)
        a = jnp.where(m, a * a_sh, a)
        s *= 2
    carry = hcar[...]
    tiles = []
    for j in range(n_tiles):
        h_j = b[j] + a[j] * carry
        carry = h_j[SUBLANE - 1:SUBLANE, :]
        tiles.append(h_j)
    h = jnp.concatenate(tiles, axis=0)
    hcar[...] = carry
    y_ref[...] = (h * _silu(g_ref[...])).astype(y_ref.dtype)

    @pl.when(ti == n_blocks - 1)
    def _():
        hout_ref[...] = h[tc - 1:tc, :]


def _lru_prompt(u, nb, t, conv_w, lp, wa_d, wx_d, cinit, h0, l, tc=256):
    nt = t // tc
    const = lambda shape: pl.BlockSpec((None,) + shape, lambda b, i: (l,) + tuple(0 for _ in shape))
    return pl.pallas_call(
        functools.partial(_lru_prompt_kernel, n_blocks=nt, tc=tc),
        out_shape=(jax.ShapeDtypeStruct((nb * t, MIX_W), BF16),
                   jax.ShapeDtypeStruct((nb, 1, MIX_W), F32)),
        grid=(nb, nt),
        in_specs=[pl.BlockSpec((tc, MIX_W), lambda b, i: (b * nt + i, OFF_BX // MIX_W)),
                  pl.BlockSpec((tc, MIX_W), lambda b, i: (b * nt + i, OFF_BG // MIX_W)),
                  const((CONV_W, MIX_W)), const((SUBLANE, MIX_W)),
                  const((MIX_W, MIX_W)), const((MIX_W, MIX_W)),
                  pl.BlockSpec((None, SUBLANE, MIX_W), lambda b, i: (b, 0, 0)),
                  pl.BlockSpec((None, 1, MIX_W), lambda b, i: (b, 0, 0))],
        out_specs=(pl.BlockSpec((tc, MIX_W), lambda b, i: (b * nt + i, 0)),
                   pl.BlockSpec((None, 1, MIX_W), lambda b, i: (b, 0, 0))),
        scratch_shapes=[pltpu.VMEM((tc + SUBLANE, MIX_W), F32), pltpu.VMEM((1, MIX_W), F32)],
        compiler_params=_cparams(("parallel", "arbitrary")),
        name="lru_prompt",
    )(u, u, conv_w, lp, wa_d, wx_d, cinit, h0)


def _lru_step_kernel(x_ref, g_ref, buf_ref, h0_ref, cw_ref, lp_ref, wa_ref, wx_ref, y_ref, hout_ref):
    x = x_ref[...]
    cw = cw_ref[...]
    lp = lp_ref[...]
    xc = buf_ref[0] * cw[0:1] + buf_ref[1] * cw[1:2] + buf_ref[2] * cw[2:3] + x * cw[3:4] + lp[LP_CONVB:LP_CONVB + 1]
    a, b = _lru_gates(xc, wa_ref, wx_ref, lp)
    h = a * h0_ref[...] + b
    hout_ref[...] = h
    y_ref[...] = h * _silu(g_ref[...])


def _lru_step(u, buf_t, h0, conv_w, lp, wa_d, wx_d, l):
    nb = u.shape[0]
    const = lambda shape: pl.BlockSpec((None,) + shape, lambda i: (l,) + tuple(0 for _ in shape))
    out = pl.BlockSpec((nb, MIX_W), lambda i: (0, 0))
    return pl.pallas_call(
        _lru_step_kernel,
        out_shape=(jax.ShapeDtypeStruct((nb, MIX_W), F32), jax.ShapeDtypeStruct((nb, MIX_W), F32)),
        grid=(1,),
        in_specs=[pl.BlockSpec((nb, MIX_W), lambda i: (0, OFF_BX // MIX_W)),
                  pl.BlockSpec((nb, MIX_W), lambda i: (0, OFF_BG // MIX_W)),
                  const((CONV_W - 1, nb, MIX_W)), const((nb, MIX_W)),
                  const((CONV_W, MIX_W)), const((SUBLANE, MIX_W)),
                  const((MIX_W, MIX_W)), const((MIX_W, MIX_W))],
        out_specs=(out, out),
        compiler_params=_cparams(("arbitrary",)),
        name="lru_step",
    )(u, u, buf_t, h0, conv_w, lp, wa_d, wx_d)


GP_ALOG, GP_DTB, GP_NORM = range(3)


def _gdn_prompt_kernel(q_ref, k_ref, v_ref, z_ref, ab_ref, cwq_ref, cwk_ref, cwv_ref, gp_ref,
                       ciq_ref, cik_ref, civ_ref, s0_ref, y_ref, sout_ref, xbuf, st, *, n_chunks, heads, nck):
    c = GDN_CHUNK
    tc = nck * c
    hw = heads * LANE
    h_base = pl.program_id(1) * heads
    ti = pl.program_id(2)

    @pl.when(ti == 0)
    def _():
        xbuf[0:SUBLANE, :] = jnp.concatenate([ciq_ref[...], cik_ref[...], civ_ref[...]], axis=1)
        st[...] = s0_ref[...]

    cur = jnp.concatenate([q_ref[...], k_ref[...], v_ref[...]], axis=1)
    xbuf[SUBLANE:SUBLANE + tc, :] = cur
    cw = jnp.concatenate([cwq_ref[...], cwk_ref[...], cwv_ref[...]], axis=1)

    gp = gp_ref[...]
    norm_g = gp[GP_NORM:GP_NORM + 1]
    lane_c = lax.broadcasted_iota(jnp.int32, (c, LANE), 1)
    row, col = _iota2(c, c)
    incl = row >= col
    strict = row > col
    eye = row == col
    incl_01 = incl.astype(BF16)
    mm = functools.partial(_dot, mode=P_CHUNK)
    hs_ = range(heads)
    state = {"s": [st[j] for j in hs_]}
    out_rows = [None] * nck

    def chunk_stages(k):
        base = SUBLANE + k * c
        x0 = cur[k * c:(k + 1) * c]
        conv = (xbuf[pl.ds(base - 3, c), :] * cw[0:1] + xbuf[pl.ds(base - 2, c), :] * cw[1:2]
                + xbuf[pl.ds(base - 1, c), :] * cw[2:3] + x0 * cw[3:4])
        act = _silu(conv)
        yield
        ab = ab_ref[k * c:(k + 1) * c, :]
        g_all = -jnp.exp(gp[GP_ALOG:GP_ALOG + 1]) * _softplus(ab + gp[GP_DTB:GP_DTB + 1])
        beta_all = _sigmoid(ab)
        qh = [act[:, j * DK:(j + 1) * DK] for j in hs_]
        kh = [act[:, hw + j * DK:hw + (j + 1) * DK] for j in hs_]
        vh = [act[:, 2 * hw + j * DV:2 * hw + (j + 1) * DV] for j in hs_]
        qn = [x * lax.rsqrt(jnp.sum(x * x, axis=-1, keepdims=True) + 1e-6) * (DK ** -0.5) for x in qh]
        kn = [x * lax.rsqrt(jnp.sum(x * x, axis=-1, keepdims=True) + 1e-6) for x in kh]
        beta = [jnp.sum(jnp.where(lane_c == h_base + j + H_C, beta_all, 0.0), axis=-1, keepdims=True) for j in hs_]
        yield
        gc_all = _dot_exact_lhs(incl_01, g_all)
        gc = [jnp.broadcast_to(jnp.sum(jnp.where(lane_c == h_base + j, gc_all, 0.0), axis=-1, keepdims=True),
                               (c, LANE)) for j in hs_]
        gc_row = [jnp.sum(jnp.where(eye, x, 0.0), axis=0, keepdims=True) for x in gc]
        decay = [jnp.where(incl, jnp.exp(jnp.where(incl, gc[j] - gc_row[j], 0.0)), 0.0) for j in hs_]
        g_last = [x[c - 1:c, :] for x in gc]
        eg = [jnp.exp(x) for x in gc]
        yield
        kb = [kn[j] * beta[j] for j in hs_]
        p_all = [mm(jnp.concatenate([kb[j], qn[j]], axis=0), kn[j], _NT) for j in hs_]
        yield
        a_mat = [jnp.where(strict, p_all[j][0:c] * decay[j], 0.0) for j in hs_]
        qk = [p_all[j][c:2 * c] * decay[j] for j in hs_]
        k_dec = [kn[j] * jnp.exp(g_last[j] - gc[j]) for j in hs_]
        rhs = [jnp.concatenate([kb[j] * eg[j], vh[j] * beta[j]], axis=1) for j in hs_]
        t_inv = yield from _inv_unit_lower(a_mat, c, levels=3)
        yield
        wu = [mm(t_inv[j], rhs[j]) for j in hs_]
        yield
        w3 = [mm(k_dec[j], wu[j], _TN) for j in hs_]
        w4 = [mm(qk[j], wu[j]) for j in hs_]
        phi = [jnp.where(eye, jnp.exp(g_last[j]), 0.0) - w3[j][:, 0:DK] for j in hs_]
        q1 = [qn[j] * eg[j] - w4[j][:, 0:DK] for j in hs_]
        yield
        s_old = state["s"]
        o = [_dot(q1[j], s_old[j], _NN, P_STATE) + w4[j][:, DK:] for j in hs_]
        state["s"] = [_dot(phi[j], s_old[j], _NN, P_STATE) + w3[j][:, DK:] for j in hs_]
        yield
        zg = _silu(z_ref[k * c:(k + 1) * c, :])
        outs = [o[j] * lax.rsqrt(jnp.mean(o[j] * o[j], axis=-1, keepdims=True) + NORM_EPS) * norm_g
                * zg[:, j * DV:(j + 1) * DV] for j in hs_]
        out_rows[k] = jnp.concatenate(outs, axis=1)

    _interleave([chunk_stages(k) for k in range(nck)], lead=GDN_LEAD)
    xbuf[0:SUBLANE, :] = cur[tc - SUBLANE:tc, :]
    for j in hs_:
        st[j] = state["s"][j]
    y_ref[...] = jnp.concatenate(out_rows, axis=0).astype(y_ref.dtype)

    @pl.when(ti == n_chunks - 1)
    def _():
        sout_ref[...] = st[...]


def _gdn_prompt(u, nb, t, conv_w, gpar, cinit, s0, l, heads=8, nck=2):
    assert GDN_CHUNK == LANE and DK == LANE and DV == LANE
    c = nck * GDN_CHUNK
    nt = t // c
    hw = heads * LANE

    def uspec(off):
        return pl.BlockSpec((c, hw), lambda b, h, i: (b * nt + i, off // hw + h))

    def cwspec(off):
        return pl.BlockSpec((None, CONV_W, hw), lambda b, h, i: (l, 0, off // hw + h))

    def cispec(off):
        return pl.BlockSpec((None, SUBLANE, hw), lambda b, h, i: (b, 0, off // hw + h))

    return pl.pallas_call(
        functools.partial(_gdn_prompt_kernel, n_chunks=nt, heads=heads, nck=nck),
        out_shape=(jax.ShapeDtypeStruct((nb * t, MIX_W), BF16),
                   jax.ShapeDtypeStruct((nb, H_C, DK, DV), F32)),
        grid=(nb, H_C // heads, nt),
        in_specs=[uspec(OFF_Q), uspec(OFF_Q + MIX_W), uspec(OFF_Q + 2 * MIX_W), uspec(OFF_Z),
                  pl.BlockSpec((c, LANE), lambda b, h, i: (b * nt + i, OFF_AB // LANE)),
                  cwspec(0), cwspec(MIX_W), cwspec(2 * MIX_W),
                  pl.BlockSpec((None, SUBLANE, LANE), lambda b, h, i: (l, 0, 0)),
                  cispec(0), cispec(MIX_W), cispec(2 * MIX_W),
                  pl.BlockSpec((None, heads, DK, DV), lambda b, h, i: (b, h, 0, 0))],
        out_specs=(pl.BlockSpec((c, hw), lambda b, h, i: (b * nt + i, h)),
                   pl.BlockSpec((None, heads, DK, DV), lambda b, h, i: (b, h, 0, 0))),
        scratch_shapes=[pltpu.VMEM((c + SUBLANE, 3 * hw), F32), pltpu.VMEM((heads, DK, DV), F32)],
        compiler_params=_cparams(("parallel", "parallel", "arbitrary")),
        name="gdn_prompt",
    )(u, u, u, u, u, conv_w, conv_w, conv_w, gpar, cinit, cinit, cinit, s0)


def _gdn_step_kernel(qkv_ref, z_ref, ab_ref, buf_ref, cw_ref, gp_ref, s_ref, acc_ref, y_ref, sout_ref, hs, *, bb):
    del acc_ref
    cw = cw_ref[...]
    cur = qkv_ref[...]
    act = _silu(buf_ref[0] * cw[0:1] + buf_ref[1] * cw[1:2] + buf_ref[2] * cw[2:3] + cur * cw[3:4])
    gp = gp_ref[...]
    ab = ab_ref[...]
    g_all = -jnp.exp(gp[GP_ALOG:GP_ALOG + 1]) * _softplus(ab + gp[GP_DTB:GP_DTB + 1])
    beta_all = _sigmoid(ab)
    zg = _silu(z_ref[...])
    hs_q, hs_k, hs_v, hs_eg, hs_beta, hs_zg, hs_qk = range(7)
    for h in range(H_C):
        qh = act[:, h * DK:(h + 1) * DK]
        kh = act[:, MIX_W + h * DK:MIX_W + (h + 1) * DK]
        qn = qh * lax.rsqrt(jnp.sum(qh * qh, axis=-1, keepdims=True) + 1e-6) * (DK ** -0.5)
        kn = kh * lax.rsqrt(jnp.sum(kh * kh, axis=-1, keepdims=True) + 1e-6)
        hs[hs_q, h] = qn
        hs[hs_k, h] = kn
        hs[hs_v, h] = act[:, 2 * MIX_W + h * DV:2 * MIX_W + (h + 1) * DV]
        hs[hs_eg, h] = jnp.broadcast_to(jnp.exp(g_all[:, h:h + 1]), (bb, LANE))
        hs[hs_beta, h] = jnp.broadcast_to(beta_all[:, H_C + h:H_C + h + 1], (bb, LANE))
        hs[hs_zg, h] = zg[:, h * DV:(h + 1) * DV]
        hs[hs_qk, h] = jnp.broadcast_to(jnp.sum(qn * kn, axis=-1, keepdims=True), (bb, LANE))

    row, col = _iota2(DK, DV)
    diag = row == col
    rowb = lax.broadcasted_iota(jnp.int32, (bb, DV), 0)
    norm_g = gp[GP_NORM:GP_NORM + 1]
    heads = range(H_C)

    def body(b, outs):
        def rows(q):
            return [hs[q, h, pl.ds(b, 1), :] for h in heads]

        q, k, v, eg, beta = rows(hs_q), rows(hs_k), rows(hs_v), rows(hs_eg), rows(hs_beta)
        s = [s_ref[b, h] for h in heads]
        kcol = [jnp.sum(jnp.where(diag, jnp.broadcast_to(x, (DK, DV)), 0.0), axis=-1, keepdims=True) for x in k]
        qcol = [jnp.sum(jnp.where(diag, jnp.broadcast_to(x, (DK, DV)), 0.0), axis=-1, keepdims=True) for x in q]
        ks = [jnp.sum(kcol[h] * s[h], axis=0, keepdims=True) for h in heads]
        qs = [jnp.sum(qcol[h] * s[h], axis=0, keepdims=True) for h in heads]
        v_new = [beta[h] * (v[h] - eg[h] * ks[h]) for h in heads]
        for h in heads:
            sout_ref[b, h] = s[h] * eg[h] + kcol[h] * v_new[h]
        qk = rows(hs_qk)
        o = [eg[h] * qs[h] + qk[h] * v_new[h] for h in heads]
        ms = [jnp.mean(x * x, axis=-1, keepdims=True) for x in o]
        zg_h = rows(hs_zg)
        res = [o[h] * lax.rsqrt(ms[h] + NORM_EPS) * norm_g * zg_h[h] for h in heads]
        return tuple(jnp.where(rowb == b, jnp.broadcast_to(res[h], (bb, DV)), outs[h]) for h in heads)

    outs = lax.fori_loop(0, bb, body, tuple(jnp.zeros((bb, DV), F32) for _ in heads))
    y_ref[...] = jnp.concatenate(outs, axis=1)


def _gdn_step(u, buf_t, conv_w, gpar, s_all, s_acc, l, bb=SUBLANE):
    nb = u.shape[0]
    lfull = lambda shape: pl.BlockSpec((None,) + shape, lambda i: (l,) + tuple(0 for _ in shape))
    st_spec = pl.BlockSpec((None, bb, H_C, DK, DV), lambda i: (l, i, 0, 0, 0))
    return pl.pallas_call(
        functools.partial(_gdn_step_kernel, bb=bb),
        out_shape=(jax.ShapeDtypeStruct((nb, MIX_W), F32),
                   jax.ShapeDtypeStruct(s_acc.shape, F32)),
        grid=(nb // bb,),
        in_specs=[pl.BlockSpec((bb, QKV_W), lambda i: (i, OFF_Q // QKV_W)),
                  pl.BlockSpec((bb, MIX_W), lambda i: (i, OFF_Z // MIX_W)),
                  pl.BlockSpec((bb, LANE), lambda i: (i, OFF_AB // LANE)),
                  pl.BlockSpec((None, CONV_W - 1, bb, QKV_W), lambda i: (l, 0, i, 0)),
                  lfull((CONV_W, QKV_W)), lfull((SUBLANE, LANE)),
                  st_spec,
                  pl.BlockSpec(memory_space=pl.ANY)],
        out_specs=(pl.BlockSpec((bb, MIX_W), lambda i: (i, 0)), st_spec),
        scratch_shapes=[pltpu.VMEM((7, H_C, bb, LANE), F32)],
        input_output_aliases={7: 1},
        compiler_params=_cparams(("parallel",)),
        name="gdn_step",
    )(u, u, u, buf_t, conv_w, gpar, s_all, s_acc)


def _pad_cols(w, width):
    pad = width - w.shape[-1]
    return jnp.pad(w, [(0, 0)] * (w.ndim - 1) + [(0, pad)])


def _layout_in_cols(w):
    a_sh_end = A_SHIFT_W
    o_ag = a_sh_end
    o_bx = o_ag + MIX_W
    o_bg = o_bx + MIX_W
    o_q = o_bg + MIX_W
    o_a = o_q + QKV_W
    o_z = o_a + 2 * H_C
    o_m = o_z + MIX_W
    parts = [w[..., 0:3 * MIX_W], w[..., o_ag:o_bx], w[..., o_bx:o_bg], w[..., o_bg:o_q], w[..., o_q:o_a],
             w[..., o_z:o_m], w[..., o_m:IN_COLS],
             _pad_cols(w[..., 3 * MIX_W:3 * MIX_W + LORA], LANE),
             _pad_cols(w[..., 3 * MIX_W + LORA:a_sh_end], LANE),
             _pad_cols(w[..., o_a:o_z], LANE)]
    used = sum(part.shape[-1] for part in parts)
    parts.append(jnp.zeros(w.shape[:-1] + (N_PAD - used,), w.dtype))
    return jnp.concatenate(parts, axis=-1)


def _shift_to_padded(s):
    return jnp.concatenate([s[..., 0:3 * MIX_W], _pad_cols(s[..., 3 * MIX_W:3 * MIX_W + LORA], LANE),
                            _pad_cols(s[..., 3 * MIX_W + LORA:], LANE)], axis=-1)


def _shift_from_u(u_rows):
    return jnp.concatenate([u_rows[..., 0:3 * MIX_W], u_rows[..., OFF_WD:OFF_WD + LORA],
                            u_rows[..., OFF_AD:OFF_AD + LORA]], axis=-1)


def _block_diag(w):
    eye = jnp.eye(LRU_BLOCKS, dtype=w.dtype)
    return jnp.einsum('lnij,nm->lnimj', w, eye).reshape(w.shape[0], MIX_W, MIX_W)


def _rows_table(rows, n_rows):
    tab = jnp.stack(rows, axis=1)
    return jnp.pad(tab, ((0, 0), (0, n_rows - len(rows)), (0, 0)))


def _prep_params(p):
    depth = p['rwkv_mix'].shape[0]
    mix = p['rwkv_mix']
    pvec = _rows_table([mix[:, 0:MIX_W], mix[:, MIX_W:2 * MIX_W], mix[:, 2 * MIX_W:3 * MIX_W],
                        p['rwkv_w0'], p['rwkv_a0'], p['rwkv_kk'], p['rwkv_ka'], p['rwkv_rk'].reshape(depth, MIX_W),
                        p['rwkv_lnx_g'], p['rwkv_lnx_b']], 16)
    mix_pad = _shift_to_padded(mix)
    mixl = _rows_table([mix_pad[:, 3 * MIX_W:3 * MIX_W + LANE], mix_pad[:, 3 * MIX_W + LANE:]], SUBLANE)
    lora_pad = ((0, 0), (0, LANE - LORA), (0, 0))
    lp = _rows_table([p['lru_conv_b'], p['lru_ba'], p['lru_bx'], p['lru_lambda']], SUBLANE)
    gpar = _rows_table([_pad_cols(p['gdn_a_log'], LANE), _pad_cols(p['gdn_dt_bias'], LANE), p['gdn_norm_g']], SUBLANE)
    ptab = jnp.stack([p['rwkv_rk'].reshape(depth, MIX_W), p['rwkv_lnx_g'], p['rwkv_lnx_b']], axis=1)
    ptab = jnp.broadcast_to(ptab[..., None], (depth, 3, MIX_W, LANE))
    return dict(pvec=pvec, mix_pad=mix_pad[:, None], mixl=mixl, ptab=ptab,
                w2p=jnp.pad(p['rwkv_w2'], lora_pad), a2p=jnp.pad(p['rwkv_a2'], lora_pad),
                lru_cw=p['lru_conv_w'], lp=lp,
                wa_d=_block_diag(p['lru_wa']).astype(BF16), wx_d=_block_diag(p['lru_wx']).astype(BF16),
                gdn_cw=p['gdn_conv_w'], gpar=gpar)


def _prompt_layer(x2d, nb, t, l, pp, zeros, mod_p, norm_g3, w_in_p, w_branch_b, w_out_b, final_g, last):
    u = _inproj(x2d, mod_p, norm_g3, w_in_p, l, rows_per_mod=t, tm=PROMPT_TM, tn=INPROJ_TN)
    ya, wkv = _rwkv_prompt(u, nb, t, pp['pvec'], pp['mixl'], pp['w2p'], pp['a2p'], zeros['shift'], zeros['wkv'], l)
    yb, h_last = _lru_prompt(u, nb, t, pp['lru_cw'], pp['lp'], pp['wa_d'], pp['wx_d'],
                             zeros['lru_conv'], zeros['lru_h'], l)
    yc, ssm = _gdn_prompt(u, nb, t, pp['gdn_cw'], pp['gpar'], zeros['gdn_conv'], zeros['ssm'], l)
    x_new = _mergeout(ya, yb, yc, u, w_branch_b, w_out_b, x2d, mod_p, final_g, l, rows_per_mod=t, tm=MERGEOUT_TM,
                      final_norm=last)
    u3 = u.reshape(nb, t, N_PAD)
    states = (_shift_from_u(u3[:, t - 1]), wkv,
              u3[:, t - (CONV_W - 1):, OFF_BX:OFF_BX + MIX_W], h_last.reshape(nb, MIX_W),
              u3[:, t - (CONV_W - 1):, OFF_Q:OFF_Q + QKV_W], ssm)
    return x_new, states


def _sample_layer(x2d, l, pp, st, acc, mod_s, norm_g3, w_in_p, w_branch_b, w_out_b, final_g, last):
    nb = x2d.shape[0]
    u = _inproj(x2d, mod_s, norm_g3, w_in_p, l, rows_per_mod=1, tm=nb, tn=INPROJ_TN)
    ya_t, wkv_acc = _rwkv_step(u, st['shift'], pp['mix_pad'], pp['pvec'], pp['w2p'], pp['a2p'], pp['ptab'],
                               st['wkv_t'], acc[0], l)
    ya = ya_t.T
    yb, h_new = _lru_step(u, st['lru_conv_t'], st['lru_h'], pp['lru_cw'], pp['lp'], pp['wa_d'], pp['wx_d'], l)
    yc, ssm_acc = _gdn_step(u, st['gdn_conv_t'], pp['gdn_cw'], pp['gpar'], st['ssm'], acc[1], l)
    x_new = _mergeout(ya, yb, yc, u, w_branch_b, w_out_b, x2d, mod_s, final_g, l, rows_per_mod=1, tm=nb,
                      final_norm=last)
    rows = (_shift_from_u(u), u[:, OFF_BX:OFF_BX + MIX_W], h_new, u[:, OFF_Q:OFF_Q + QKV_W])
    return x_new, rows, (wkv_acc, ssm_acc)


def kernel(x_prompt, x_sample, state_rwkv_shift, state_rwkv_wkv, state_lru_conv, state_lru_h, state_gdn_conv, state_gdn_ssm, c_prompt, c_sample, ada_w, ada_b, norm_g, w_in, rwkv_mix, rwkv_w0, rwkv_w2, rwkv_a0, rwkv_a2, rwkv_kk, rwkv_ka, rwkv_rk, rwkv_lnx_g, rwkv_lnx_b, lru_conv_w, lru_conv_b, lru_wa, lru_ba, lru_wx, lru_bx, lru_lambda, gdn_conv_w, gdn_a_log, gdn_dt_bias, gdn_norm_g, w_branch, w_out, final_g):
    p = dict(rwkv_mix=rwkv_mix, rwkv_w0=rwkv_w0, rwkv_w2=rwkv_w2, rwkv_a0=rwkv_a0, rwkv_a2=rwkv_a2,
             rwkv_kk=rwkv_kk, rwkv_ka=rwkv_ka, rwkv_rk=rwkv_rk, rwkv_lnx_g=rwkv_lnx_g, rwkv_lnx_b=rwkv_lnx_b,
             lru_conv_w=lru_conv_w, lru_conv_b=lru_conv_b, lru_wa=lru_wa, lru_ba=lru_ba, lru_wx=lru_wx,
             lru_bx=lru_bx, lru_lambda=lru_lambda, gdn_conv_w=gdn_conv_w, gdn_a_log=gdn_a_log,
             gdn_dt_bias=gdn_dt_bias, gdn_norm_g=gdn_norm_g)
    depth = w_in.shape[0]
    bp, t, _ = x_prompt.shape
    bs = x_sample.shape[0]

    w_in_p = jnp.swapaxes(_layout_in_cols(w_in).astype(BF16), 1, 2)
    w_branch_b = w_branch.astype(BF16)
    w_out_b = w_out.astype(BF16)
    norm_g3 = norm_g.reshape(depth, 1, D_MODEL)

    rows = bp + bs
    rows_pad = -(-rows // SUBLANE) * SUBLANE
    c_all = jnp.pad(jnp.concatenate([c_prompt, c_sample], axis=0), ((0, rows_pad - rows), (0, 0)))
    mod = _ada_all(c_all, ada_w, ada_b)
    mod_p = mod[:, 0:bp].reshape(depth, bp, 1, 3 * D_MODEL)
    mod_s = mod[:, bp:bp + bs]

    pp = _prep_params(p)
    zeros = dict(shift=jnp.zeros((bp, SUBLANE, SHIFT_PAD), F32), wkv=jnp.zeros((bp, H_A, HEAD_A, HEAD_A), F32),
                 lru_conv=jnp.zeros((bp, SUBLANE, MIX_W), F32), lru_h=jnp.zeros((bp, 1, MIX_W), F32),
                 gdn_conv=jnp.zeros((bp, SUBLANE, QKV_W), F32), ssm=jnp.zeros((bp, H_C, DK, DV), F32))
    wkv_t = jnp.transpose(state_rwkv_wkv, (0, 2, 3, 4, 1))
    st = dict(shift=_shift_to_padded(state_rwkv_shift), wkv_t=wkv_t,
              lru_conv_t=jnp.swapaxes(state_lru_conv, 1, 2), lru_h=state_lru_h,
              gdn_conv_t=jnp.swapaxes(state_gdn_conv, 1, 2), ssm=state_gdn_ssm)
    acc = (jnp.zeros(wkv_t.shape, F32), jnp.zeros(state_gdn_ssm.shape, F32))

    xp = x_prompt.reshape(bp * t, D_MODEL)
    xs = x_sample.reshape(bs, D_MODEL)
    p_states, s_rows = [], []
    final_g2 = final_g.reshape(1, D_MODEL)
    for l in range(depth):
        last = l == depth - 1
        xp, pst = _prompt_layer(xp, bp, t, l, pp, zeros, mod_p, norm_g3, w_in_p, w_branch_b, w_out_b, final_g2, last)
        p_states.append(pst)
        xs, rows_l, acc = _sample_layer(xs, l, pp, st, acc, mod_s, norm_g3, w_in_p, w_branch_b, w_out_b,
                                        final_g2, last)
        s_rows.append(rows_l)

    y_prompt = xp.reshape(bp, t, D_MODEL)
    y_sample = xs.reshape(bs, 1, D_MODEL)
    p_out = tuple(jnp.stack([s[i] for s in p_states]) for i in range(6))
    s_shift, s_bx, s_h, s_qkv = (jnp.stack([r[i] for r in s_rows]) for i in range(4))
    s_lru_conv = jnp.concatenate([state_lru_conv[:, :, 1:], s_bx[:, :, None]], axis=2)
    s_gdn_conv = jnp.concatenate([state_gdn_conv[:, :, 1:], s_qkv[:, :, None]], axis=2)
    s_wkv = jnp.transpose(acc[0], (0, 4, 1, 2, 3))
    return (y_prompt, y_sample) + p_out + (s_shift, s_wkv, s_lru_conv, s_h, s_gdn_conv, acc[1])
```
